```python
import math
import jax, jax.numpy as jnp
from jax import lax
import numpy as np

D_MODEL = 1024
BATCH = 4
SEQ = 8192
DEPTH = 4
DEC_BATCH = 32
DEC_SEQ = 16
PAST_LEN = 4096

CHUNK = 64
CONV_WIDTH = 31
CONV_BUF = CONV_WIDTH - 1
D_CONV = 384
D_SSM = 384
SSM_GROUP = 16
N_SSM_GROUPS = D_SSM // SSM_GROUP
SSM_STATE = 64
N_MEM = 256
N_MEM_HEADS = 4
MEM_HEAD_DIM = 64
D_ATT = N_MEM_HEADS * MEM_HEAD_DIM
D_MIX = D_CONV + D_SSM + D_ATT
D_IN = 2 * D_CONV + D_SSM + D_ATT
N_EXPERTS = 16
N_EXPERT_GROUPS = 4
EXPERTS_PER_GROUP = N_EXPERTS // N_EXPERT_GROUPS
TOP_K = 2
D_EXPERT = 512
DEEPNORM_ALPHA = (2.0 * DEPTH) ** 0.25
DEEPNORM_BETA = (8.0 * DEPTH) ** -0.25
LN_EPS = 1e-5
DT_MIN = 1e-3
DT_MAX = 1e-1

kernel_name = "hymba_conformer_s5_moe_stream_step"


def layer_norm(x, g, b):
    xf = x.astype(jnp.float32)
    mu = jnp.mean(xf, axis=-1, keepdims=True)
    var = jnp.mean(jnp.square(xf - mu), axis=-1, keepdims=True)
    y = (xf - mu) * lax.rsqrt(var + LN_EPS) * g.astype(jnp.float32) + b.astype(jnp.float32)
    return y.astype(x.dtype)


def conv_group(u, buf, w_dw, b_dw, ln_g, ln_b):
    xpad = jnp.concatenate([buf.astype(u.dtype), u], axis=1)
    kern = w_dw.astype(u.dtype)[:, None, :]
    y = lax.conv_general_dilated(xpad, kern, window_strides=(1,), padding='VALID',
                                 dimension_numbers=('NWC', 'WIO', 'NWC'),
                                 feature_group_count=D_CONV) + b_dw
    new_buf = xpad[:, -CONV_BUF:]
    return jax.nn.silu(layer_norm(y, ln_g, ln_b)), new_buf


def ssm_group(u, h0_re, h0_im, a_re, a_im, b_re, b_im, c_re, c_im, d, log_dt, w_glu, b_glu):
    bsz, seq, _ = u.shape
    f32 = jnp.float32
    uf = u.astype(f32).reshape(bsz, seq, N_SSM_GROUPS, SSM_GROUP)
    a = lax.complex(a_re.astype(f32), a_im.astype(f32))
    dt = jnp.exp(log_dt.astype(f32))[:, None]
    a_bar = jnp.exp(a * dt)
    b_mat = lax.complex(b_re.astype(f32), b_im.astype(f32))
    b_bar = ((a_bar - 1.0) / a)[..., None] * b_mat
    c_mat = lax.complex(c_re.astype(f32), c_im.astype(f32))
    bu = jnp.einsum('blgi,gpi->blgp', uf.astype(jnp.complex64), b_bar)
    h0 = lax.complex(h0_re.astype(f32), h0_im.astype(f32))
    bu = bu.at[:, 0].add(a_bar * h0)
    a_seq = jnp.broadcast_to(a_bar, bu.shape)

    def combine(left, right):
        a_l, b_l = left
        a_r, b_r = right
        return a_l * a_r, a_r * b_l + b_r

    _, h = lax.associative_scan(combine, (a_seq, bu), axis=1)
    y = jnp.einsum('gip,blgp->blgi', c_mat, h).real + d.astype(f32).reshape(N_SSM_GROUPS, SSM_GROUP) * uf
    y = jax.nn.gelu(y.reshape(bsz, seq, D_SSM)).astype(u.dtype)
    y = y * jax.nn.sigmoid(y @ w_glu + b_glu)
    h_last = h[:, -1]
    return y, jnp.real(h_last), jnp.imag(h_last)


def mem_attention(q, mem_k, mem_v):
    s = jnp.einsum('blhd,bmhd->bhlm', q, mem_k).astype(jnp.float32) * (MEM_HEAD_DIM ** -0.5)
    p = jax.nn.softmax(s, axis=-1).astype(mem_v.dtype)
    o = jnp.einsum('bhlm,bmhd->blhd', p, mem_v)
    return o.reshape(q.shape[0], q.shape[1], D_ATT)


def moe(x, w_router, b_router, w_up, b_up, w_down):
    bsz, seq, dm = x.shape
    xt = x.reshape(bsz * seq, dm)
    logits = (xt @ w_router).astype(jnp.float32) + b_router.astype(jnp.float32)
    aff = jax.nn.softmax(logits, axis=-1)
    grouped = aff.reshape(-1, N_EXPERT_GROUPS, EXPERTS_PER_GROUP)
    group_score = jnp.sum(lax.top_k(grouped, TOP_K)[0], axis=-1)
    g_sel = jnp.argmax(group_score, axis=-1)
    g_mask = jax.nn.one_hot(g_sel, N_EXPERT_GROUPS, dtype=jnp.bool_)[..., None]
    masked = jnp.where(g_mask, grouped, -jnp.inf).reshape(-1, N_EXPERTS)
    top_vals, top_idx = lax.top_k(masked, TOP_K)
    gates = top_vals / jnp.sum(top_vals, axis=-1, keepdims=True)
    gate_dense = jnp.sum(jax.nn.one_hot(top_idx, N_EXPERTS, dtype=jnp.float32) * gates[..., None], axis=1)
    h = jax.nn.gelu(jnp.einsum('td,edf->tef', xt, w_up) + b_up) * gate_dense.astype(x.dtype)[..., None]
    y = jnp.einsum('tef,efd->td', h, w_down)
    return y.reshape(bsz, seq, dm)


def trunk_layer(x, conv_buf, h_re, h_im, mem_k, mem_v, p, w_router, b_router):
    bsz, seq, _ = x.shape
    proj = x @ p['w_in']
    conv_a = proj[..., :D_CONV]
    conv_g = proj[..., D_CONV:2 * D_CONV]
    ssm_u = proj[..., 2 * D_CONV:2 * D_CONV + D_SSM]
    q = proj[..., 2 * D_CONV + D_SSM:].reshape(bsz, seq, N_MEM_HEADS, MEM_HEAD_DIM)
    conv_y, new_buf = conv_group(conv_a * jax.nn.sigmoid(conv_g), conv_buf, p['w_dw'], p['b_dw'],
                                 p['conv_ln_g'], p['conv_ln_b'])
    ssm_y, new_re, new_im = ssm_group(ssm_u, h_re, h_im, p['a_re'], p['a_im'], p['b_re'], p['b_im'],
                                      p['c_re'], p['c_im'], p['d'], p['log_dt'], p['w_glu'], p['b_glu'])
    att_y = mem_attention(q, mem_k, mem_v)
    mix = jnp.concatenate([conv_y, ssm_y, att_y], axis=-1) @ p['w_out']
    x = layer_norm(DEEPNORM_ALPHA * x + mix, p['ln1_g'], p['ln1_b'])
    x = layer_norm(DEEPNORM_ALPHA * x + moe(x, w_router, b_router, p['w_up'], p['b_up'], p['w_down']),
                   p['ln2_g'], p['ln2_b'])
    return x, new_buf, new_re, new_im


def setup_inputs(seed: int = 0) -> dict:
    key = jax.random.key(seed)
    ks = jax.random.split(key, 40)
    f32 = jnp.float32

    def nrm(k, shape, scale):
        return jax.random.normal(k, shape, f32) * scale

    G, P, I = N_SSM_GROUPS, SSM_STATE, SSM_GROUP
    n_idx = jnp.arange(P, dtype=f32)
    return {
        "x_prompt": nrm(ks[0], (BATCH, SEQ, D_MODEL), 1.0),
        "x_sample": nrm(ks[1], (DEC_BATCH, DEC_SEQ, D_MODEL), 1.0),
        "cache_conv": nrm(ks[2], (DEPTH, DEC_BATCH, CONV_BUF, D_CONV), 0.5),
        "state_ssm_re": nrm(ks[3], (DEPTH, DEC_BATCH, G, P), 0.1),
        "state_ssm_im": nrm(ks[4], (DEPTH, DEC_BATCH, G, P), 0.1),
        "cache_mem_k": nrm(ks[5], (DEPTH, DEC_BATCH, N_MEM, N_MEM_HEADS, MEM_HEAD_DIM), 1.0),
        "cache_mem_v": nrm(ks[6], (DEPTH, DEC_BATCH, N_MEM, N_MEM_HEADS, MEM_HEAD_DIM), DEEPNORM_BETA),
        "mem_prompt": nrm(ks[7], (BATCH, N_MEM, D_MODEL), 1.0),
        "w_in": nrm(ks[8], (DEPTH, D_MODEL, D_IN), D_MODEL ** -0.5),
        "w_dw": nrm(ks[9], (DEPTH, CONV_WIDTH, D_CONV), CONV_WIDTH ** -0.5),
        "b_dw": nrm(ks[10], (DEPTH, D_CONV), 0.01),
        "conv_ln_g": 1.0 + nrm(ks[11], (DEPTH, D_CONV), 0.01),
        "conv_ln_b": nrm(ks[12], (DEPTH, D_CONV), 0.01),
        "ssm_a_re": -0.5 + nrm(ks[13], (DEPTH, G, P), 0.01),
        "ssm_a_im": jnp.pi * n_idx + nrm(ks[14], (DEPTH, G, P), 0.01),
        "ssm_b_re": nrm(ks[15], (DEPTH, G, P, I), (2.0 * I) ** -0.5),
        "ssm_b_im": nrm(ks[16], (DEPTH, G, P, I), (2.0 * I) ** -0.5),
        "ssm_c_re": nrm(ks[17], (DEPTH, G, I, P), P ** -0.5),
        "ssm_c_im": nrm(ks[18], (DEPTH, G, I, P), P ** -0.5),
        "ssm_d": nrm(ks[19], (DEPTH, D_SSM), 1.0),
        "ssm_log_dt": jax.random.uniform(ks[20], (DEPTH, G), f32, math.log(DT_MIN), math.log(DT_MAX)),
        "ssm_w_glu": nrm(ks[21], (DEPTH, D_SSM, D_SSM), D_SSM ** -0.5),
        "ssm_b_glu": nrm(ks[22], (DEPTH, D_SSM), 0.01),
        "w_mem_k": nrm(ks[23], (DEPTH, D_MODEL, D_ATT), D_MODEL ** -0.5),
        "w_mem_v": nrm(ks[24], (DEPTH, D_MODEL, D_ATT), D_MODEL ** -0.5 * DEEPNORM_BETA),
        "w_out": nrm(ks[25], (DEPTH, D_MIX, D_MODEL), D_MIX ** -0.5 * DEEPNORM_BETA),
        "ln1_g": 1.0 + nrm(ks[26], (DEPTH, D_MODEL), 0.01),
        "ln1_b": nrm(ks[27], (DEPTH, D_MODEL), 0.01),
        "w_router": nrm(ks[28], (D_MODEL, N_EXPERTS), D_MODEL ** -0.5),
        "b_router": nrm(ks[29], (N_EXPERTS,), 0.01),
        "w_up": nrm(ks[30], (DEPTH, N_EXPERTS, D_MODEL, D_EXPERT), D_MODEL ** -0.5),
        "b_up": nrm(ks[31], (DEPTH, N_EXPERTS, D_EXPERT), 0.01),
        "w_down": nrm(ks[32], (DEPTH, N_EXPERTS, D_EXPERT, D_MODEL), D_EXPERT ** -0.5 * DEEPNORM_BETA),
        "ln2_g": 1.0 + nrm(ks[33], (DEPTH, D_MODEL), 0.01),
        "ln2_b": nrm(ks[34], (DEPTH, D_MODEL), 0.01),
    }


def reference(x_prompt, x_sample, cache_conv, state_ssm_re, state_ssm_im, cache_mem_k, cache_mem_v,
              mem_prompt, w_in, w_dw, b_dw, conv_ln_g, conv_ln_b, ssm_a_re, ssm_a_im, ssm_b_re,
              ssm_b_im, ssm_c_re, ssm_c_im, ssm_d, ssm_log_dt, ssm_w_glu, ssm_b_glu, w_mem_k, w_mem_v,
              w_out, ln1_g, ln1_b, w_router, b_router, w_up, b_up, w_down, ln2_g, ln2_b):
    xp = x_prompt
    xs = x_sample
    bp = x_prompt.shape[0]
    mb, mm, _ = mem_prompt.shape
    conv_p, re_p, im_p, mk_p, mv_p = [], [], [], [], []
    conv_s, re_s, im_s = [], [], []
    for l in range(DEPTH):
        p = {
            'w_in': w_in[l], 'w_dw': w_dw[l], 'b_dw': b_dw[l],
            'conv_ln_g': conv_ln_g[l], 'conv_ln_b': conv_ln_b[l],
            'a_re': ssm_a_re[l], 'a_im': ssm_a_im[l], 'b_re': ssm_b_re[l], 'b_im': ssm_b_im[l],
            'c_re': ssm_c_re[l], 'c_im': ssm_c_im[l], 'd': ssm_d[l], 'log_dt': ssm_log_dt[l],
            'w_glu': ssm_w_glu[l], 'b_glu': ssm_b_glu[l], 'w_out': w_out[l],
            'ln1_g': ln1_g[l], 'ln1_b': ln1_b[l], 'w_up': w_up[l], 'b_up': b_up[l],
            'w_down': w_down[l], 'ln2_g': ln2_g[l], 'ln2_b': ln2_b[l],
        }
        mk = (mem_prompt @ w_mem_k[l]).reshape(mb, mm, N_MEM_HEADS, MEM_HEAD_DIM)
        mv = (mem_prompt @ w_mem_v[l]).reshape(mb, mm, N_MEM_HEADS, MEM_HEAD_DIM)
        zero_buf = jnp.zeros((bp, CONV_BUF, D_CONV), xp.dtype)
        zero_h = jnp.zeros((bp, N_SSM_GROUPS, SSM_STATE), jnp.float32)
        xp, cb, hr, hi = trunk_layer(xp, zero_buf, zero_h, zero_h, mk, mv, p, w_router, b_router)
        conv_p.append(cb); re_p.append(hr); im_p.append(hi); mk_p.append(mk); mv_p.append(mv)
        xs, cbs, hrs, his = trunk_layer(xs, cache_conv[l], state_ssm_re[l], state_ssm_im[l],
                                        cache_mem_k[l], cache_mem_v[l], p, w_router, b_router)
        conv_s.append(cbs); re_s.append(hrs); im_s.append(his)
    return (xp, xs, jnp.stack(conv_p), jnp.stack(re_p), jnp.stack(im_p), jnp.stack(mk_p),
            jnp.stack(mv_p), jnp.stack(conv_s), jnp.stack(re_s), jnp.stack(im_s))
```

```python
import functools
import math

import numpy as np
import jax
import jax.numpy as jnp
from jax import lax
from jax.experimental import pallas as pl
from jax.experimental.pallas import tpu as pltpu

F32 = jnp.float32
BF16 = jnp.bfloat16

D_MODEL = 1024
CONV_WIDTH = 31
CONV_BUF = CONV_WIDTH - 1
D_CONV = 384
D_SSM = 384
SSM_GROUP = 16
N_SSM_GROUPS = D_SSM // SSM_GROUP
SSM_STATE = 64
D_STATE = N_SSM_GROUPS * SSM_STATE
N_MEM = 256
N_MEM_HEADS = 4
MEM_HEAD_DIM = 64
D_ATT = N_MEM_HEADS * MEM_HEAD_DIM
D_IN = 2 * D_CONV + D_SSM + D_ATT
N_EXPERTS = 16
N_EXPERT_GROUPS = 4
EXPERTS_PER_GROUP = N_EXPERTS // N_EXPERT_GROUPS
D_EXPERT = 512
LN_EPS = 1e-5

SEQS_PER_STREAM = 4
HIST_ROWS = CONV_BUF * SEQS_PER_STREAM
SUBLANES = 8
SCAN_LANES = 512
CONV_ROWS = 32
VMEM_LIMIT_BYTES = 56 * 1024 * 1024


def _sigmoid(x):
    return 1.0 / (1.0 + jnp.exp(-x))


def _gelu_tanh(x):
    c = math.sqrt(2.0 / math.pi)
    return 0.5 * x * (1.0 + jnp.tanh(c * (x + 0.044715 * (x * x * x))))


def _layer_norm(z, g, b):
    mu = jnp.mean(z, axis=-1, keepdims=True)
    zc = z - mu
    var = jnp.mean(zc * zc, axis=-1, keepdims=True)
    return zc * lax.rsqrt(var + LN_EPS) * g + b


def _dot(a, b):
    return jnp.dot(a, b, preferred_element_type=F32)


def _mixer_kernel(x_ref, hist0_ref, h0_ref, k_ref, v_ref, perm_ref, permt_ref,
                  w_in_ref, wdw_ref, bdw_ref, clg_ref, clb_ref, a1_ref, a2_ref,
                  bre_ref, bim_ref, cre_ref, cim_ref, d_ref, wglu_ref, bglu_ref,
                  wout_ref, g1_ref, b1_ref,
                  x1_ref, hist_out_ref, h_out_ref,
                  xpad0, xpad4, cy, ush, yim, bu_re, bu_im, hre, him, hcar,
                  *, tm, lc, alpha):
    i = pl.program_id(1)

    @pl.when(i == 0)
    def _():
        xpad0[0:HIST_ROWS, :] = hist0_ref[0]
        hcar[...] = h0_ref[0]
        ush[...] = jnp.zeros_like(ush)

    x = x_ref[0]
    proj = _dot(x.astype(BF16), w_in_ref[...])

    g = proj[:, 0:D_CONV] * _sigmoid(proj[:, D_CONV:2 * D_CONV])
    xpad0[HIST_ROWS:HIST_ROWS + tm, :] = g
    xpad4[0:HIST_ROWS + tm - 4, :] = xpad0[4:HIST_ROWS + tm, :]

    nsub = CONV_ROWS // SUBLANES

    def conv_rows(rb, carry):
        r0 = pl.multiple_of(rb * CONV_ROWS, CONV_ROWS)
        accs = [jnp.broadcast_to(bdw_ref[...], (SUBLANES, D_CONV)) for _ in range(nsub)]
        for k in range(CONV_WIDTH):
            wk = wdw_ref[SUBLANES * k:SUBLANES * (k + 1), :]
            for sb in range(nsub):
                off = r0 + SEQS_PER_STREAM * k + SUBLANES * sb
                if k % 2 == 0:
                    xs = xpad0[pl.ds(pl.multiple_of(off, SUBLANES), SUBLANES), :]
                else:
                    xs = xpad4[pl.ds(pl.multiple_of(off - 4, SUBLANES), SUBLANES), :]
                accs[sb] = accs[sb] + xs * wk
        for sb in range(nsub):
            yn = _layer_norm(accs[sb], clg_ref[...], clb_ref[...])
            cy[pl.ds(pl.multiple_of(r0 + SUBLANES * sb, SUBLANES), SUBLANES), :] = yn * _sigmoid(yn)
        return carry

    lax.fori_loop(0, tm // CONV_ROWS, conv_rows, 0)

    new_hist = xpad0[tm:tm + HIST_ROWS, :]
    xpad0[0:HIST_ROWS, :] = new_hist
    hist_out_ref[0] = new_hist

    u = proj[:, 2 * D_CONV:2 * D_CONV + D_SSM]
    bu_re[...] = _dot(u.astype(BF16), bre_ref[...])
    ush[4:tm + 4, :] = u
    bu_im[...] = _dot(ush[...].astype(BF16), bim_ref[...])

    lo = lax.broadcasted_iota(jnp.int32, (SUBLANES, SCAN_LANES), 0) < 4
    for c in range(D_STATE // SCAN_LANES):
        cs = slice(c * SCAN_LANES, (c + 1) * SCAN_LANES)
        a1 = a1_ref[:, cs]
        a2 = a2_ref[:, cs]

        def scan_pair(j, carry, cs=cs, a1=a1, a2=a2):
            h_prev, im_cur = carry
            r = pl.multiple_of(j * SUBLANES, SUBLANES)
            re_cur = bu_re[pl.ds(r, SUBLANES), cs]
            im_next = bu_im[pl.ds(r + SUBLANES, SUBLANES), cs]
            p_even = jnp.where(lo, re_cur, im_cur)
            p_odd = jnp.where(lo, im_next, re_cur)
            h_even = a1 * pltpu.roll(h_prev, 4, 0) + a2 * h_prev + p_even
            h_odd = a1 * pltpu.roll(h_even, 4, 0) - a2 * h_even + p_odd
            hre[pl.ds(r, SUBLANES), cs] = jnp.where(lo, h_even, h_odd)
            him[pl.ds(r, SUBLANES), cs] = jnp.where(lo, h_prev, h_even)
            return h_odd, im_next

        h_last, _ = lax.fori_loop(0, lc // 2, scan_pair,
                                  (hcar[:, cs], bu_im[0:SUBLANES, cs]))
        him[tm:tm + SUBLANES, cs] = jnp.where(lo, h_last, 0.0)
        hcar[:, cs] = h_last
    h_out_ref[0] = hcar[...]

    y_re = _dot(hre[...].astype(BF16), cre_ref[...])
    y_im = _dot(him[...].astype(BF16), cim_ref[...])
    yim[...] = y_im
    y = y_re + yim[4:tm + 4, :] + d_ref[...] * u
    y = _gelu_tanh(y)
    ssm_y = y * _sigmoid(_dot(y.astype(BF16), wglu_ref[...]) + bglu_ref[...])

    q = proj[:, 2 * D_CONV + D_SSM:D_IN].astype(BF16)
    q_seq = _dot(perm_ref[...], q).astype(BF16)
    outs = []
    for s in range(SEQS_PER_STREAM):
        qs = q_seq[s * lc:(s + 1) * lc, :]
        acc = jnp.zeros((lc, D_ATT), F32)
        for h in range(N_MEM_HEADS):
            sc = _dot(qs, k_ref[0, s, h]) * (MEM_HEAD_DIM ** -0.5)
            sc = sc - jnp.max(sc, axis=-1, keepdims=True)
            e = jnp.exp(sc)
            p = e / jnp.sum(e, axis=-1, keepdims=True)
            acc = acc + _dot(p.astype(BF16), v_ref[0, s, h])
        outs.append(acc)
    att_seq = jnp.concatenate(outs, axis=0).astype(BF16)
    att = _dot(permt_ref[...], att_seq).astype(BF16)

    mix = (_dot(cy[...].astype(BF16), wout_ref[0:D_CONV, :])
           + _dot(ssm_y.astype(BF16), wout_ref[D_CONV:D_CONV + D_SSM, :])
           + _dot(att, wout_ref[D_CONV + D_SSM:D_MODEL, :]))
    x1_ref[0] = _layer_norm(alpha * x + mix, g1_ref[...], b1_ref[...])


def _mixer_call(x, hist0, h0, kpad, vpad, perm, permt, lw, *, lc, alpha):
    nq, rows, _ = x.shape
    tm = lc * SEQS_PER_STREAM
    nt = rows // tm
    kern = functools.partial(_mixer_kernel, tm=tm, lc=lc, alpha=alpha)

    def const(shape):
        return pl.BlockSpec(shape, lambda q, i: (0,) * len(shape))

    in_specs = [
        pl.BlockSpec((1, tm, D_MODEL), lambda q, i: (q, i, 0)),
        pl.BlockSpec((1, HIST_ROWS, D_CONV), lambda q, i: (q, 0, 0)),
        pl.BlockSpec((1, SUBLANES, D_STATE), lambda q, i: (q, 0, 0)),
        pl.BlockSpec((1, SEQS_PER_STREAM, N_MEM_HEADS, D_ATT, N_MEM), lambda q, i: (q, 0, 0, 0, 0)),
        pl.BlockSpec((1, SEQS_PER_STREAM, N_MEM_HEADS, N_MEM, D_ATT), lambda q, i: (q, 0, 0, 0, 0)),
        const((tm, tm)), const((tm, tm)),
        const((D_MODEL, D_IN)),
        const((CONV_WIDTH * SUBLANES, D_CONV)), const((1, D_CONV)), const((1, D_CONV)), const((1, D_CONV)),
        const((SUBLANES, D_STATE)), const((SUBLANES, D_STATE)),
        const((D_SSM, D_STATE)), const((D_SSM, D_STATE)),
        const((D_STATE, D_SSM)), const((D_STATE, D_SSM)),
        const((1, D_SSM)), const((D_SSM, D_SSM)), const((1, D_SSM)),
        const((D_MODEL, D_MODEL)), const((1, D_MODEL)), const((1, D_MODEL)),
    ]
    out_specs = [
        pl.BlockSpec((1, tm, D_MODEL), lambda q, i: (q, i, 0)),
        pl.BlockSpec((1, HIST_ROWS, D_CONV), lambda q, i: (q, 0, 0)),
        pl.BlockSpec((1, SUBLANES, D_STATE), lambda q, i: (q, 0, 0)),
    ]
    out_shape = [
        jax.ShapeDtypeStruct((nq, rows, D_MODEL), F32),
        jax.ShapeDtypeStruct((nq, HIST_ROWS, D_CONV), F32),
        jax.ShapeDtypeStruct((nq, SUBLANES, D_STATE), F32),
    ]
    scratch = [
        pltpu.VMEM((HIST_ROWS + tm + SUBLANES, D_CONV), F32),
        pltpu.VMEM((HIST_ROWS + tm + SUBLANES, D_CONV), F32),
        pltpu.VMEM((tm, D_CONV), F32),
        pltpu.VMEM((tm + SUBLANES, D_SSM), F32),
        pltpu.VMEM((tm + SUBLANES, D_SSM), F32),
        pltpu.VMEM((tm, D_STATE), F32),
        pltpu.VMEM((tm + SUBLANES, D_STATE), F32),
        pltpu.VMEM((tm, D_STATE), F32),
        pltpu.VMEM((tm + SUBLANES, D_STATE), F32),
        pltpu.VMEM((SUBLANES, D_STATE), F32),
    ]
    return pl.pallas_call(
        kern,
        grid=(nq, nt),
        in_specs=in_specs,
        out_specs=out_specs,
        out_shape=out_shape,
        scratch_shapes=scratch,
        compiler_params=pltpu.CompilerParams(
            dimension_semantics=("arbitrary", "arbitrary"),
            vmem_limit_bytes=VMEM_LIMIT_BYTES),
        name="mixer",
    )(x, hist0, h0, kpad, vpad, perm, permt,
      lw["w_in"], lw["wdw"], lw["bdw"], lw["clg"], lw["clb"], lw["a1"], lw["a2"],
      lw["bre"], lw["bim"], lw["cre"], lw["cim"], lw["d"], lw["wglu"], lw["bglu"],
      lw["w_out"], lw["g1"], lw["b1"])


def _route(logits):
    shape = logits.shape
    m = jnp.max(logits, axis=-1, keepdims=True)
    e = jnp.exp(logits - m)
    aff = e / jnp.sum(e, axis=-1, keepdims=True)
    lane = lax.broadcasted_iota(jnp.int32, shape, 1)
    lane_f = lane.astype(F32)
    grp = lax.shift_right_logical(lane, 2)
    neg = -1.0

    def top2(vals):
        v1 = jnp.max(vals, axis=-1, keepdims=True)
        i1 = jnp.min(jnp.where(vals == v1, lane_f, float(N_EXPERTS)), axis=-1, keepdims=True)
        rest = jnp.where(lane_f == i1, neg, vals)
        v2 = jnp.max(rest, axis=-1, keepdims=True)
        i2 = jnp.min(jnp.where(rest == v2, lane_f, float(N_EXPERTS)), axis=-1, keepdims=True)
        return v1, i1, v2, i2

    best = None
    sel = None
    for gi in range(N_EXPERT_GROUPS):
        v1, _, v2, _ = top2(jnp.where(grp == gi, aff, neg))
        score = v1 + v2
        if best is None:
            best, sel = score, jnp.zeros_like(score, dtype=jnp.int32)
        else:
            better = score > best
            sel = jnp.where(better, gi, sel)
            best = jnp.where(better, score, best)
    v1, i1, v2, i2 = top2(jnp.where(grp == sel, aff, neg))
    denom = v1 + v2
    return jnp.where(lane_f == i1, v1 / denom, jnp.where(lane_f == i2, v2 / denom, 0.0))


def _moe_dense_kernel(x_ref, wr_ref, br_ref, wup_ref, bup_ref, wdn_ref, g2_ref, b2_ref,
                      o_ref, xb, gate, acc, *, alpha):
    e = pl.program_id(1)

    @pl.when(e == 0)
    def _():
        xv = x_ref[...].astype(BF16)
        xb[...] = xv
        gate[...] = _route(_dot(xv, wr_ref[...]) + br_ref[...])
        acc[...] = jnp.zeros_like(acc)

    lane = lax.broadcasted_iota(jnp.int32, gate.shape, 1)
    gcol = jnp.sum(jnp.where(lane == e, gate[...], 0.0), axis=-1, keepdims=True)
    h = _gelu_tanh(_dot(xb[...], wup_ref[0]) + bup_ref[0]) * gcol
    acc[...] += _dot(h.astype(BF16), wdn_ref[0])

    @pl.when(e == N_EXPERTS - 1)
    def _():
        o_ref[...] = _layer_norm(alpha * x_ref[...] + acc[...], g2_ref[...], b2_ref[...])


def _moe_call(x, wr, br, lw, *, tmm, alpha):
    rows = x.shape[0]
    nt = rows // tmm
    kern = functools.partial(_moe_dense_kernel, alpha=alpha)
    return pl.pallas_call(
        kern,
        grid=(nt, N_EXPERTS),
        in_specs=[
            pl.BlockSpec((tmm, D_MODEL), lambda i, e: (i, 0)),
            pl.BlockSpec((D_MODEL, N_EXPERTS), lambda i, e: (0, 0)),
            pl.BlockSpec((1, N_EXPERTS), lambda i, e: (0, 0)),
            pl.BlockSpec((1, D_MODEL, D_EXPERT), lambda i, e: (e, 0, 0)),
            pl.BlockSpec((1, 1, D_EXPERT), lambda i, e: (e, 0, 0)),
            pl.BlockSpec((1, D_EXPERT, D_MODEL), lambda i, e: (e, 0, 0)),
            pl.BlockSpec((1, D_MODEL), lambda i, e: (0, 0)),
            pl.BlockSpec((1, D_MODEL), lambda i, e: (0, 0)),
        ],
        out_specs=pl.BlockSpec((tmm, D_MODEL), lambda i, e: (i, 0)),
        out_shape=jax.ShapeDtypeStruct((rows, D_MODEL), F32),
        scratch_shapes=[
            pltpu.VMEM((tmm, D_MODEL), BF16),
            pltpu.VMEM((tmm, N_EXPERTS), F32),
            pltpu.VMEM((tmm, D_MODEL), F32),
        ],
        compiler_params=pltpu.CompilerParams(
            dimension_semantics=("arbitrary", "arbitrary"),
            vmem_limit_bytes=VMEM_LIMIT_BYTES),
        name="moe",
    )(x, wr, br, lw["w_up"], lw["b_up"], lw["w_down"], lw["g2"], lw["b2"])


def _memkv_kernel(m_ref, wk_ref, wv_ref, k_ref, v_ref):
    mb = m_ref[...].astype(BF16)
    k_ref[0] = _dot(mb, wk_ref[0])
    v_ref[0] = _dot(mb, wv_ref[0])


def _memkv_call(mem, wk, wv):
    depth = wk.shape[0]
    rows = mem.shape[0]
    return pl.pallas_call(
        _memkv_kernel,
        grid=(depth,),
        in_specs=[
            pl.BlockSpec((rows, D_MODEL), lambda l: (0, 0)),
            pl.BlockSpec((1, D_MODEL, D_ATT), lambda l: (l, 0, 0)),
            pl.BlockSpec((1, D_MODEL, D_ATT), lambda l: (l, 0, 0)),
        ],
        out_specs=[
            pl.BlockSpec((1, rows, D_ATT), lambda l: (l, 0, 0)),
            pl.BlockSpec((1, rows, D_ATT), lambda l: (l, 0, 0)),
        ],
        out_shape=[jax.ShapeDtypeStruct((depth, rows, D_ATT), F32)] * 2,
        compiler_params=pltpu.CompilerParams(
            dimension_semantics=("arbitrary",), vmem_limit_bytes=VMEM_LIMIT_BYTES),
        name="memkv",
    )(mem, wk, wv)


def _perm_matrices(lc):
    tm = lc * SEQS_PER_STREAM
    p = np.zeros((tm, tm), np.float32)
    for s in range(SEQS_PER_STREAM):
        for t in range(lc):
            p[s * lc + t, t * SEQS_PER_STREAM + s] = 1.0
    return jnp.asarray(p, BF16), jnp.asarray(p.T, BF16)


def _to_streams(a):
    b, l, c = a.shape
    q = b // SEQS_PER_STREAM
    return a.reshape(q, SEQS_PER_STREAM, l, c).transpose(0, 2, 1, 3).reshape(q, l * SEQS_PER_STREAM, c)


def _from_streams(a, l):
    q, _, c = a.shape
    return a.reshape(q, l, SEQS_PER_STREAM, c).transpose(0, 2, 1, 3).reshape(q * SEQS_PER_STREAM, l, c)


def _pack_state(re, im):
    b = re.shape[0]
    q = b // SEQS_PER_STREAM
    re = re.reshape(q, SEQS_PER_STREAM, D_STATE)
    im = im.reshape(q, SEQS_PER_STREAM, D_STATE)
    return jnp.concatenate([im, re], axis=1)


def _unpack_state(h):
    q = h.shape[0]
    im = h[:, 0:SEQS_PER_STREAM].reshape(q * SEQS_PER_STREAM, N_SSM_GROUPS, SSM_STATE)
    re = h[:, SEQS_PER_STREAM:].reshape(q * SEQS_PER_STREAM, N_SSM_GROUPS, SSM_STATE)
    return re, im


def _pad_heads(mk, mv):
    b = mk.shape[0]
    q = b // SEQS_PER_STREAM
    eye = jnp.eye(N_MEM_HEADS, dtype=mk.dtype)
    kt = mk.transpose(0, 2, 3, 1)
    kp = jnp.einsum("bhdm,hg->bhgdm", kt, eye).reshape(b, N_MEM_HEADS, D_ATT, N_MEM)
    vt = mv.transpose(0, 2, 1, 3)
    vp = jnp.einsum("bhmd,hg->bhmgd", vt, eye).reshape(b, N_MEM_HEADS, N_MEM, D_ATT)
    kp = kp.reshape(q, SEQS_PER_STREAM, N_MEM_HEADS, D_ATT, N_MEM).astype(BF16)
    vp = vp.reshape(q, SEQS_PER_STREAM, N_MEM_HEADS, N_MEM, D_ATT).astype(BF16)
    return kp, vp


def _layer_params(l, w_in, w_dw, b_dw, conv_ln_g, conv_ln_b, ssm_a_re, ssm_a_im, ssm_b_re,
                  ssm_b_im, ssm_c_re, ssm_c_im, ssm_d, ssm_log_dt, ssm_w_glu, ssm_b_glu,
                  w_out, ln1_g, ln1_b, w_up, b_up, w_down, ln2_g, ln2_b):
    g, p = N_SSM_GROUPS, SSM_STATE
    a = lax.complex(ssm_a_re[l], ssm_a_im[l])
    dt = jnp.exp(ssm_log_dt[l])[:, None]
    a_bar = jnp.exp(a * dt)
    b_bar = ((a_bar - 1.0) / a)[..., None] * lax.complex(ssm_b_re[l], ssm_b_im[l])
    eye = jnp.eye(g, dtype=F32)

    def b_block(m):
        return jnp.einsum("gpi,gh->gihp", m, eye).reshape(D_SSM, D_STATE).astype(BF16)

    def c_block(m):
        return jnp.einsum("gip,gh->gphi", m, eye).reshape(D_STATE, D_SSM).astype(BF16)

    ar = jnp.real(a_bar).reshape(1, D_STATE)
    ai = jnp.imag(a_bar).reshape(1, D_STATE)
    half = SUBLANES // 2
    return {
        "w_in": w_in[l].astype(BF16),
        "wdw": jnp.repeat(w_dw[l], SUBLANES, axis=0),
        "bdw": b_dw[l][None], "clg": conv_ln_g[l][None], "clb": conv_ln_b[l][None],
        "a1": jnp.broadcast_to(ar, (SUBLANES, D_STATE)),
        "a2": jnp.concatenate([jnp.broadcast_to(-ai, (half, D_STATE)),
                               jnp.broadcast_to(ai, (half, D_STATE))], axis=0),
        "bre": b_block(jnp.real(b_bar)), "bim": b_block(jnp.imag(b_bar)),
        "cre": c_block(ssm_c_re[l]), "cim": c_block(-ssm_c_im[l]),
        "d": ssm_d[l][None], "wglu": ssm_w_glu[l].astype(BF16), "bglu": ssm_b_glu[l][None],
        "w_out": w_out[l].astype(BF16), "g1": ln1_g[l][None], "b1": ln1_b[l][None],
        "w_up": w_up[l].astype(BF16), "b_up": b_up[l][:, None, :], "w_down": w_down[l].astype(BF16),
        "g2": ln2_g[l][None], "b2": ln2_b[l][None],
    }


def _moe_tile(rows):
    return 1024 if rows % 1024 == 0 else rows


def kernel(x_prompt, x_sample, cache_conv, state_ssm_re, state_ssm_im, cache_mem_k, cache_mem_v, mem_prompt, w_in, w_dw, b_dw, conv_ln_g, conv_ln_b, ssm_a_re, ssm_a_im, ssm_b_re, ssm_b_im, ssm_c_re, ssm_c_im, ssm_d, ssm_log_dt, ssm_w_glu, ssm_b_glu, w_mem_k, w_mem_v, w_out, ln1_g, ln1_b, w_router, b_router, w_up, b_up, w_down, ln2_g, ln2_b):
    depth = w_in.shape[0]
    alpha = (2.0 * depth) ** 0.25
    bp, seq, _ = x_prompt.shape
    bs, dec_seq, _ = x_sample.shape
    assert bp == SEQS_PER_STREAM and bs % SEQS_PER_STREAM == 0
    lc_p = min(128, seq)
    lc_s = dec_seq
    assert seq % lc_p == 0 and lc_p % 16 == 0 and lc_s % 16 == 0

    mb, mm, _ = mem_prompt.shape
    mk_all, mv_all = _memkv_call(mem_prompt.reshape(mb * mm, D_MODEL),
                                 w_mem_k.astype(BF16), w_mem_v.astype(BF16))
    mk_all = mk_all.reshape(depth, mb, mm, N_MEM_HEADS, MEM_HEAD_DIM)
    mv_all = mv_all.reshape(depth, mb, mm, N_MEM_HEADS, MEM_HEAD_DIM)

    perm_p, permt_p = _perm_matrices(lc_p)
    perm_s, permt_s = _perm_matrices(lc_s)
    wr = w_router.astype(BF16)
    br = b_router[None]

    xp = _to_streams(x_prompt)
    xs = _to_streams(x_sample)
    nq_s = bs // SEQS_PER_STREAM
    zero_hist = jnp.zeros((1, HIST_ROWS, D_CONV), F32)
    zero_h = jnp.zeros((1, SUBLANES, D_STATE), F32)

    conv_p, re_p, im_p, conv_s, re_s, im_s = [], [], [], [], [], []
    for l in range(depth):
        lw = _layer_params(l, w_in, w_dw, b_dw, conv_ln_g, conv_ln_b, ssm_a_re, ssm_a_im,
                           ssm_b_re, ssm_b_im, ssm_c_re, ssm_c_im, ssm_d, ssm_log_dt, ssm_w_glu,
                           ssm_b_glu, w_out, ln1_g, ln1_b, w_up, b_up, w_down, ln2_g, ln2_b)
        kp, vp = _pad_heads(mk_all[l], mv_all[l])
        x1, hist, hst = _mixer_call(xp, zero_hist, zero_h, kp, vp, perm_p, permt_p, lw,
                                    lc=lc_p, alpha=alpha)
        rows = x1.shape[1]
        xp = _moe_call(x1.reshape(rows, D_MODEL), wr, br, lw, tmm=_moe_tile(rows),
                       alpha=alpha).reshape(1, rows, D_MODEL)
        conv_p.append(_from_streams(hist, CONV_BUF))
        hr, hi = _unpack_state(hst)
        re_p.append(hr)
        im_p.append(hi)
        kp, vp = _pad_heads(cache_mem_k[l], cache_mem_v[l])
        x1, hist, hst = _mixer_call(xs, _to_streams(cache_conv[l]),
                                    _pack_state(state_ssm_re[l], state_ssm_im[l]),
                                    kp, vp, perm_s, permt_s, lw, lc=lc_s, alpha=alpha)
        rows = nq_s * x1.shape[1]
        xs = _moe_call(x1.reshape(rows, D_MODEL), wr, br, lw, tmm=_moe_tile(rows),
                       alpha=alpha).reshape(nq_s, rows // nq_s, D_MODEL)
        conv_s.append(_from_streams(hist, CONV_BUF))
        hr, hi = _unpack_state(hst)
        re_s.append(hr)
        im_s.append(hi)

    return (_from_streams(xp, seq), _from_streams(xs, dec_seq),
            jnp.stack(conv_p), jnp.stack(re_p), jnp.stack(im_p), mk_all, mv_all,
            jnp.stack(conv_s), jnp.stack(re_s), jnp.stack(im_s))
```

```python
import functools
import math

import numpy as np
import jax
import jax.numpy as jnp
from jax import lax
from jax.experimental import pallas as pl
from jax.experimental.pallas import tpu as pltpu
from jax.experimental.pallas import tpu_sc as plsc

F32 = jnp.float32
BF16 = jnp.bfloat16

D_MODEL = 1024
CONV_WIDTH = 31
CONV_BUF = CONV_WIDTH - 1
D_CONV = 384
D_SSM = 384
SSM_GROUP = 16
N_SSM_GROUPS = D_SSM // SSM_GROUP
SSM_STATE = 64
D_STATE = N_SSM_GROUPS * SSM_STATE
N_MEM = 256
N_MEM_HEADS = 4
MEM_HEAD_DIM = 64
D_ATT = N_MEM_HEADS * MEM_HEAD_DIM
D_IN = 2 * D_CONV + D_SSM + D_ATT
N_EXPERTS = 16
N_EXPERT_GROUPS = 4
EXPERTS_PER_GROUP = N_EXPERTS // N_EXPERT_GROUPS
D_EXPERT = 512
LN_EPS = 1e-5

SEQS_PER_STREAM = 4
HIST_ROWS = CONV_BUF * SEQS_PER_STREAM
SUBLANES = 8
SCAN_LANES = 512
CONV_ROWS = 32
VMEM_LIMIT_BYTES = 56 * 1024 * 1024
HI_HALF_MASK = np.int32(-65536)
META_COLS = 8
SC_CORES = 2
SC_WORKERS = SC_CORES * 16
SC_MAX_CHUNK = 128


def _sigmoid(x):
    return 1.0 / (1.0 + jnp.exp(-x))


def _gelu_tanh(x):
    c = math.sqrt(2.0 / math.pi)
    return 0.5 * x * (1.0 + jnp.tanh(c * (x + 0.044715 * (x * x * x))))


def _layer_norm(z, g, b):
    mu = jnp.mean(z, axis=-1, keepdims=True)
    zc = z - mu
    var = jnp.mean(zc * zc, axis=-1, keepdims=True)
    return zc * lax.rsqrt(var + LN_EPS) * g + b


def _dot(a, b):
    return jnp.dot(a, b, preferred_element_type=F32)


def _mixer_kernel(x_ref, hist0_ref, h0_ref, k_ref, v_ref, perm_ref, permt_ref,
                  w_in_ref, wdw_ref, bdw_ref, clg_ref, clb_ref, a1_ref, a2_ref,
                  bre_ref, bim_ref, cre_ref, cim_ref, d_ref, wglu_ref, bglu_ref,
                  wout_ref, g1_ref, b1_ref,
                  x1_ref, hist_out_ref, h_out_ref,
                  xpad0, xpad4, cy, ush, yim, bu_re, bu_im, hre, him, hcar,
                  *, tm, lc, alpha):
    i = pl.program_id(1)

    @pl.when(i == 0)
    def _():
        xpad0[0:HIST_ROWS, :] = hist0_ref[0]
        hcar[...] = h0_ref[0]
        ush[...] = jnp.zeros_like(ush)

    x = x_ref[0]
    proj = _dot(x.astype(BF16), w_in_ref[...])

    g = proj[:, 0:D_CONV] * _sigmoid(proj[:, D_CONV:2 * D_CONV])
    xpad0[HIST_ROWS:HIST_ROWS + tm, :] = g
    xpad4[0:HIST_ROWS + tm - 4, :] = xpad0[4:HIST_ROWS + tm, :]

    nsub = CONV_ROWS // SUBLANES

    def conv_rows(rb, carry):
        r0 = pl.multiple_of(rb * CONV_ROWS, CONV_ROWS)
        accs = [jnp.broadcast_to(bdw_ref[...], (SUBLANES, D_CONV)) for _ in range(nsub)]
        for k in range(CONV_WIDTH):
            wk = wdw_ref[SUBLANES * k:SUBLANES * (k + 1), :]
            for sb in range(nsub):
                off = r0 + SEQS_PER_STREAM * k + SUBLANES * sb
                if k % 2 == 0:
                    xs = xpad0[pl.ds(pl.multiple_of(off, SUBLANES), SUBLANES), :]
                else:
                    xs = xpad4[pl.ds(pl.multiple_of(off - 4, SUBLANES), SUBLANES), :]
                accs[sb] = accs[sb] + xs * wk
        for sb in range(nsub):
            yn = _layer_norm(accs[sb], clg_ref[...], clb_ref[...])
            cy[pl.ds(pl.multiple_of(r0 + SUBLANES * sb, SUBLANES), SUBLANES), :] = yn * _sigmoid(yn)
        return carry

    lax.fori_loop(0, tm // CONV_ROWS, conv_rows, 0)

    new_hist = xpad0[tm:tm + HIST_ROWS, :]
    xpad0[0:HIST_ROWS, :] = new_hist
    hist_out_ref[0] = new_hist

    u = proj[:, 2 * D_CONV:2 * D_CONV + D_SSM]
    bu_re[...] = _dot(u.astype(BF16), bre_ref[...])
    ush[4:tm + 4, :] = u
    bu_im[...] = _dot(ush[...].astype(BF16), bim_ref[...])

    lo = lax.broadcasted_iota(jnp.int32, (SUBLANES, SCAN_LANES), 0) < 4
    for c in range(D_STATE // SCAN_LANES):
        cs = slice(c * SCAN_LANES, (c + 1) * SCAN_LANES)
        a1 = a1_ref[:, cs]
        a2 = a2_ref[:, cs]

        def scan_pair(j, carry, cs=cs, a1=a1, a2=a2):
            h_prev, im_cur = carry
            r = pl.multiple_of(j * SUBLANES, SUBLANES)
            re_cur = bu_re[pl.ds(r, SUBLANES), cs]
            im_next = bu_im[pl.ds(r + SUBLANES, SUBLANES), cs]
            p_even = jnp.where(lo, re_cur, im_cur)
            p_odd = jnp.where(lo, im_next, re_cur)
            h_even = a1 * pltpu.roll(h_prev, 4, 0) + a2 * h_prev + p_even
            h_odd = a1 * pltpu.roll(h_even, 4, 0) - a2 * h_even + p_odd
            hre[pl.ds(r, SUBLANES), cs] = jnp.where(lo, h_even, h_odd)
            him[pl.ds(r, SUBLANES), cs] = jnp.where(lo, h_prev, h_even)
            return h_odd, im_next

        h_last, _ = lax.fori_loop(0, lc // 2, scan_pair,
                                  (hcar[:, cs], bu_im[0:SUBLANES, cs]))
        him[tm:tm + SUBLANES, cs] = jnp.where(lo, h_last, 0.0)
        hcar[:, cs] = h_last
    h_out_ref[0] = hcar[...]

    y_re = _dot(hre[...].astype(BF16), cre_ref[...])
    y_im = _dot(him[...].astype(BF16), cim_ref[...])
    yim[...] = y_im
    y = y_re + yim[4:tm + 4, :] + d_ref[...] * u
    y = _gelu_tanh(y)
    ssm_y = y * _sigmoid(_dot(y.astype(BF16), wglu_ref[...]) + bglu_ref[...])

    q = proj[:, 2 * D_CONV + D_SSM:D_IN].astype(BF16)
    q_seq = _dot(perm_ref[...], q).astype(BF16)
    outs = []
    for s in range(SEQS_PER_STREAM):
        qs = q_seq[s * lc:(s + 1) * lc, :]
        acc = jnp.zeros((lc, D_ATT), F32)
        for h in range(N_MEM_HEADS):
            sc = _dot(qs, k_ref[0, s, h]) * (MEM_HEAD_DIM ** -0.5)
            sc = sc - jnp.max(sc, axis=-1, keepdims=True)
            e = jnp.exp(sc)
            p = e / jnp.sum(e, axis=-1, keepdims=True)
            acc = acc + _dot(p.astype(BF16), v_ref[0, s, h])
        outs.append(acc)
    att_seq = jnp.concatenate(outs, axis=0).astype(BF16)
    att = _dot(permt_ref[...], att_seq).astype(BF16)

    mix = (_dot(cy[...].astype(BF16), wout_ref[0:D_CONV, :])
           + _dot(ssm_y.astype(BF16), wout_ref[D_CONV:D_CONV + D_SSM, :])
           + _dot(att, wout_ref[D_CONV + D_SSM:D_MODEL, :]))
    x1_ref[0] = _layer_norm(alpha * x + mix, g1_ref[...], b1_ref[...])


def _mixer_call(x, hist0, h0, kpad, vpad, perm, permt, lw, *, lc, alpha):
    nq, rows, _ = x.shape
    tm = lc * SEQS_PER_STREAM
    nt = rows // tm
    kern = functools.partial(_mixer_kernel, tm=tm, lc=lc, alpha=alpha)

    def const(shape):
        return pl.BlockSpec(shape, lambda q, i: (0,) * len(shape))

    in_specs = [
        pl.BlockSpec((1, tm, D_MODEL), lambda q, i: (q, i, 0)),
        pl.BlockSpec((1, HIST_ROWS, D_CONV), lambda q, i: (q, 0, 0)),
        pl.BlockSpec((1, SUBLANES, D_STATE), lambda q, i: (q, 0, 0)),
        pl.BlockSpec((1, SEQS_PER_STREAM, N_MEM_HEADS, D_ATT, N_MEM), lambda q, i: (q, 0, 0, 0, 0)),
        pl.BlockSpec((1, SEQS_PER_STREAM, N_MEM_HEADS, N_MEM, D_ATT), lambda q, i: (q, 0, 0, 0, 0)),
        const((tm, tm)), const((tm, tm)),
        const((D_MODEL, D_IN)),
        const((CONV_WIDTH * SUBLANES, D_CONV)), const((1, D_CONV)), const((1, D_CONV)), const((1, D_CONV)),
        const((SUBLANES, D_STATE)), const((SUBLANES, D_STATE)),
        const((D_SSM, D_STATE)), const((D_SSM, D_STATE)),
        const((D_STATE, D_SSM)), const((D_STATE, D_SSM)),
        const((1, D_SSM)), const((D_SSM, D_SSM)), const((1, D_SSM)),
        const((D_MODEL, D_MODEL)), const((1, D_MODEL)), const((1, D_MODEL)),
    ]
    out_specs = [
        pl.BlockSpec((1, tm, D_MODEL), lambda q, i: (q, i, 0)),
        pl.BlockSpec((1, HIST_ROWS, D_CONV), lambda q, i: (q, 0, 0)),
        pl.BlockSpec((1, SUBLANES, D_STATE), lambda q, i: (q, 0, 0)),
    ]
    out_shape = [
        jax.ShapeDtypeStruct((nq, rows, D_MODEL), F32),
        jax.ShapeDtypeStruct((nq, HIST_ROWS, D_CONV), F32),
        jax.ShapeDtypeStruct((nq, SUBLANES, D_STATE), F32),
    ]
    scratch = [
        pltpu.VMEM((HIST_ROWS + tm + SUBLANES, D_CONV), F32),
        pltpu.VMEM((HIST_ROWS + tm + SUBLANES, D_CONV), F32),
        pltpu.VMEM((tm, D_CONV), F32),
        pltpu.VMEM((tm + SUBLANES, D_SSM), F32),
        pltpu.VMEM((tm + SUBLANES, D_SSM), F32),
        pltpu.VMEM((tm, D_STATE), F32),
        pltpu.VMEM((tm + SUBLANES, D_STATE), F32),
        pltpu.VMEM((tm, D_STATE), F32),
        pltpu.VMEM((tm + SUBLANES, D_STATE), F32),
        pltpu.VMEM((SUBLANES, D_STATE), F32),
    ]
    return pl.pallas_call(
        kern,
        grid=(nq, nt),
        in_specs=in_specs,
        out_specs=out_specs,
        out_shape=out_shape,
        scratch_shapes=scratch,
        compiler_params=pltpu.CompilerParams(
            dimension_semantics=("arbitrary", "arbitrary"),
            vmem_limit_bytes=VMEM_LIMIT_BYTES),
        name="mixer",
    )(x, hist0, h0, kpad, vpad, perm, permt,
      lw["w_in"], lw["wdw"], lw["bdw"], lw["clg"], lw["clb"], lw["a1"], lw["a2"],
      lw["bre"], lw["bim"], lw["cre"], lw["cim"], lw["d"], lw["wglu"], lw["bglu"],
      lw["w_out"], lw["g1"], lw["b1"])


def _route(logits):
    shape = logits.shape
    m = jnp.max(logits, axis=-1, keepdims=True)
    e = jnp.exp(logits - m)
    aff = e / jnp.sum(e, axis=-1, keepdims=True)
    lane = lax.broadcasted_iota(jnp.int32, shape, 1)
    lane_f = lane.astype(F32)
    grp = lax.shift_right_logical(lane, 2)
    neg = -1.0

    def top2(vals):
        v1 = jnp.max(vals, axis=-1, keepdims=True)
        i1 = jnp.min(jnp.where(vals == v1, lane_f, float(N_EXPERTS)), axis=-1, keepdims=True)
        rest = jnp.where(lane_f == i1, neg, vals)
        v2 = jnp.max(rest, axis=-1, keepdims=True)
        i2 = jnp.min(jnp.where(rest == v2, lane_f, float(N_EXPERTS)), axis=-1, keepdims=True)
        return v1, i1, v2, i2

    best = None
    sel = None
    for gi in range(N_EXPERT_GROUPS):
        v1, _, v2, _ = top2(jnp.where(grp == gi, aff, neg))
        score = v1 + v2
        if best is None:
            best, sel = score, jnp.zeros_like(score, dtype=jnp.int32)
        else:
            better = score > best
            sel = jnp.where(better, gi, sel)
            best = jnp.where(better, score, best)
    v1, i1, v2, i2 = top2(jnp.where(grp == sel, aff, neg))
    denom = v1 + v2
    return i1, i2, v1 / denom, v2 / denom


def _pack_halves(y):
    half = y.shape[1] // 2
    lo = lax.bitcast_convert_type(y[:, :half].astype(BF16).astype(F32), jnp.int32)
    hi = lax.bitcast_convert_type(y[:, half:].astype(BF16).astype(F32), jnp.int32)
    return lax.shift_right_logical(lo, 16) | (hi & HI_HALF_MASK)


def _unpack_halves(w):
    lo = lax.bitcast_convert_type(lax.shift_left(w, 16), F32)
    hi = lax.bitcast_convert_type(w & HI_HALF_MASK, F32)
    return lo, hi


def _route_kernel(x_ref, wr_ref, br_ref, tri_ref, xp_ref, meta_ref, cnt_ref, running):
    i = pl.program_id(0)

    @pl.when(i == 0)
    def _():
        running[...] = jnp.zeros_like(running)

    x = x_ref[...]
    xp_ref[...] = _pack_halves(x)
    i1, i2, g1, g2 = _route(_dot(x.astype(BF16), wr_ref[...]) + br_ref[...])
    tm = x.shape[0]
    lane_f = lax.broadcasted_iota(jnp.int32, (tm, N_EXPERTS), 1).astype(F32)
    hot1 = jnp.where(lane_f == i1, 1.0, 0.0)
    hot2 = jnp.where(lane_f == i2, 1.0, 0.0)
    both = hot1 + hot2
    before = _dot(tri_ref[...], both.astype(BF16)) + running[...]
    r1 = jnp.sum(hot1 * before, axis=-1, keepdims=True)
    r2 = jnp.sum(hot2 * before, axis=-1, keepdims=True)
    running[...] = running[...] + jnp.sum(both, axis=0, keepdims=True)
    cnt_ref[...] = running[...]
    col = lax.broadcasted_iota(jnp.int32, (tm, META_COLS), 1)
    meta = jnp.where(col == 0, i1, jnp.where(col == 1, i2, jnp.where(col == 2, g1, jnp.where(
        col == 3, g2, jnp.where(col == 4, r1, jnp.where(col == 5, r2, 0.0))))))
    meta_ref[...] = meta


def _route_call(x1, wr, br, *, tmr):
    rows = x1.shape[0]
    tri = jnp.asarray(np.tril(np.ones((tmr, tmr), np.float32), -1), BF16)
    return pl.pallas_call(
        _route_kernel,
        grid=(rows // tmr,),
        in_specs=[
            pl.BlockSpec((tmr, D_MODEL), lambda i: (i, 0)),
            pl.BlockSpec((D_MODEL, N_EXPERTS), lambda i: (0, 0)),
            pl.BlockSpec((1, N_EXPERTS), lambda i: (0, 0)),
            pl.BlockSpec((tmr, tmr), lambda i: (0, 0)),
        ],
        out_specs=[
            pl.BlockSpec((tmr, D_MODEL // 2), lambda i: (i, 0)),
            pl.BlockSpec((tmr, META_COLS), lambda i: (i, 0)),
            pl.BlockSpec((1, N_EXPERTS), lambda i: (0, 0)),
        ],
        out_shape=[
            jax.ShapeDtypeStruct((rows, D_MODEL // 2), jnp.int32),
            jax.ShapeDtypeStruct((rows, META_COLS), F32),
            jax.ShapeDtypeStruct((1, N_EXPERTS), F32),
        ],
        scratch_shapes=[pltpu.VMEM((1, N_EXPERTS), F32)],
        compiler_params=pltpu.CompilerParams(
            dimension_semantics=("arbitrary",), vmem_limit_bytes=VMEM_LIMIT_BYTES),
        name="route",
    )(x1, wr, br, tri)


def _expert_kernel(te_ref, nt_ref, xs_ref, wup_ref, bup_ref, wdn_ref, ys_ref):
    @pl.when(pl.program_id(0) < nt_ref[0])
    def _():
        lo, hi = _unpack_halves(xs_ref[...])
        half = D_MODEL // 2
        h = (_dot(lo.astype(BF16), wup_ref[0, 0:half, :])
             + _dot(hi.astype(BF16), wup_ref[0, half:D_MODEL, :]) + bup_ref[0])
        ys_ref[...] = _pack_halves(_dot(_gelu_tanh(h).astype(BF16), wdn_ref[0]))


def _expert_call(xs, tile_expert, n_tiles, lw, *, rt):
    rows = xs.shape[0]

    def row_map(i, te, nt):
        return (jnp.minimum(i, nt[0] - 1), 0)

    def w_map(i, te, nt):
        return (te[i], 0, 0)

    return pl.pallas_call(
        _expert_kernel,
        grid_spec=pltpu.PrefetchScalarGridSpec(
            num_scalar_prefetch=2,
            grid=(rows // rt,),
            in_specs=[
                pl.BlockSpec((rt, D_MODEL // 2), row_map),
                pl.BlockSpec((1, D_MODEL, D_EXPERT), w_map),
                pl.BlockSpec((1, 1, D_EXPERT), w_map),
                pl.BlockSpec((1, D_EXPERT, D_MODEL), w_map),
            ],
            out_specs=pl.BlockSpec((rt, D_MODEL // 2), row_map),
        ),
        out_shape=jax.ShapeDtypeStruct((rows, D_MODEL // 2), jnp.int32),
        compiler_params=pltpu.CompilerParams(
            dimension_semantics=("arbitrary",), vmem_limit_bytes=VMEM_LIMIT_BYTES),
        name="experts",
    )(tile_expert, n_tiles, xs, lw["w_up"], lw["b_up"], lw["w_down"])


def _combine_kernel(x_ref, yg_ref, meta_ref, g2_ref, b2_ref, o_ref, *, alpha):
    meta = meta_ref[...]
    g1 = meta[:, 2:3]
    g2 = meta[:, 3:4]
    lo1, hi1 = _unpack_halves(yg_ref[0])
    lo2, hi2 = _unpack_halves(yg_ref[1])
    moe = jnp.concatenate([g1 * lo1 + g2 * lo2, g1 * hi1 + g2 * hi2], axis=1)
    o_ref[...] = _layer_norm(alpha * x_ref[...] + moe, g2_ref[...], b2_ref[...])


def _combine_call(x1, yg, meta, lw, *, tmc, alpha):
    rows = x1.shape[0]
    return pl.pallas_call(
        functools.partial(_combine_kernel, alpha=alpha),
        grid=(rows // tmc,),
        in_specs=[
            pl.BlockSpec((tmc, D_MODEL), lambda i: (i, 0)),
            pl.BlockSpec((2, tmc, D_MODEL // 2), lambda i: (0, i, 0)),
            pl.BlockSpec((tmc, META_COLS), lambda i: (i, 0)),
            pl.BlockSpec((1, D_MODEL), lambda i: (0, 0)),
            pl.BlockSpec((1, D_MODEL), lambda i: (0, 0)),
        ],
        out_specs=pl.BlockSpec((tmc, D_MODEL), lambda i: (i, 0)),
        out_shape=jax.ShapeDtypeStruct((rows, D_MODEL), F32),
        compiler_params=pltpu.CompilerParams(
            dimension_semantics=("arbitrary",), vmem_limit_bytes=VMEM_LIMIT_BYTES),
        name="combine",
    )(x1, yg, meta, lw["g2"], lw["b2"])


def _sc_mesh():
    return plsc.VectorSubcoreMesh(core_axis_name="c", subcore_axis_name="s")


def _sc_chunk(rows):
    per_worker = rows // SC_WORKERS
    chunk = min(SC_MAX_CHUNK, per_worker)
    assert per_worker % chunk == 0 and chunk % 8 == 0
    return per_worker // chunk, chunk


def _sc_dispatch(xp, pos, n_slots):
    rows, width = xp.shape
    nch, ch = _sc_chunk(rows)
    posr = pos.reshape(2, SC_WORKERS, nch, ch)

    @functools.partial(
        pl.kernel, mesh=_sc_mesh(),
        out_type=jax.ShapeDtypeStruct((n_slots, width), jnp.int32),
        scratch_types=[pltpu.VMEM((ch,), jnp.int32), pltpu.VMEM((ch,), jnp.int32),
                       pltpu.VMEM((ch, width), jnp.int32), pltpu.SemaphoreType.DMA],
        name="sc_dispatch")
    def k(x_hbm, pos_hbm, o_hbm, idx0, idx1, buf, sem):
        wid = lax.axis_index("s") * SC_CORES + lax.axis_index("c")

        @pl.loop(0, nch)
        def _(c):
            base = (wid * nch + c) * ch
            pltpu.sync_copy(x_hbm.at[pl.ds(base, ch)], buf)
            pltpu.sync_copy(pos_hbm.at[0, wid, c], idx0)
            pltpu.sync_copy(pos_hbm.at[1, wid, c], idx1)
            pltpu.async_copy(buf, o_hbm.at[idx0], sem).wait()
            pltpu.async_copy(buf, o_hbm.at[idx1], sem).wait()

    return k(xp, posr)


def _sc_gather(ys, pos):
    width = ys.shape[1]
    rows = pos.shape[1]
    nch, ch = _sc_chunk(rows)
    posr = pos.reshape(2, SC_WORKERS, nch, ch)

    @functools.partial(
        pl.kernel, mesh=_sc_mesh(),
        out_type=jax.ShapeDtypeStruct((2, rows, width), jnp.int32),
        scratch_types=[pltpu.VMEM((ch,), jnp.int32), pltpu.VMEM((ch, width), jnp.int32),
                       pltpu.SemaphoreType.DMA],
        name="sc_gather")
    def k(y_hbm, pos_hbm, o_hbm, idx, buf, sem):
        wid = lax.axis_index("s") * SC_CORES + lax.axis_index("c")

        @pl.loop(0, nch)
        def _(c):
            base = (wid * nch + c) * ch
            for kk in range(2):
                pltpu.sync_copy(pos_hbm.at[kk, wid, c], idx)
                pltpu.async_copy(y_hbm.at[idx], buf, sem).wait()
                pltpu.sync_copy(buf, o_hbm.at[kk, pl.ds(base, ch)])

    return k(ys, posr)


def _moe_call(x1, wr, br, lw, *, alpha):
    rows = x1.shape[0]
    tmr = min(1024, rows)
    rt = 512 if rows >= 8192 else 128
    n_slots = 2 * rows + N_EXPERTS * rt
    nt_max = n_slots // rt
    xp, meta, counts = _route_call(x1, wr, br, tmr=tmr)

    cnt = counts[0].astype(jnp.int32)
    padded = ((cnt + rt - 1) // rt) * rt
    ends = jnp.cumsum(padded)
    offs = ends - padded
    eidx = meta[:, 0:2].astype(jnp.int32)
    rank = meta[:, 4:6].astype(jnp.int32)
    pos = (offs[eidx] + rank).T
    n_tiles = ends[-1:] // rt
    tiles = jnp.minimum(jnp.arange(nt_max, dtype=jnp.int32), n_tiles[0] - 1)
    tile_expert = jnp.minimum(jnp.searchsorted(ends // rt, tiles, side="right"),
                              N_EXPERTS - 1).astype(jnp.int32)

    xs = _sc_dispatch(xp, pos, n_slots)
    ys = _expert_call(xs, tile_expert, n_tiles.astype(jnp.int32), lw, rt=rt)
    yg = _sc_gather(ys, pos)
    return _combine_call(x1, yg, meta, lw, tmc=tmr, alpha=alpha)


def _memkv_kernel(m_ref, wk_ref, wv_ref, k_ref, v_ref):
    mb = m_ref[...].astype(BF16)
    k_ref[0] = _dot(mb, wk_ref[0])
    v_ref[0] = _dot(mb, wv_ref[0])


def _memkv_call(mem, wk, wv):
    depth = wk.shape[0]
    rows = mem.shape[0]
    return pl.pallas_call(
        _memkv_kernel,
        grid=(depth,),
        in_specs=[
            pl.BlockSpec((rows, D_MODEL), lambda l: (0, 0)),
            pl.BlockSpec((1, D_MODEL, D_ATT), lambda l: (l, 0, 0)),
            pl.BlockSpec((1, D_MODEL, D_ATT), lambda l: (l, 0, 0)),
        ],
        out_specs=[
            pl.BlockSpec((1, rows, D_ATT), lambda l: (l, 0, 0)),
            pl.BlockSpec((1, rows, D_ATT), lambda l: (l, 0, 0)),
        ],
        out_shape=[jax.ShapeDtypeStruct((depth, rows, D_ATT), F32)] * 2,
        compiler_params=pltpu.CompilerParams(
            dimension_semantics=("arbitrary",), vmem_limit_bytes=VMEM_LIMIT_BYTES),
        name="memkv",
    )(mem, wk, wv)


def _perm_matrices(lc):
    tm = lc * SEQS_PER_STREAM
    p = np.zeros((tm, tm), np.float32)
    for s in range(SEQS_PER_STREAM):
        for t in range(lc):
            p[s * lc + t, t * SEQS_PER_STREAM + s] = 1.0
    return jnp.asarray(p, BF16), jnp.asarray(p.T, BF16)


def _to_streams(a):
    b, l, c = a.shape
    q = b // SEQS_PER_STREAM
    return a.reshape(q, SEQS_PER_STREAM, l, c).transpose(0, 2, 1, 3).reshape(q, l * SEQS_PER_STREAM, c)


def _from_streams(a, l):
    q, _, c = a.shape
    return a.reshape(q, l, SEQS_PER_STREAM, c).transpose(0, 2, 1, 3).reshape(q * SEQS_PER_STREAM, l, c)


def _pack_state(re, im):
    b = re.shape[0]
    q = b // SEQS_PER_STREAM
    re = re.reshape(q, SEQS_PER_STREAM, D_STATE)
    im = im.reshape(q, SEQS_PER_STREAM, D_STATE)
    return jnp.concatenate([im, re], axis=1)


def _unpack_state(h):
    q = h.shape[0]
    im = h[:, 0:SEQS_PER_STREAM].reshape(q * SEQS_PER_STREAM, N_SSM_GROUPS, SSM_STATE)
    re = h[:, SEQS_PER_STREAM:].reshape(q * SEQS_PER_STREAM, N_SSM_GROUPS, SSM_STATE)
    return re, im


def _pad_heads(mk, mv):
    b = mk.shape[0]
    q = b // SEQS_PER_STREAM
    eye = jnp.eye(N_MEM_HEADS, dtype=mk.dtype)
    kt = mk.transpose(0, 2, 3, 1)
    kp = jnp.einsum("bhdm,hg->bhgdm", kt, eye).reshape(b, N_MEM_HEADS, D_ATT, N_MEM)
    vt = mv.transpose(0, 2, 1, 3)
    vp = jnp.einsum("bhmd,hg->bhmgd", vt, eye).reshape(b, N_MEM_HEADS, N_MEM, D_ATT)
    kp = kp.reshape(q, SEQS_PER_STREAM, N_MEM_HEADS, D_ATT, N_MEM).astype(BF16)
    vp = vp.reshape(q, SEQS_PER_STREAM, N_MEM_HEADS, N_MEM, D_ATT).astype(BF16)
    return kp, vp


def _layer_params(l, w_in, w_dw, b_dw, conv_ln_g, conv_ln_b, ssm_a_re, ssm_a_im, ssm_b_re,
                  ssm_b_im, ssm_c_re, ssm_c_im, ssm_d, ssm_log_dt, ssm_w_glu, ssm_b_glu,
                  w_out, ln1_g, ln1_b, w_up, b_up, w_down, ln2_g, ln2_b):
    g, p = N_SSM_GROUPS, SSM_STATE
    a = lax.complex(ssm_a_re[l], ssm_a_im[l])
    dt = jnp.exp(ssm_log_dt[l])[:, None]
    a_bar = jnp.exp(a * dt)
    b_bar = ((a_bar - 1.0) / a)[..., None] * lax.complex(ssm_b_re[l], ssm_b_im[l])
    eye = jnp.eye(g, dtype=F32)

    def b_block(m):
        return jnp.einsum("gpi,gh->gihp", m, eye).reshape(D_SSM, D_STATE).astype(BF16)

    def c_block(m):
        return jnp.einsum("gip,gh->gphi", m, eye).reshape(D_STATE, D_SSM).astype(BF16)

    ar = jnp.real(a_bar).reshape(1, D_STATE)
    ai = jnp.imag(a_bar).reshape(1, D_STATE)
    half = SUBLANES // 2
    return {
        "w_in": w_in[l].astype(BF16),
        "wdw": jnp.repeat(w_dw[l], SUBLANES, axis=0),
        "bdw": b_dw[l][None], "clg": conv_ln_g[l][None], "clb": conv_ln_b[l][None],
        "a1": jnp.broadcast_to(ar, (SUBLANES, D_STATE)),
        "a2": jnp.concatenate([jnp.broadcast_to(-ai, (half, D_STATE)),
                               jnp.broadcast_to(ai, (half, D_STATE))], axis=0),
        "bre": b_block(jnp.real(b_bar)), "bim": b_block(jnp.imag(b_bar)),
        "cre": c_block(ssm_c_re[l]), "cim": c_block(-ssm_c_im[l]),
        "d": ssm_d[l][None], "wglu": ssm_w_glu[l].astype(BF16), "bglu": ssm_b_glu[l][None],
        "w_out": w_out[l].astype(BF16), "g1": ln1_g[l][None], "b1": ln1_b[l][None],
        "w_up": w_up[l].astype(BF16), "b_up": b_up[l][:, None, :], "w_down": w_down[l].astype(BF16),
        "g2": ln2_g[l][None], "b2": ln2_b[l][None],
    }


def kernel(x_prompt, x_sample, cache_conv, state_ssm_re, state_ssm_im, cache_mem_k, cache_mem_v, mem_prompt, w_in, w_dw, b_dw, conv_ln_g, conv_ln_b, ssm_a_re, ssm_a_im, ssm_b_re, ssm_b_im, ssm_c_re, ssm_c_im, ssm_d, ssm_log_dt, ssm_w_glu, ssm_b_glu, w_mem_k, w_mem_v, w_out, ln1_g, ln1_b, w_router, b_router, w_up, b_up, w_down, ln2_g, ln2_b):
    depth = w_in.shape[0]
    alpha = (2.0 * depth) ** 0.25
    bp, seq, _ = x_prompt.shape
    bs, dec_seq, _ = x_sample.shape
    assert bp == SEQS_PER_STREAM and bs % SEQS_PER_STREAM == 0
    lc_p = min(128, seq)
    lc_s = dec_seq
    assert seq % lc_p == 0 and lc_p % 16 == 0 and lc_s % 16 == 0

    mb, mm, _ = mem_prompt.shape
    mk_all, mv_all = _memkv_call(mem_prompt.reshape(mb * mm, D_MODEL),
                                 w_mem_k.astype(BF16), w_mem_v.astype(BF16))
    mk_all = mk_all.reshape(depth, mb, mm, N_MEM_HEADS, MEM_HEAD_DIM)
    mv_all = mv_all.reshape(depth, mb, mm, N_MEM_HEADS, MEM_HEAD_DIM)

    perm_p, permt_p = _perm_matrices(lc_p)
    perm_s, permt_s = _perm_matrices(lc_s)
    wr = w_router.astype(BF16)
    br = b_router[None]

    xp = _to_streams(x_prompt)
    xs = _to_streams(x_sample)
    nq_s = bs // SEQS_PER_STREAM
    zero_hist = jnp.zeros((1, HIST_ROWS, D_CONV), F32)
    zero_h = jnp.zeros((1, SUBLANES, D_STATE), F32)

    conv_p, re_p, im_p, conv_s, re_s, im_s = [], [], [], [], [], []
    for l in range(depth):
        lw = _layer_params(l, w_in, w_dw, b_dw, conv_ln_g, conv_ln_b, ssm_a_re, ssm_a_im,
                           ssm_b_re, ssm_b_im, ssm_c_re, ssm_c_im, ssm_d, ssm_log_dt, ssm_w_glu,
                           ssm_b_glu, w_out, ln1_g, ln1_b, w_up, b_up, w_down, ln2_g, ln2_b)
        kp, vp = _pad_heads(mk_all[l], mv_all[l])
        x1, hist, hst = _mixer_call(xp, zero_hist, zero_h, kp, vp, perm_p, permt_p, lw,
                                    lc=lc_p, alpha=alpha)
        rows = x1.shape[1]
        xp = _moe_call(x1.reshape(rows, D_MODEL), wr, br, lw, alpha=alpha).reshape(1, rows, D_MODEL)
        conv_p.append(_from_streams(hist, CONV_BUF))
        hr, hi = _unpack_state(hst)
        re_p.append(hr)
        im_p.append(hi)
        kp, vp = _pad_heads(cache_mem_k[l], cache_mem_v[l])
        x1, hist, hst = _mixer_call(xs, _to_streams(cache_conv[l]),
                                    _pack_state(state_ssm_re[l], state_ssm_im[l]),
                                    kp, vp, perm_s, permt_s, lw, lc=lc_s, alpha=alpha)
        rows = nq_s * x1.shape[1]
        xs = _moe_call(x1.reshape(rows, D_MODEL), wr, br, lw, alpha=alpha).reshape(nq_s, rows // nq_s, D_MODEL)
        conv_s.append(_from_streams(hist, CONV_BUF))
        hr, hi = _unpack_state(hst)
        re_s.append(hr)
        im_s.append(hi)

    return (_from_streams(xp, seq), _from_streams(xs, dec_seq),
            jnp.stack(conv_p), jnp.stack(re_p), jnp.stack(im_p), mk_all, mv_all,
            jnp.stack(conv_s), jnp.stack(re_s), jnp.stack(im_s))
```

```python
import functools
import math

import numpy as np
import jax
import jax.numpy as jnp
from jax import lax
from jax.experimental import pallas as pl
from jax.experimental.pallas import tpu as pltpu
from jax.experimental.pallas import tpu_sc as plsc

F32 = jnp.float32
BF16 = jnp.bfloat16

D_MODEL = 1024
CONV_WIDTH = 31
CONV_BUF = CONV_WIDTH - 1
D_CONV = 384
D_SSM = 384
SSM_GROUP = 16
N_SSM_GROUPS = D_SSM // SSM_GROUP
SSM_STATE = 64
D_STATE = N_SSM_GROUPS * SSM_STATE
N_MEM = 256
N_MEM_HEADS = 4
MEM_HEAD_DIM = 64
D_ATT = N_MEM_HEADS * MEM_HEAD_DIM
D_IN = 2 * D_CONV + D_SSM + D_ATT
N_EXPERTS = 16
N_EXPERT_GROUPS = 4
EXPERTS_PER_GROUP = N_EXPERTS // N_EXPERT_GROUPS
D_EXPERT = 512
LN_EPS = 1e-5

SEQS_PER_STREAM = 4
HIST_ROWS = CONV_BUF * SEQS_PER_STREAM
SUBLANES = 8
SCAN_LANES = 512
CONV_ROWS = 32
VMEM_LIMIT_BYTES = 56 * 1024 * 1024
HI_HALF_MASK = np.int32(-65536)
META_COLS = 8
SC_CORES = 2
SC_WORKERS = SC_CORES * 16
SC_MAX_CHUNK = 128


def _sigmoid(x):
    return 1.0 / (1.0 + jnp.exp(-x))


def _gelu_tanh(x):
    c = math.sqrt(2.0 / math.pi)
    return 0.5 * x * (1.0 + jnp.tanh(c * (x + 0.044715 * (x * x * x))))


def _layer_norm(z, g, b):
    mu = jnp.mean(z, axis=-1, keepdims=True)
    zc = z - mu
    var = jnp.mean(zc * zc, axis=-1, keepdims=True)
    return zc * lax.rsqrt(var + LN_EPS) * g + b


def _dot(a, b):
    return jnp.dot(a, b, preferred_element_type=F32)


def _mixer_kernel(*refs, tm, lc, alpha, fuse_in):
    refs = list(refs)
    if fuse_in:
        xprev_ref, yg_ref, metain_ref, g2p_ref, b2p_ref = refs[:5]
        refs = refs[5:]
    else:
        xprev_ref = refs.pop(0)
    (hist0_ref, h0_ref, k_ref, v_ref, perm_ref, permt_ref,
     w_in_ref, wdw_ref, bdw_ref, clg_ref, clb_ref, a1_ref, a2_ref,
     bre_ref, bim_ref, cre_ref, cim_ref, d_ref, wglu_ref, bglu_ref,
     wout_ref, g1_ref, b1_ref, wr_ref, br_ref, tri_ref,
     x1_ref, hist_out_ref, h_out_ref, xp_ref, meta_ref, cnt_ref,
     xpad0, xpad4, cy, ush, yim, bu_re, bu_im, hre, him, hcar, xin, running) = refs
    i = pl.program_id(1)

    @pl.when(i == 0)
    def _():
        xpad0[0:HIST_ROWS, :] = hist0_ref[0]
        hcar[...] = h0_ref[0]
        ush[...] = jnp.zeros_like(ush)

    @pl.when((i == 0) & (pl.program_id(0) == 0))
    def _():
        running[...] = jnp.zeros_like(running)

    if fuse_in:
        xin[...] = _moe_combine(xprev_ref[0], yg_ref, metain_ref[...], g2p_ref[...], b2p_ref[...], alpha)
    else:
        xin[...] = xprev_ref[0]
    x = xin[...]
    proj = _dot(x.astype(BF16), w_in_ref[...])

    g = proj[:, 0:D_CONV] * _sigmoid(proj[:, D_CONV:2 * D_CONV])
    xpad0[HIST_ROWS:HIST_ROWS + tm, :] = g
    xpad4[0:HIST_ROWS + tm - 4, :] = xpad0[4:HIST_ROWS + tm, :]

    nsub = CONV_ROWS // SUBLANES

    def conv_rows(rb, carry):
        r0 = pl.multiple_of(rb * CONV_ROWS, CONV_ROWS)
        accs = [jnp.broadcast_to(bdw_ref[...], (SUBLANES, D_CONV)) for _ in range(nsub)]
        for k in range(CONV_WIDTH):
            wk = wdw_ref[SUBLANES * k:SUBLANES * (k + 1), :]
            for sb in range(nsub):
                off = r0 + SEQS_PER_STREAM * k + SUBLANES * sb
                if k % 2 == 0:
                    xs = xpad0[pl.ds(pl.multiple_of(off, SUBLANES), SUBLANES), :]
                else:
                    xs = xpad4[pl.ds(pl.multiple_of(off - 4, SUBLANES), SUBLANES), :]
                accs[sb] = accs[sb] + xs * wk
        for sb in range(nsub):
            cy[pl.ds(pl.multiple_of(r0 + SUBLANES * sb, SUBLANES), SUBLANES), :] = accs[sb]
        return carry

    lax.fori_loop(0, tm // CONV_ROWS, conv_rows, 0)
    conv_n = _layer_norm(cy[...], clg_ref[...], clb_ref[...])
    cy[...] = conv_n * _sigmoid(conv_n)

    new_hist = xpad0[tm:tm + HIST_ROWS, :]
    xpad0[0:HIST_ROWS, :] = new_hist
    hist_out_ref[0] = new_hist

    u = proj[:, 2 * D_CONV:2 * D_CONV + D_SSM]
    bu_re[...] = _dot(u.astype(BF16), bre_ref[...])
    ush[4:tm + 4, :] = u
    bu_im[...] = _dot(ush[...].astype(BF16), bim_ref[...])

    lo = lax.broadcasted_iota(jnp.int32, (SUBLANES, SCAN_LANES), 0) < 4
    for c in range(D_STATE // SCAN_LANES):
        cs = slice(c * SCAN_LANES, (c + 1) * SCAN_LANES)
        a1 = a1_ref[:, cs]
        a2 = a2_ref[:, cs]

        def scan_pair(j, carry, cs=cs, a1=a1, a2=a2):
            h_prev, im_cur = carry
            r = pl.multiple_of(j * SUBLANES, SUBLANES)
            re_cur = bu_re[pl.ds(r, SUBLANES), cs]
            im_next = bu_im[pl.ds(r + SUBLANES, SUBLANES), cs]
            p_even = jnp.where(lo, re_cur, im_cur)
            p_odd = jnp.where(lo, im_next, re_cur)
            h_even = a1 * pltpu.roll(h_prev, 4, 0) + a2 * h_prev + p_even
            h_odd = a1 * pltpu.roll(h_even, 4, 0) - a2 * h_even + p_odd
            hre[pl.ds(r, SUBLANES), cs] = jnp.where(lo, h_even, h_odd)
            him[pl.ds(r, SUBLANES), cs] = jnp.where(lo, h_prev, h_even)
            return h_odd, im_next

        h_last, _ = lax.fori_loop(0, lc // 2, scan_pair,
                                  (hcar[:, cs], bu_im[0:SUBLANES, cs]))
        him[tm:tm + SUBLANES, cs] = jnp.where(lo, h_last, 0.0)
        hcar[:, cs] = h_last
    h_out_ref[0] = hcar[...]

    y_re = _dot(hre[...].astype(BF16), cre_ref[...])
    y_im = _dot(him[...].astype(BF16), cim_ref[...])
    yim[...] = y_im
    y = y_re + yim[4:tm + 4, :] + d_ref[...] * u
    y = _gelu_tanh(y)
    ssm_y = y * _sigmoid(_dot(y.astype(BF16), wglu_ref[...]) + bglu_ref[...])

    q = proj[:, 2 * D_CONV + D_SSM:D_IN].astype(BF16)
    q_seq = _dot(perm_ref[...], q).astype(BF16)
    outs = []
    for s in range(SEQS_PER_STREAM):
        qs = q_seq[s * lc:(s + 1) * lc, :]
        acc = jnp.zeros((lc, D_ATT), F32)
        for h in range(N_MEM_HEADS):
            sc = _dot(qs, k_ref[0, s, h]) * (MEM_HEAD_DIM ** -0.5)
            sc = sc - jnp.max(sc, axis=-1, keepdims=True)
            e = jnp.exp(sc)
            p = e * (1.0 / jnp.sum(e, axis=-1, keepdims=True))
            acc = acc + _dot(p.astype(BF16), v_ref[0, s, h])
        outs.append(acc)
    att_seq = jnp.concatenate(outs, axis=0).astype(BF16)
    att = _dot(permt_ref[...], att_seq).astype(BF16)

    mix = (_dot(cy[...].astype(BF16), wout_ref[0:D_CONV, :])
           + _dot(ssm_y.astype(BF16), wout_ref[D_CONV:D_CONV + D_SSM, :])
           + _dot(att, wout_ref[D_CONV + D_SSM:D_MODEL, :]))
    x1 = _layer_norm(alpha * xin[...] + mix, g1_ref[...], b1_ref[...])
    x1_ref[0] = x1
    _route_rows(x1, wr_ref, br_ref, tri_ref, xp_ref, meta_ref, cnt_ref, running)


def _mixer_call(x, prev_moe, hist0, h0, kpad, vpad, perm, permt, lw, wr, br, *, lc, alpha):
    nq, rows, _ = x.shape
    tm = lc * SEQS_PER_STREAM
    nt = rows // tm
    fuse_in = prev_moe is not None
    kern = functools.partial(_mixer_kernel, tm=tm, lc=lc, alpha=alpha, fuse_in=fuse_in)
    tri = jnp.asarray(np.tril(np.ones((tm, tm), np.float32), -1), BF16)

    def const(shape):
        return pl.BlockSpec(shape, lambda q, i: (0,) * len(shape))

    def flat(shape):
        return pl.BlockSpec(shape, lambda q, i: (0,) * (len(shape) - 2) + (q * nt + i, 0))

    in_specs = [pl.BlockSpec((1, tm, D_MODEL), lambda q, i: (q, i, 0))]
    operands = [x]
    if fuse_in:
        in_specs += [flat((2, tm, D_MODEL // 2)), flat((tm, META_COLS)),
                     const((1, D_MODEL)), const((1, D_MODEL))]
        operands += list(prev_moe)
    in_specs += [
        pl.BlockSpec((1, HIST_ROWS, D_CONV), lambda q, i: (q, 0, 0)),
        pl.BlockSpec((1, SUBLANES, D_STATE), lambda q, i: (q, 0, 0)),
        pl.BlockSpec((1, SEQS_PER_STREAM, N_MEM_HEADS, D_ATT, N_MEM), lambda q, i: (q, 0, 0, 0, 0)),
        pl.BlockSpec((1, SEQS_PER_STREAM, N_MEM_HEADS, N_MEM, D_ATT), lambda q, i: (q, 0, 0, 0, 0)),
        const((tm, tm)), const((tm, tm)),
        const((D_MODEL, D_IN)),
        const((CONV_WIDTH * SUBLANES, D_CONV)), const((1, D_CONV)), const((1, D_CONV)), const((1, D_CONV)),
        const((SUBLANES, D_STATE)), const((SUBLANES, D_STATE)),
        const((D_SSM, D_STATE)), const((D_SSM, D_STATE)),
        const((D_STATE, D_SSM)), const((D_STATE, D_SSM)),
        const((1, D_SSM)), const((D_SSM, D_SSM)), const((1, D_SSM)),
        const((D_MODEL, D_MODEL)), const((1, D_MODEL)), const((1, D_MODEL)),
        const((D_MODEL, N_EXPERTS)), const((1, N_EXPERTS)), const((tm, tm)),
    ]
    out_specs = [
        pl.BlockSpec((1, tm, D_MODEL), lambda q, i: (q, i, 0)),
        pl.BlockSpec((1, HIST_ROWS, D_CONV), lambda q, i: (q, 0, 0)),
        pl.BlockSpec((1, SUBLANES, D_STATE), lambda q, i: (q, 0, 0)),
        flat((tm, D_MODEL // 2)), flat((tm, META_COLS)), const((1, N_EXPERTS)),
    ]
    out_shape = [
        jax.ShapeDtypeStruct((nq, rows, D_MODEL), F32),
        jax.ShapeDtypeStruct((nq, HIST_ROWS, D_CONV), F32),
        jax.ShapeDtypeStruct((nq, SUBLANES, D_STATE), F32),
        jax.ShapeDtypeStruct((nq * rows, D_MODEL // 2), jnp.int32),
        jax.ShapeDtypeStruct((nq * rows, META_COLS), F32),
        jax.ShapeDtypeStruct((1, N_EXPERTS), F32),
    ]
    scratch = [
        pltpu.VMEM((HIST_ROWS + tm + SUBLANES, D_CONV), F32),
        pltpu.VMEM((HIST_ROWS + tm + SUBLANES, D_CONV), F32),
        pltpu.VMEM((tm, D_CONV), F32),
        pltpu.VMEM((tm + SUBLANES, D_SSM), F32),
        pltpu.VMEM((tm + SUBLANES, D_SSM), F32),
        pltpu.VMEM((tm, D_STATE), F32),
        pltpu.VMEM((tm + SUBLANES, D_STATE), F32),
        pltpu.VMEM((tm, D_STATE), F32),
        pltpu.VMEM((tm + SUBLANES, D_STATE), F32),
        pltpu.VMEM((SUBLANES, D_STATE), F32),
        pltpu.VMEM((tm, D_MODEL), F32),
        pltpu.VMEM((1, N_EXPERTS), F32),
    ]
    return pl.pallas_call(
        kern,
        grid=(nq, nt),
        in_specs=in_specs,
        out_specs=out_specs,
        out_shape=out_shape,
        scratch_shapes=scratch,
        compiler_params=pltpu.CompilerParams(
            dimension_semantics=("arbitrary", "arbitrary"),
            vmem_limit_bytes=VMEM_LIMIT_BYTES),
        name="mixer",
    )(*operands, hist0, h0, kpad, vpad, perm, permt,
      lw["w_in"], lw["wdw"], lw["bdw"], lw["clg"], lw["clb"], lw["a1"], lw["a2"],
      lw["bre"], lw["bim"], lw["cre"], lw["cim"], lw["d"], lw["wglu"], lw["bglu"],
      lw["w_out"], lw["g1"], lw["b1"], wr, br, tri)


def _route(logits):
    shape = logits.shape
    m = jnp.max(logits, axis=-1, keepdims=True)
    e = jnp.exp(logits - m)
    aff = e / jnp.sum(e, axis=-1, keepdims=True)
    lane = lax.broadcasted_iota(jnp.int32, shape, 1)
    lane_f = lane.astype(F32)
    grp = lax.shift_right_logical(lane, 2)
    neg = -1.0

    def top2(vals):
        v1 = jnp.max(vals, axis=-1, keepdims=True)
        i1 = jnp.min(jnp.where(vals == v1, lane_f, float(N_EXPERTS)), axis=-1, keepdims=True)
        rest = jnp.where(lane_f == i1, neg, vals)
        v2 = jnp.max(rest, axis=-1, keepdims=True)
        i2 = jnp.min(jnp.where(rest == v2, lane_f, float(N_EXPERTS)), axis=-1, keepdims=True)
        return v1, i1, v2, i2

    best = None
    sel = None
    for gi in range(N_EXPERT_GROUPS):
        v1, _, v2, _ = top2(jnp.where(grp == gi, aff, neg))
        score = v1 + v2
        if best is None:
            best, sel = score, jnp.zeros_like(score, dtype=jnp.int32)
        else:
            better = score > best
            sel = jnp.where(better, gi, sel)
            best = jnp.where(better, score, best)
    v1, i1, v2, i2 = top2(jnp.where(grp == sel, aff, neg))
    denom = v1 + v2
    return i1, i2, v1 / denom, v2 / denom


def _pack_halves(y):
    half = y.shape[1] // 2
    lo = lax.bitcast_convert_type(y[:, :half].astype(BF16).astype(F32), jnp.int32)
    hi = lax.bitcast_convert_type(y[:, half:].astype(BF16).astype(F32), jnp.int32)
    return lax.shift_right_logical(lo, 16) | (hi & HI_HALF_MASK)


def _unpack_halves(w):
    lo = lax.bitcast_convert_type(lax.shift_left(w, 16), F32)
    hi = lax.bitcast_convert_type(w & HI_HALF_MASK, F32)
    return lo, hi


def _route_rows(x, wr_ref, br_ref, tri_ref, xp_ref, meta_ref, cnt_ref, running):
    xp_ref[...] = _pack_halves(x)
    i1, i2, g1, g2 = _route(_dot(x.astype(BF16), wr_ref[...]) + br_ref[...])
    tm = x.shape[0]
    lane_f = lax.broadcasted_iota(jnp.int32, (tm, N_EXPERTS), 1).astype(F32)
    hot1 = jnp.where(lane_f == i1, 1.0, 0.0)
    hot2 = jnp.where(lane_f == i2, 1.0, 0.0)
    both = hot1 + hot2
    before = _dot(tri_ref[...], both.astype(BF16)) + running[...]
    r1 = jnp.sum(hot1 * before, axis=-1, keepdims=True)
    r2 = jnp.sum(hot2 * before, axis=-1, keepdims=True)
    running[...] = running[...] + jnp.sum(both, axis=0, keepdims=True)
    cnt_ref[...] = running[...]
    col = lax.broadcasted_iota(jnp.int32, (tm, META_COLS), 1)
    meta = jnp.where(col == 0, i1, jnp.where(col == 1, i2, jnp.where(col == 2, g1, jnp.where(
        col == 3, g2, jnp.where(col == 4, r1, jnp.where(col == 5, r2, 0.0))))))
    meta_ref[...] = meta


def _expert_kernel(te_ref, nt_ref, xs_ref, wup_ref, bup_ref, wdn_ref, ys_ref, wup_bf, wdn_bf):
    i = pl.program_id(0)
    prev = te_ref[jnp.maximum(i - 1, 0)]

    @pl.when((i == 0) | (te_ref[i] != prev))
    def _():
        wup_bf[...] = wup_ref[0].astype(BF16)
        wdn_bf[...] = wdn_ref[0].astype(BF16)

    @pl.when(i < nt_ref[0])
    def _():
        lo, hi = _unpack_halves(xs_ref[...])
        half = D_MODEL // 2
        h = (_dot(lo.astype(BF16), wup_bf[0:half, :])
             + _dot(hi.astype(BF16), wup_bf[half:D_MODEL, :]) + bup_ref[0])
        ys_ref[...] = _pack_halves(_dot(_gelu_tanh(h).astype(BF16), wdn_bf[...]))


def _expert_call(xs, tile_expert, n_tiles, lw, *, rt):
    rows = xs.shape[0]

    def row_map(i, te, nt):
        return (jnp.minimum(i, nt[0] - 1), 0)

    def w_map(i, te, nt):
        return (te[i], 0, 0)

    return pl.pallas_call(
        _expert_kernel,
        grid_spec=pltpu.PrefetchScalarGridSpec(
            num_scalar_prefetch=2,
            grid=(rows // rt,),
            in_specs=[
                pl.BlockSpec((rt, D_MODEL // 2), row_map),
                pl.BlockSpec((1, D_MODEL, D_EXPERT), w_map),
                pl.BlockSpec((1, 1, D_EXPERT), w_map),
                pl.BlockSpec((1, D_EXPERT, D_MODEL), w_map),
            ],
            out_specs=pl.BlockSpec((rt, D_MODEL // 2), row_map),
            scratch_shapes=[pltpu.VMEM((D_MODEL, D_EXPERT), BF16),
                            pltpu.VMEM((D_EXPERT, D_MODEL), BF16)],
        ),
        out_shape=jax.ShapeDtypeStruct((rows, D_MODEL // 2), jnp.int32),
        compiler_params=pltpu.CompilerParams(
            dimension_semantics=("arbitrary",), vmem_limit_bytes=VMEM_LIMIT_BYTES),
        name="experts",
    )(tile_expert, n_tiles, xs, lw["w_up"], lw["b_up"], lw["w_down"])


def _moe_combine(x1, yg_ref, meta, g, b, alpha):
    g1 = meta[:, 2:3]
    g2 = meta[:, 3:4]
    lo1, hi1 = _unpack_halves(yg_ref[0])
    lo2, hi2 = _unpack_halves(yg_ref[1])
    moe = jnp.concatenate([g1 * lo1 + g2 * lo2, g1 * hi1 + g2 * hi2], axis=1)
    return _layer_norm(alpha * x1 + moe, g, b)


def _combine_kernel(x_ref, yg_ref, meta_ref, g2_ref, b2_ref, o_ref, *, alpha):
    o_ref[...] = _moe_combine(x_ref[...], yg_ref, meta_ref[...], g2_ref[...], b2_ref[...], alpha)


def _combine_call(x1, yg, meta, lw, *, alpha):
    rows = x1.shape[0]
    tmc = min(1024, rows)
    return pl.pallas_call(
        functools.partial(_combine_kernel, alpha=alpha),
        grid=(rows // tmc,),
        in_specs=[
            pl.BlockSpec((tmc, D_MODEL), lambda i: (i, 0)),
            pl.BlockSpec((2, tmc, D_MODEL // 2), lambda i: (0, i, 0)),
            pl.BlockSpec((tmc, META_COLS), lambda i: (i, 0)),
            pl.BlockSpec((1, D_MODEL), lambda i: (0, 0)),
            pl.BlockSpec((1, D_MODEL), lambda i: (0, 0)),
        ],
        out_specs=pl.BlockSpec((tmc, D_MODEL), lambda i: (i, 0)),
        out_shape=jax.ShapeDtypeStruct((rows, D_MODEL), F32),
        compiler_params=pltpu.CompilerParams(
            dimension_semantics=("arbitrary",), vmem_limit_bytes=VMEM_LIMIT_BYTES),
        name="combine",
    )(x1, yg, meta, lw["g2"], lw["b2"])


def _sc_mesh():
    return plsc.VectorSubcoreMesh(core_axis_name="c", subcore_axis_name="s")


def _sc_chunk(rows):
    per_worker = rows // SC_WORKERS
    chunk = min(SC_MAX_CHUNK, per_worker)
    assert per_worker % chunk == 0 and chunk % 8 == 0
    return per_worker // chunk, chunk


def _sc_dispatch(xp, pos, n_slots):
    rows, width = xp.shape
    nch, ch = _sc_chunk(rows)
    posr = pos.reshape(2, SC_WORKERS, nch, ch)

    @functools.partial(
        pl.kernel, mesh=_sc_mesh(),
        out_type=jax.ShapeDtypeStruct((n_slots, width), jnp.int32),
        scratch_types=[pltpu.VMEM((ch,), jnp.int32), pltpu.VMEM((ch,), jnp.int32),
                       pltpu.VMEM((ch, width), jnp.int32), pltpu.SemaphoreType.DMA],
        name="sc_dispatch")
    def k(x_hbm, pos_hbm, o_hbm, idx0, idx1, buf, sem):
        wid = lax.axis_index("s") * SC_CORES + lax.axis_index("c")

        @pl.loop(0, nch)
        def _(c):
            base = (wid * nch + c) * ch
            pltpu.sync_copy(x_hbm.at[pl.ds(base, ch)], buf)
            pltpu.sync_copy(pos_hbm.at[0, wid, c], idx0)
            pltpu.sync_copy(pos_hbm.at[1, wid, c], idx1)
            pltpu.async_copy(buf, o_hbm.at[idx0], sem).wait()
            pltpu.async_copy(buf, o_hbm.at[idx1], sem).wait()

    return k(xp, posr)


def _sc_gather(ys, pos):
    width = ys.shape[1]
    rows = pos.shape[1]
    nch, ch = _sc_chunk(rows)
    posr = pos.reshape(2, SC_WORKERS, nch, ch)

    @functools.partial(
        pl.kernel, mesh=_sc_mesh(),
        out_type=jax.ShapeDtypeStruct((2, rows, width), jnp.int32),
        scratch_types=[pltpu.VMEM((ch,), jnp.int32), pltpu.VMEM((ch, width), jnp.int32),
                       pltpu.SemaphoreType.DMA],
        name="sc_gather")
    def k(y_hbm, pos_hbm, o_hbm, idx, buf, sem):
        wid = lax.axis_index("s") * SC_CORES + lax.axis_index("c")

        @pl.loop(0, nch)
        def _(c):
            base = (wid * nch + c) * ch
            for kk in range(2):
                pltpu.sync_copy(pos_hbm.at[kk, wid, c], idx)
                pltpu.async_copy(y_hbm.at[idx], buf, sem).wait()
                pltpu.sync_copy(buf, o_hbm.at[kk, pl.ds(base, ch)])

    return k(ys, posr)


def _moe_rows(xp, meta, counts, lw):
    rows = xp.shape[0]
    rt = 512 if rows >= 8192 else 128
    n_slots = 2 * rows + N_EXPERTS * rt
    nt_max = n_slots // rt

    cnt = counts[0].astype(jnp.int32)
    padded = ((cnt + rt - 1) // rt) * rt
    ends = jnp.cumsum(padded)
    offs = ends - padded
    eidx = meta[:, 0:2].astype(jnp.int32)
    rank = meta[:, 4:6].astype(jnp.int32)
    experts = jnp.arange(N_EXPERTS, dtype=jnp.int32)
    pos = (jnp.sum(jnp.where(eidx[..., None] == experts, offs, 0), axis=-1) + rank).T
    n_tiles = ends[-1:] // rt
    tiles = jnp.minimum(jnp.arange(nt_max, dtype=jnp.int32), n_tiles[0] - 1)
    tile_expert = jnp.minimum(jnp.sum((ends // rt)[None, :] <= tiles[:, None], axis=1),
                              N_EXPERTS - 1).astype(jnp.int32)

    xs = _sc_dispatch(xp, pos, n_slots)
    ys = _expert_call(xs, tile_expert, n_tiles.astype(jnp.int32), lw, rt=rt)
    return _sc_gather(ys, pos)


def _memkv_kernel(m_ref, wk_ref, wv_ref, k_ref, v_ref):
    mb = m_ref[...].astype(BF16)
    k_ref[0] = _dot(mb, wk_ref[0])
    v_ref[0] = _dot(mb, wv_ref[0])


def _memkv_call(mem, wk, wv):
    depth = wk.shape[0]
    rows = mem.shape[0]
    return pl.pallas_call(
        _memkv_kernel,
        grid=(depth,),
        in_specs=[
            pl.BlockSpec((rows, D_MODEL), lambda l: (0, 0)),
            pl.BlockSpec((1, D_MODEL, D_ATT), lambda l: (l, 0, 0)),
            pl.BlockSpec((1, D_MODEL, D_ATT), lambda l: (l, 0, 0)),
        ],
        out_specs=[
            pl.BlockSpec((1, rows, D_ATT), lambda l: (l, 0, 0)),
            pl.BlockSpec((1, rows, D_ATT), lambda l: (l, 0, 0)),
        ],
        out_shape=[jax.ShapeDtypeStruct((depth, rows, D_ATT), F32)] * 2,
        compiler_params=pltpu.CompilerParams(
            dimension_semantics=("arbitrary",), vmem_limit_bytes=VMEM_LIMIT_BYTES),
        name="memkv",
    )(mem, wk, wv)


def _perm_matrices(lc):
    tm = lc * SEQS_PER_STREAM
    p = np.zeros((tm, tm), np.float32)
    for s in range(SEQS_PER_STREAM):
        for t in range(lc):
            p[s * lc + t, t * SEQS_PER_STREAM + s] = 1.0
    return jnp.asarray(p, BF16), jnp.asarray(p.T, BF16)


def _to_streams(a):
    b, l, c = a.shape
    q = b // SEQS_PER_STREAM
    return a.reshape(q, SEQS_PER_STREAM, l, c).transpose(0, 2, 1, 3).reshape(q, l * SEQS_PER_STREAM, c)


def _from_streams(a, l):
    q, _, c = a.shape
    return a.reshape(q, l, SEQS_PER_STREAM, c).transpose(0, 2, 1, 3).reshape(q * SEQS_PER_STREAM, l, c)


def _pack_state(re, im):
    b = re.shape[0]
    q = b // SEQS_PER_STREAM
    re = re.reshape(q, SEQS_PER_STREAM, D_STATE)
    im = im.reshape(q, SEQS_PER_STREAM, D_STATE)
    return jnp.concatenate([im, re], axis=1)


def _unpack_state(h):
    q = h.shape[0]
    im = h[:, 0:SEQS_PER_STREAM].reshape(q * SEQS_PER_STREAM, N_SSM_GROUPS, SSM_STATE)
    re = h[:, SEQS_PER_STREAM:].reshape(q * SEQS_PER_STREAM, N_SSM_GROUPS, SSM_STATE)
    return re, im


def _pad_heads(mk, mv):
    b = mk.shape[0]
    q = b // SEQS_PER_STREAM
    eye = jnp.eye(N_MEM_HEADS, dtype=mk.dtype)
    kt = mk.transpose(0, 2, 3, 1)
    kp = jnp.einsum("bhdm,hg->bhgdm", kt, eye).reshape(b, N_MEM_HEADS, D_ATT, N_MEM)
    vt = mv.transpose(0, 2, 1, 3)
    vp = jnp.einsum("bhmd,hg->bhmgd", vt, eye).reshape(b, N_MEM_HEADS, N_MEM, D_ATT)
    kp = kp.reshape(q, SEQS_PER_STREAM, N_MEM_HEADS, D_ATT, N_MEM).astype(BF16)
    vp = vp.reshape(q, SEQS_PER_STREAM, N_MEM_HEADS, N_MEM, D_ATT).astype(BF16)
    return kp, vp


def _layer_params(l, w_in, w_dw, b_dw, conv_ln_g, conv_ln_b, ssm_a_re, ssm_a_im, ssm_b_re,
                  ssm_b_im, ssm_c_re, ssm_c_im, ssm_d, ssm_log_dt, ssm_w_glu, ssm_b_glu,
                  w_out, ln1_g, ln1_b, w_up, b_up, w_down, ln2_g, ln2_b):
    g, p = N_SSM_GROUPS, SSM_STATE
    a = lax.complex(ssm_a_re[l], ssm_a_im[l])
    dt = jnp.exp(ssm_log_dt[l])[:, None]
    a_bar = jnp.exp(a * dt)
    b_bar = ((a_bar - 1.0) / a)[..., None] * lax.complex(ssm_b_re[l], ssm_b_im[l])
    eye = jnp.eye(g, dtype=F32)

    def b_block(m):
        return jnp.einsum("gpi,gh->gihp", m, eye).reshape(D_SSM, D_STATE).astype(BF16)

    def c_block(m):
        return jnp.einsum("gip,gh->gphi", m, eye).reshape(D_STATE, D_SSM).astype(BF16)

    ar = jnp.real(a_bar).reshape(1, D_STATE)
    ai = jnp.imag(a_bar).reshape(1, D_STATE)
    half = SUBLANES // 2
    return {
        "w_in": w_in[l].astype(BF16),
        "wdw": jnp.repeat(w_dw[l], SUBLANES, axis=0),
        "bdw": b_dw[l][None], "clg": conv_ln_g[l][None], "clb": conv_ln_b[l][None],
        "a1": jnp.broadcast_to(ar, (SUBLANES, D_STATE)),
        "a2": jnp.concatenate([jnp.broadcast_to(-ai, (half, D_STATE)),
                               jnp.broadcast_to(ai, (half, D_STATE))], axis=0),
        "bre": b_block(jnp.real(b_bar)), "bim": b_block(jnp.imag(b_bar)),
        "cre": c_block(ssm_c_re[l]), "cim": c_block(-ssm_c_im[l]),
        "d": ssm_d[l][None], "wglu": ssm_w_glu[l].astype(BF16), "bglu": ssm_b_glu[l][None],
        "w_out": w_out[l].astype(BF16), "g1": ln1_g[l][None], "b1": ln1_b[l][None],
        "w_up": w_up[l], "b_up": b_up[l][:, None, :], "w_down": w_down[l],
        "g2": ln2_g[l][None], "b2": ln2_b[l][None],
    }


def kernel(x_prompt, x_sample, cache_conv, state_ssm_re, state_ssm_im, cache_mem_k, cache_mem_v, mem_prompt, w_in, w_dw, b_dw, conv_ln_g, conv_ln_b, ssm_a_re, ssm_a_im, ssm_b_re, ssm_b_im, ssm_c_re, ssm_c_im, ssm_d, ssm_log_dt, ssm_w_glu, ssm_b_glu, w_mem_k, w_mem_v, w_out, ln1_g, ln1_b, w_router, b_router, w_up, b_up, w_down, ln2_g, ln2_b):
    depth = w_in.shape[0]
    alpha = (2.0 * depth) ** 0.25
    bp, seq, _ = x_prompt.shape
    bs, dec_seq, _ = x_sample.shape
    assert bp == SEQS_PER_STREAM and bs % SEQS_PER_STREAM == 0
    lc_p = min(128, seq)
    lc_s = dec_seq
    assert seq % lc_p == 0 and lc_p % 16 == 0 and lc_s % 16 == 0

    mb, mm, _ = mem_prompt.shape
    mk_all, mv_all = _memkv_call(mem_prompt.reshape(mb * mm, D_MODEL),
                                 w_mem_k.astype(BF16), w_mem_v.astype(BF16))
    mk_all = mk_all.reshape(depth, mb, mm, N_MEM_HEADS, MEM_HEAD_DIM)
    mv_all = mv_all.reshape(depth, mb, mm, N_MEM_HEADS, MEM_HEAD_DIM)

    perm_p, permt_p = _perm_matrices(lc_p)
    perm_s, permt_s = _perm_matrices(lc_s)
    wr = w_router.astype(BF16)
    br = b_router[None]

    xp = _to_streams(x_prompt)
    xs = _to_streams(x_sample)
    nq_s = bs // SEQS_PER_STREAM
    zero_hist = jnp.zeros((1, HIST_ROWS, D_CONV), F32)
    zero_h = jnp.zeros((1, SUBLANES, D_STATE), F32)

    conv_p, re_p, im_p, conv_s, re_s, im_s = [], [], [], [], [], []
    moe_p = moe_s = None
    for l in range(depth):
        lw = _layer_params(l, w_in, w_dw, b_dw, conv_ln_g, conv_ln_b, ssm_a_re, ssm_a_im,
                           ssm_b_re, ssm_b_im, ssm_c_re, ssm_c_im, ssm_d, ssm_log_dt, ssm_w_glu,
                           ssm_b_glu, w_out, ln1_g, ln1_b, w_up, b_up, w_down, ln2_g, ln2_b)
        kp, vp = _pad_heads(mk_all[l], mv_all[l])
        xp, hist, hst, rows_p, meta_p, cnt_p = _mixer_call(
            xp, moe_p, zero_hist, zero_h, kp, vp, perm_p, permt_p, lw, wr, br, lc=lc_p, alpha=alpha)
        moe_p = (_moe_rows(rows_p, meta_p, cnt_p, lw), meta_p, lw["g2"], lw["b2"])
        conv_p.append(_from_streams(hist, CONV_BUF))
        hr, hi = _unpack_state(hst)
        re_p.append(hr)
        im_p.append(hi)
        kp, vp = _pad_heads(cache_mem_k[l], cache_mem_v[l])
        xs, hist, hst, rows_s, meta_s, cnt_s = _mixer_call(
            xs, moe_s, _to_streams(cache_conv[l]), _pack_state(state_ssm_re[l], state_ssm_im[l]),
            kp, vp, perm_s, permt_s, lw, wr, br, lc=lc_s, alpha=alpha)
        moe_s = (_moe_rows(rows_s, meta_s, cnt_s, lw), meta_s, lw["g2"], lw["b2"])
        conv_s.append(_from_streams(hist, CONV_BUF))
        hr, hi = _unpack_state(hst)
        re_s.append(hr)
        im_s.append(hi)

    xp = _combine_call(xp.reshape(-1, D_MODEL), moe_p[0], moe_p[1], lw, alpha=alpha).reshape(xp.shape)
    xs = _combine_call(xs.reshape(-1, D_MODEL), moe_s[0], moe_s[1], lw, alpha=alpha).reshape(xs.shape)

    return (_from_streams(xp, seq), _from_streams(xs, dec_seq),
            jnp.stack(conv_p), jnp.stack(re_p), jnp.stack(im_p), mk_all, mv_all,
            jnp.stack(conv_s), jnp.stack(re_s), jnp.stack(im_s))
```

```python
import functools
import math

import numpy as np
import jax
import jax.numpy as jnp
from jax import lax
from jax.experimental import pallas as pl
from jax.experimental.pallas import tpu as pltpu
from jax.experimental.pallas import tpu_sc as plsc

F32 = jnp.float32
BF16 = jnp.bfloat16

D_MODEL = 1024
CONV_WIDTH = 31
CONV_BUF = CONV_WIDTH - 1
D_CONV = 384
D_SSM = 384
SSM_GROUP = 16
N_SSM_GROUPS = D_SSM // SSM_GROUP
SSM_STATE = 64
D_STATE = N_SSM_GROUPS * SSM_STATE
N_MEM = 256
N_MEM_HEADS = 4
MEM_HEAD_DIM = 64
D_ATT = N_MEM_HEADS * MEM_HEAD_DIM
D_IN = 2 * D_CONV + D_SSM + D_ATT
N_EXPERTS = 16
N_EXPERT_GROUPS = 4
EXPERTS_PER_GROUP = N_EXPERTS // N_EXPERT_GROUPS
D_EXPERT = 512
LN_EPS = 1e-5

SEQS_PER_STREAM = 4
HIST_ROWS = CONV_BUF * SEQS_PER_STREAM
SUBLANES = 8
LANES = 128
SCAN_LANES = 512
SSM_BLOCKS = D_STATE // SCAN_LANES
CONV_ROWS = 32
EXPERT_SUB_ROWS = 512
VMEM_LIMIT_BYTES = 56 * 1024 * 1024
HI_HALF_MASK = np.int32(-65536)
META_COLS = 8
SC_CORES = 2
SC_WORKERS = SC_CORES * 16
SC_MAX_CHUNK = 128


def _sigmoid(x):
    return 1.0 / (1.0 + jnp.exp(-x))


def _gelu_tanh(x):
    c = math.sqrt(2.0 / math.pi)
    return 0.5 * x * (1.0 + jnp.tanh(c * (x + 0.044715 * (x * x * x))))


def _layer_norm(z, g, b):
    mu = jnp.mean(z, axis=-1, keepdims=True)
    zc = z - mu
    var = jnp.mean(zc * zc, axis=-1, keepdims=True)
    return zc * lax.rsqrt(var + LN_EPS) * g + b


def _dot(a, b):
    return jnp.dot(a, b, preferred_element_type=F32)


def _mixer_kernel(*refs, tm, lc, alpha, fuse_in):
    refs = list(refs)
    if fuse_in:
        xprev_ref, yg_ref, metain_ref, g2p_ref, b2p_ref = refs[:5]
        refs = refs[5:]
    else:
        xprev_ref = refs.pop(0)
        xslab = refs.pop()
    (hist0_ref, h0_ref, k_ref, v_ref, perm_ref, permt_ref,
     w_in_ref, wdw_ref, bdw_ref, clg_ref, clb_ref, a1_ref, a2_ref,
     bre_ref, bim_ref, cre_ref, cim_ref, d_ref, wglu_ref, bglu_ref,
     wout_ref, g1_ref, b1_ref, wr_ref, br_ref, tri_ref,
     x1_ref, hist_out_ref, h_out_ref, xp_ref, meta_ref, metat_ref, cnt_ref,
     xpad0, xpad4, cy, ush, yim, bu_re, bu_im, hre, him, hcar, xin, running) = refs
    i = pl.program_id(1)

    @pl.when(i == 0)
    def _():
        xpad0[0:HIST_ROWS, :] = hist0_ref[0]
        hcar[...] = h0_ref[0]
        ush[...] = jnp.zeros_like(ush)

    @pl.when((i == 0) & (pl.program_id(0) == 0))
    def _():
        running[...] = jnp.zeros_like(running)

    if fuse_in:
        xin[...] = _moe_combine(xprev_ref[0], yg_ref, metain_ref[...], g2p_ref[...], b2p_ref[...], alpha)
    else:
        nslab = D_MODEL // LANES
        for s in range(SEQS_PER_STREAM):
            for j in range(nslab):
                xslab[j, pl.ds(s, lc, stride=SEQS_PER_STREAM), :] = xprev_ref[s, :, j * LANES:(j + 1) * LANES]
        xin[...] = jnp.concatenate([xslab[j] for j in range(nslab)], axis=1)
    x = xin[...]
    proj = _dot(x.astype(BF16), w_in_ref[...])

    g = proj[:, 0:D_CONV] * _sigmoid(proj[:, D_CONV:2 * D_CONV])
    xpad0[HIST_ROWS:HIST_ROWS + tm, :] = g
    xpad4[0:HIST_ROWS + tm - 4, :] = xpad0[4:HIST_ROWS + tm, :]

    nsub = CONV_ROWS // SUBLANES

    def conv_rows(rb, carry):
        r0 = pl.multiple_of(rb * CONV_ROWS, CONV_ROWS)
        accs = [jnp.broadcast_to(bdw_ref[...], (SUBLANES, D_CONV)) for _ in range(nsub)]
        for k in range(CONV_WIDTH):
            wk = wdw_ref[SUBLANES * k:SUBLANES * (k + 1), :]
            for sb in range(nsub):
                off = r0 + SEQS_PER_STREAM * k + SUBLANES * sb
                if k % 2 == 0:
                    xs = xpad0[pl.ds(pl.multiple_of(off, SUBLANES), SUBLANES), :]
                else:
                    xs = xpad4[pl.ds(pl.multiple_of(off - 4, SUBLANES), SUBLANES), :]
                accs[sb] = accs[sb] + xs * wk
        for sb in range(nsub):
            cy[pl.ds(pl.multiple_of(r0 + SUBLANES * sb, SUBLANES), SUBLANES), :] = accs[sb]
        return carry

    lax.fori_loop(0, tm // CONV_ROWS, conv_rows, 0)
    conv_n = _layer_norm(cy[...], clg_ref[...], clb_ref[...])
    cy[...] = conv_n * _sigmoid(conv_n)

    new_hist = xpad0[tm:tm + HIST_ROWS, :]
    xpad0[0:HIST_ROWS, :] = new_hist
    hist_out_ref[0] = new_hist

    u = proj[:, 2 * D_CONV:2 * D_CONV + D_SSM]
    ush[4:tm + 4, :] = u
    ub = u.astype(BF16)
    ub_sh = ush[...].astype(BF16)
    for m in range(SSM_BLOCKS):
        ch = slice(m * LANES, (m + 1) * LANES)
        st = slice(m * SCAN_LANES, (m + 1) * SCAN_LANES)
        bu_re[:, st] = _dot(ub[:, ch], bre_ref[m])
        bu_im[:, st] = _dot(ub_sh[:, ch], bim_ref[m])

    lo = lax.broadcasted_iota(jnp.int32, (SUBLANES, SCAN_LANES), 0) < 4
    for c in range(D_STATE // SCAN_LANES):
        cs = slice(c * SCAN_LANES, (c + 1) * SCAN_LANES)
        a1 = a1_ref[:, cs]
        a2 = a2_ref[:, cs]

        def scan_pair(j, carry, cs=cs, a1=a1, a2=a2):
            h_prev, im_cur = carry
            r = pl.multiple_of(j * SUBLANES, SUBLANES)
            re_cur = bu_re[pl.ds(r, SUBLANES), cs]
            im_next = bu_im[pl.ds(r + SUBLANES, SUBLANES), cs]
            p_even = jnp.where(lo, re_cur, im_cur)
            p_odd = jnp.where(lo, im_next, re_cur)
            h_even = a1 * pltpu.roll(h_prev, 4, 0) + a2 * h_prev + p_even
            h_odd = a1 * pltpu.roll(h_even, 4, 0) - a2 * h_even + p_odd
            hre[pl.ds(r, SUBLANES), cs] = jnp.where(lo, h_even, h_odd)
            him[pl.ds(r, SUBLANES), cs] = jnp.where(lo, h_prev, h_even)
            return h_odd, im_next

        h_last, _ = lax.fori_loop(0, lc // 2, scan_pair,
                                  (hcar[:, cs], bu_im[0:SUBLANES, cs]))
        him[tm:tm + SUBLANES, cs] = jnp.where(lo, h_last, 0.0)
        hcar[:, cs] = h_last
    h_out_ref[0] = hcar[...]

    y_re_blocks = []
    for m in range(SSM_BLOCKS):
        ch = slice(m * LANES, (m + 1) * LANES)
        st = slice(m * SCAN_LANES, (m + 1) * SCAN_LANES)
        y_re_blocks.append(_dot(hre[:, st].astype(BF16), cre_ref[m]))
        yim[:, ch] = _dot(him[:, st].astype(BF16), cim_ref[m])
    y_re = jnp.concatenate(y_re_blocks, axis=1)
    y = y_re + yim[4:tm + 4, :] + d_ref[...] * u
    y = _gelu_tanh(y)
    ssm_y = y * _sigmoid(_dot(y.astype(BF16), wglu_ref[...]) + bglu_ref[...])

    q = proj[:, 2 * D_CONV + D_SSM:D_IN].astype(BF16)
    q_seq = _dot(perm_ref[...], q).astype(BF16)
    outs = []
    for s in range(SEQS_PER_STREAM):
        qs = q_seq[s * lc:(s + 1) * lc, :]
        acc = jnp.zeros((lc, D_ATT), F32)
        for h in range(N_MEM_HEADS):
            sc = _dot(qs, k_ref[0, s, h]) * (MEM_HEAD_DIM ** -0.5)
            sc = sc - jnp.max(sc, axis=-1, keepdims=True)
            e = jnp.exp(sc)
            p = e * (1.0 / jnp.sum(e, axis=-1, keepdims=True))
            acc = acc + _dot(p.astype(BF16), v_ref[0, s, h])
        outs.append(acc)
    att_seq = jnp.concatenate(outs, axis=0).astype(BF16)
    att = _dot(permt_ref[...], att_seq).astype(BF16)

    mix = _dot(jnp.concatenate([cy[...].astype(BF16), ssm_y.astype(BF16), att], axis=1), wout_ref[...])
    x1 = _layer_norm(alpha * xin[...] + mix, g1_ref[...], b1_ref[...])
    x1_ref[0] = x1
    _route_rows(x1, wr_ref, br_ref, tri_ref, xp_ref, meta_ref, metat_ref, cnt_ref, running)


def _mixer_call(x, prev_moe, hist0, h0, kpad, vpad, perm, permt, lw, wr, br, *, lc, alpha):
    fuse_in = prev_moe is not None
    if fuse_in:
        nq, rows, _ = x.shape
    else:
        nq, rows = x.shape[0] // SEQS_PER_STREAM, x.shape[1] * SEQS_PER_STREAM
    tm = lc * SEQS_PER_STREAM
    nt = rows // tm
    kern = functools.partial(_mixer_kernel, tm=tm, lc=lc, alpha=alpha, fuse_in=fuse_in)
    triu = jnp.asarray(np.triu(np.ones((tm, tm), np.float32), 1), BF16)

    def const(shape):
        return pl.BlockSpec(shape, lambda q, i: (0,) * len(shape))

    def flat(shape):
        return pl.BlockSpec(shape, lambda q, i: (0,) * (len(shape) - 2) + (q * nt + i, 0))

    operands = [x]
    if fuse_in:
        in_specs = [pl.BlockSpec((1, tm, D_MODEL), lambda q, i: (q, i, 0)),
                    flat((2, tm, D_MODEL // 2)), flat((tm, META_COLS)),
                    const((1, D_MODEL)), const((1, D_MODEL))]
        operands += list(prev_moe)
    else:
        in_specs = [pl.BlockSpec((SEQS_PER_STREAM, lc, D_MODEL), lambda q, i: (q, i, 0))]
    in_specs += [
        pl.BlockSpec((1, HIST_ROWS, D_CONV), lambda q, i: (q, 0, 0)),
        pl.BlockSpec((1, SUBLANES, D_STATE), lambda q, i: (q, 0, 0)),
        pl.BlockSpec((1, SEQS_PER_STREAM, N_MEM_HEADS, D_ATT, N_MEM), lambda q, i: (q, 0, 0, 0, 0)),
        pl.BlockSpec((1, SEQS_PER_STREAM, N_MEM_HEADS, N_MEM, D_ATT), lambda q, i: (q, 0, 0, 0, 0)),
        const((tm, tm)), const((tm, tm)),
        const((D_MODEL, D_IN)),
        const((CONV_WIDTH * SUBLANES, D_CONV)), const((1, D_CONV)), const((1, D_CONV)), const((1, D_CONV)),
        const((SUBLANES, D_STATE)), const((SUBLANES, D_STATE)),
        const((SSM_BLOCKS, LANES, SCAN_LANES)), const((SSM_BLOCKS, LANES, SCAN_LANES)),
        const((SSM_BLOCKS, SCAN_LANES, LANES)), const((SSM_BLOCKS, SCAN_LANES, LANES)),
        const((1, D_SSM)), const((D_SSM, D_SSM)), const((1, D_SSM)),
        const((D_MODEL, D_MODEL)), const((1, D_MODEL)), const((1, D_MODEL)),
        const((N_EXPERTS, D_MODEL)), const((N_EXPERTS, 1)), const((tm, tm)),
    ]
    out_specs = [
        pl.BlockSpec((1, tm, D_MODEL), lambda q, i: (q, i, 0)),
        pl.BlockSpec((1, HIST_ROWS, D_CONV), lambda q, i: (q, 0, 0)),
        pl.BlockSpec((1, SUBLANES, D_STATE), lambda q, i: (q, 0, 0)),
        flat((tm, D_MODEL // 2)), flat((tm, META_COLS)),
        pl.BlockSpec((1, META_COLS, tm), lambda q, i: (q * nt + i, 0, 0)), const((N_EXPERTS, 1)),
    ]
    out_shape = [
        jax.ShapeDtypeStruct((nq, rows, D_MODEL), F32),
        jax.ShapeDtypeStruct((nq, HIST_ROWS, D_CONV), F32),
        jax.ShapeDtypeStruct((nq, SUBLANES, D_STATE), F32),
        jax.ShapeDtypeStruct((nq * rows, D_MODEL // 2), jnp.int32),
        jax.ShapeDtypeStruct((nq * rows, META_COLS), F32),
        jax.ShapeDtypeStruct((nq * nt, META_COLS, tm), F32),
        jax.ShapeDtypeStruct((N_EXPERTS, 1), F32),
    ]
    scratch = [
        pltpu.VMEM((HIST_ROWS + tm + SUBLANES, D_CONV), F32),
        pltpu.VMEM((HIST_ROWS + tm + SUBLANES, D_CONV), F32),
        pltpu.VMEM((tm, D_CONV), F32),
        pltpu.VMEM((tm + SUBLANES, D_SSM), F32),
        pltpu.VMEM((tm + SUBLANES, D_SSM), F32),
        pltpu.VMEM((tm, D_STATE), F32),
        pltpu.VMEM((tm + SUBLANES, D_STATE), F32),
        pltpu.VMEM((tm, D_STATE), F32),
        pltpu.VMEM((tm + SUBLANES, D_STATE), F32),
        pltpu.VMEM((SUBLANES, D_STATE), F32),
        pltpu.VMEM((tm, D_MODEL), F32),
        pltpu.VMEM((N_EXPERTS, 1), F32),
    ]
    if not fuse_in:
        scratch.append(pltpu.VMEM((D_MODEL // LANES, tm, LANES), F32))
    return pl.pallas_call(
        kern,
        grid=(nq, nt),
        in_specs=in_specs,
        out_specs=out_specs,
        out_shape=out_shape,
        scratch_shapes=scratch,
        compiler_params=pltpu.CompilerParams(
            dimension_semantics=("arbitrary", "arbitrary"),
            vmem_limit_bytes=VMEM_LIMIT_BYTES),
        name="mixer",
    )(*operands, hist0, h0, kpad, vpad, perm, permt,
      lw["w_in"], lw["wdw"], lw["bdw"], lw["clg"], lw["clb"], lw["a1"], lw["a2"],
      lw["bre"], lw["bim"], lw["cre"], lw["cim"], lw["d"], lw["wglu"], lw["bglu"],
      lw["w_out"], lw["g1"], lw["b1"], wr, br, triu)


def _route(logits_t):
    m = jnp.max(logits_t, axis=0, keepdims=True)
    e = jnp.exp(logits_t - m)
    aff = e / jnp.sum(e, axis=0, keepdims=True)
    rows = [aff[j:j + 1, :] for j in range(N_EXPERTS)]

    scores = []
    for gi in range(N_EXPERT_GROUPS):
        a, b, c, d = rows[EXPERTS_PER_GROUP * gi:EXPERTS_PER_GROUP * (gi + 1)]
        hi1, lo1 = jnp.maximum(a, b), jnp.minimum(a, b)
        hi2, lo2 = jnp.maximum(c, d), jnp.minimum(c, d)
        scores.append(jnp.maximum(hi1, hi2) + jnp.maximum(jnp.minimum(hi1, hi2), jnp.maximum(lo1, lo2)))
    best = scores[0]
    sel = jnp.zeros_like(best)
    for gi in range(1, N_EXPERT_GROUPS):
        better = scores[gi] > best
        sel = jnp.where(better, float(gi), sel)
        best = jnp.where(better, scores[gi], best)

    hot1, hot2 = [], []
    for gi in range(N_EXPERT_GROUPS):
        vals = rows[EXPERTS_PER_GROUP * gi:EXPERTS_PER_GROUP * (gi + 1)]
        chosen = sel == float(gi)
        for j in range(EXPERTS_PER_GROUP):
            ahead = jnp.zeros_like(best)
            for k in range(EXPERTS_PER_GROUP):
                if k < j:
                    ahead = ahead + jnp.where(vals[k] >= vals[j], 1.0, 0.0)
                elif k > j:
                    ahead = ahead + jnp.where(vals[k] > vals[j], 1.0, 0.0)
            hot1.append(jnp.where(chosen, jnp.where(ahead == 0.0, 1.0, 0.0), 0.0))
            hot2.append(jnp.where(chosen, jnp.where(ahead == 1.0, 1.0, 0.0), 0.0))
    return jnp.concatenate(hot1, axis=0), jnp.concatenate(hot2, axis=0), aff


def _pack_halves(y):
    half = y.shape[1] // 2
    lo = lax.bitcast_convert_type(y[:, :half].astype(BF16).astype(F32), jnp.int32)
    hi = lax.bitcast_convert_type(y[:, half:].astype(BF16).astype(F32), jnp.int32)
    return lax.shift_right_logical(lo, 16) | (hi & HI_HALF_MASK)


def _unpack_halves(w):
    lo = lax.bitcast_convert_type(lax.shift_left(w, 16), F32)
    hi = lax.bitcast_convert_type(w & HI_HALF_MASK, F32)
    return lo, hi


def _route_rows(x, wr_ref, br_ref, triu_ref, xp_ref, meta_ref, metat_ref, cnt_ref, running):
    tm = x.shape[0]
    xp_ref[...] = _pack_halves(x)
    logits_t = lax.dot_general(wr_ref[...], x.astype(BF16), (((1,), (1,)), ((), ())),
                               preferred_element_type=F32) + br_ref[...]
    hot1, hot2, aff = _route(logits_t)
    eid = lax.broadcasted_iota(jnp.int32, (N_EXPERTS, tm), 0).astype(F32)
    both = hot1 + hot2
    before = _dot(both.astype(BF16), triu_ref[...]) + running[...]

    def pick(hot, vals):
        return jnp.sum(hot * vals, axis=0, keepdims=True)

    v1 = pick(hot1, aff)
    v2 = pick(hot2, aff)
    denom = v1 + v2
    meta_t = jnp.concatenate(
        [pick(hot1, eid), pick(hot2, eid), v1 / denom, v2 / denom, pick(hot1, before), pick(hot2, before),
         jnp.zeros((LANES - 6, tm), F32)], axis=0)
    metat_ref[0] = meta_t[0:META_COLS, :]
    meta_ref[...] = meta_t.T[:, 0:META_COLS]
    running[...] = running[...] + jnp.sum(both, axis=1, keepdims=True)
    cnt_ref[...] = running[...]


def _expert_kernel(te_ref, nt_ref, xs_ref, wup_ref, bup_ref, wdn_ref, ys_ref, wup_bf, wdn_bf):
    i = pl.program_id(0)
    prev = te_ref[jnp.maximum(i - 1, 0)]

    @pl.when((i == 0) | (te_ref[i] != prev))
    def _():
        wup_bf[...] = wup_ref[0, 0].astype(BF16)
        wdn_bf[...] = wdn_ref[0, 0].astype(BF16)

    @pl.when(i < nt_ref[0])
    def _():
        half = D_MODEL // 2
        rt = xs_ref.shape[0]
        sub = min(EXPERT_SUB_ROWS, rt)
        for r in range(rt // sub):
            rows = slice(r * sub, (r + 1) * sub)
            lo, hi = _unpack_halves(xs_ref[rows, :])
            h = (_dot(lo.astype(BF16), wup_bf[0:half, :])
                 + _dot(hi.astype(BF16), wup_bf[half:D_MODEL, :]) + bup_ref[0])
            ys_ref[rows, :] = _pack_halves(_dot(_gelu_tanh(h).astype(BF16), wdn_bf[...]))


def _expert_call(xs, tile_expert, n_tiles, lw, *, rt):
    layer = lw["layer"]
    rows = xs.shape[0]

    def row_map(i, te, nt):
        return (jnp.minimum(i, nt[0] - 1), 0)

    def w_map(i, te, nt):
        return (layer, te[i], 0, 0)

    return pl.pallas_call(
        _expert_kernel,
        grid_spec=pltpu.PrefetchScalarGridSpec(
            num_scalar_prefetch=2,
            grid=(rows // rt,),
            in_specs=[
                pl.BlockSpec((rt, D_MODEL // 2), row_map),
                pl.BlockSpec((1, 1, D_MODEL, D_EXPERT), w_map),
                pl.BlockSpec((1, 1, D_EXPERT), lambda i, te, nt: (te[i], 0, 0)),
                pl.BlockSpec((1, 1, D_EXPERT, D_MODEL), w_map),
            ],
            out_specs=pl.BlockSpec((rt, D_MODEL // 2), row_map),
            scratch_shapes=[pltpu.VMEM((D_MODEL, D_EXPERT), BF16),
                            pltpu.VMEM((D_EXPERT, D_MODEL), BF16)],
        ),
        out_shape=jax.ShapeDtypeStruct((rows, D_MODEL // 2), jnp.int32),
        compiler_params=pltpu.CompilerParams(
            dimension_semantics=("arbitrary",), vmem_limit_bytes=VMEM_LIMIT_BYTES),
        name="experts",
    )(tile_expert, n_tiles, xs, lw["w_up"], lw["b_up"], lw["w_down"])


def _moe_combine(x1, yg_ref, meta, g, b, alpha):
    g1 = meta[:, 2:3]
    g2 = meta[:, 3:4]
    lo1, hi1 = _unpack_halves(yg_ref[0])
    lo2, hi2 = _unpack_halves(yg_ref[1])
    moe = jnp.concatenate([g1 * lo1 + g2 * lo2, g1 * hi1 + g2 * hi2], axis=1)
    return _layer_norm(alpha * x1 + moe, g, b)


def _combine_kernel(x_ref, yg_ref, meta_ref, g2_ref, b2_ref, o_ref, yslab, *, alpha):
    y = _moe_combine(x_ref[...], yg_ref, meta_ref[...], g2_ref[...], b2_ref[...], alpha)
    nslab = D_MODEL // LANES
    steps = y.shape[0] // SEQS_PER_STREAM
    for j in range(nslab):
        yslab[j] = y[:, j * LANES:(j + 1) * LANES]
    for s in range(SEQS_PER_STREAM):
        for j in range(nslab):
            o_ref[s, :, j * LANES:(j + 1) * LANES] = yslab[j, pl.ds(s, steps, stride=SEQS_PER_STREAM), :]


def _combine_call(x1, yg, meta, lw, *, alpha):
    nq, srows, _ = x1.shape
    tmc = min(1024, srows)
    tps = srows // tmc
    steps = tmc // SEQS_PER_STREAM
    return pl.pallas_call(
        functools.partial(_combine_kernel, alpha=alpha),
        grid=(nq * tps,),
        in_specs=[
            pl.BlockSpec((tmc, D_MODEL), lambda i: (i, 0)),
            pl.BlockSpec((2, tmc, D_MODEL // 2), lambda i: (0, i, 0)),
            pl.BlockSpec((tmc, META_COLS), lambda i: (i, 0)),
            pl.BlockSpec((1, D_MODEL), lambda i: (0, 0)),
            pl.BlockSpec((1, D_MODEL), lambda i: (0, 0)),
        ],
        out_specs=pl.BlockSpec((SEQS_PER_STREAM, steps, D_MODEL), lambda i: (i // tps, i % tps, 0)),
        out_shape=jax.ShapeDtypeStruct((nq * SEQS_PER_STREAM, srows // SEQS_PER_STREAM, D_MODEL), F32),
        scratch_shapes=[pltpu.VMEM((D_MODEL // LANES, tmc, LANES), F32)],
        compiler_params=pltpu.CompilerParams(
            dimension_semantics=("arbitrary",), vmem_limit_bytes=VMEM_LIMIT_BYTES),
        name="combine",
    )(x1.reshape(nq * srows, D_MODEL), yg, meta, lw["g2"], lw["b2"])


def _sc_mesh():
    return plsc.VectorSubcoreMesh(core_axis_name="c", subcore_axis_name="s")


def _sc_chunk(rows):
    per_worker = rows // SC_WORKERS
    chunk = min(SC_MAX_CHUNK, per_worker)
    assert per_worker % chunk == 0 and chunk % 8 == 0
    return per_worker // chunk, chunk


def _sc_dispatch(xp, pos, n_slots):
    rows, width = xp.shape
    nch, ch = _sc_chunk(rows)
    posr = pos.reshape(2, SC_WORKERS, nch, ch)

    @functools.partial(
        pl.kernel, mesh=_sc_mesh(),
        out_type=jax.ShapeDtypeStruct((n_slots, width), jnp.int32),
        scratch_types=[pltpu.VMEM((ch,), jnp.int32), pltpu.VMEM((ch,), jnp.int32),
                       pltpu.VMEM((ch, width), jnp.int32), pltpu.SemaphoreType.DMA],
        name="sc_dispatch")
    def k(x_hbm, pos_hbm, o_hbm, idx0, idx1, buf, sem):
        wid = lax.axis_index("s") * SC_CORES + lax.axis_index("c")

        @pl.loop(0, nch)
        def _(c):
            base = (wid * nch + c) * ch
            pltpu.sync_copy(x_hbm.at[pl.ds(base, ch)], buf)
            pltpu.sync_copy(pos_hbm.at[0, wid, c], idx0)
            pltpu.sync_copy(pos_hbm.at[1, wid, c], idx1)
            pltpu.async_copy(buf, o_hbm.at[idx0], sem).wait()
            pltpu.async_copy(buf, o_hbm.at[idx1], sem).wait()

    return k(xp, posr)


def _sc_gather(ys, pos):
    width = ys.shape[1]
    rows = pos.shape[1]
    nch, ch = _sc_chunk(rows)
    posr = pos.reshape(2, SC_WORKERS, nch, ch)

    @functools.partial(
        pl.kernel, mesh=_sc_mesh(),
        out_type=jax.ShapeDtypeStruct((2, rows, width), jnp.int32),
        scratch_types=[pltpu.VMEM((ch,), jnp.int32), pltpu.VMEM((ch, width), jnp.int32),
                       pltpu.SemaphoreType.DMA],
        name="sc_gather")
    def k(y_hbm, pos_hbm, o_hbm, idx, buf, sem):
        wid = lax.axis_index("s") * SC_CORES + lax.axis_index("c")

        @pl.loop(0, nch)
        def _(c):
            base = (wid * nch + c) * ch
            for kk in range(2):
                pltpu.sync_copy(pos_hbm.at[kk, wid, c], idx)
                pltpu.async_copy(y_hbm.at[idx], buf, sem).wait()
                pltpu.sync_copy(buf, o_hbm.at[kk, pl.ds(base, ch)])

    return k(ys, posr)


def _moe_rows(xp, meta_t, counts, lw):
    rows = xp.shape[0]
    rt = 512 if rows >= 8192 else 128
    n_slots = 2 * rows + N_EXPERTS * rt
    nt_max = n_slots // rt

    cnt = counts[:, 0].astype(jnp.int32)
    padded = ((cnt + rt - 1) // rt) * rt
    ends = jnp.cumsum(padded)
    offs = ends - padded
    meta_t = meta_t.transpose(1, 0, 2).reshape(META_COLS, rows)
    eidx = meta_t[0:2].astype(jnp.int32)
    rank = meta_t[4:6].astype(jnp.int32)
    experts = jnp.arange(N_EXPERTS, dtype=jnp.int32)[:, None]
    pos = jnp.sum(jnp.where(eidx[:, None, :] == experts, offs[:, None], 0), axis=1) + rank
    n_tiles = ends[-1:] // rt
    tiles = jnp.minimum(jnp.arange(nt_max, dtype=jnp.int32), n_tiles[0] - 1)
    tile_expert = jnp.minimum(jnp.sum((ends // rt)[None, :] <= tiles[:, None], axis=1),
                              N_EXPERTS - 1).astype(jnp.int32)

    xs = _sc_dispatch(xp, pos, n_slots)
    ys = _expert_call(xs, tile_expert, n_tiles.astype(jnp.int32), lw, rt=rt)
    return _sc_gather(ys, pos)


def _memkv_kernel(m_ref, wk_ref, wv_ref, k_ref, v_ref):
    mb = m_ref[...].astype(BF16)
    k_ref[0] = _dot(mb, wk_ref[0])
    v_ref[0] = _dot(mb, wv_ref[0])


def _memkv_call(mem, wk, wv):
    depth = wk.shape[0]
    rows = mem.shape[0]
    return pl.pallas_call(
        _memkv_kernel,
        grid=(depth,),
        in_specs=[
            pl.BlockSpec((rows, D_MODEL), lambda l: (0, 0)),
            pl.BlockSpec((1, D_MODEL, D_ATT), lambda l: (l, 0, 0)),
            pl.BlockSpec((1, D_MODEL, D_ATT), lambda l: (l, 0, 0)),
        ],
        out_specs=[
            pl.BlockSpec((1, rows, D_ATT), lambda l: (l, 0, 0)),
            pl.BlockSpec((1, rows, D_ATT), lambda l: (l, 0, 0)),
        ],
        out_shape=[jax.ShapeDtypeStruct((depth, rows, D_ATT), F32)] * 2,
        compiler_params=pltpu.CompilerParams(
            dimension_semantics=("arbitrary",), vmem_limit_bytes=VMEM_LIMIT_BYTES),
        name="memkv",
    )(mem, wk, wv)


def _perm_matrices(lc):
    tm = lc * SEQS_PER_STREAM
    p = np.zeros((tm, tm), np.float32)
    for s in range(SEQS_PER_STREAM):
        for t in range(lc):
            p[s * lc + t, t * SEQS_PER_STREAM + s] = 1.0
    return jnp.asarray(p, BF16), jnp.asarray(p.T, BF16)


def _to_streams(a):
    b, l, c = a.shape
    q = b // SEQS_PER_STREAM
    return a.reshape(q, SEQS_PER_STREAM, l, c).transpose(0, 2, 1, 3).reshape(q, l * SEQS_PER_STREAM, c)


def _from_streams(a, l):
    q, _, c = a.shape
    return a.reshape(q, l, SEQS_PER_STREAM, c).transpose(0, 2, 1, 3).reshape(q * SEQS_PER_STREAM, l, c)


def _pack_state(re, im):
    b = re.shape[0]
    q = b // SEQS_PER_STREAM
    re = re.reshape(q, SEQS_PER_STREAM, D_STATE)
    im = im.reshape(q, SEQS_PER_STREAM, D_STATE)
    return jnp.concatenate([im, re], axis=1)


def _unpack_state(h):
    q = h.shape[0]
    im = h[:, 0:SEQS_PER_STREAM].reshape(q * SEQS_PER_STREAM, N_SSM_GROUPS, SSM_STATE)
    re = h[:, SEQS_PER_STREAM:].reshape(q * SEQS_PER_STREAM, N_SSM_GROUPS, SSM_STATE)
    return re, im


def _pad_heads(mk, mv):
    b = mk.shape[0]
    q = b // SEQS_PER_STREAM
    eye = jnp.eye(N_MEM_HEADS, dtype=mk.dtype)
    kt = mk.transpose(0, 2, 3, 1)
    kp = jnp.einsum("bhdm,hg->bhgdm", kt, eye).reshape(b, N_MEM_HEADS, D_ATT, N_MEM)
    vt = mv.transpose(0, 2, 1, 3)
    vp = jnp.einsum("bhmd,hg->bhmgd", vt, eye).reshape(b, N_MEM_HEADS, N_MEM, D_ATT)
    kp = kp.reshape(q, SEQS_PER_STREAM, N_MEM_HEADS, D_ATT, N_MEM).astype(BF16)
    vp = vp.reshape(q, SEQS_PER_STREAM, N_MEM_HEADS, N_MEM, D_ATT).astype(BF16)
    return kp, vp


def _layer_params(l, w_in, w_dw, b_dw, conv_ln_g, conv_ln_b, ssm_a_re, ssm_a_im, ssm_b_re,
                  ssm_b_im, ssm_c_re, ssm_c_im, ssm_d, ssm_log_dt, ssm_w_glu, ssm_b_glu,
                  w_out, ln1_g, ln1_b, w_up, b_up, w_down, ln2_g, ln2_b):
    g, p = N_SSM_GROUPS, SSM_STATE
    a = lax.complex(ssm_a_re[l], ssm_a_im[l])
    dt = jnp.exp(ssm_log_dt[l])[:, None]
    a_bar = jnp.exp(a * dt)
    b_bar = ((a_bar - 1.0) / a)[..., None] * lax.complex(ssm_b_re[l], ssm_b_im[l])
    eye = jnp.eye(g, dtype=F32)

    def b_block(m):
        full = jnp.einsum("gpi,gh->gihp", m, eye).reshape(D_SSM, D_STATE).astype(BF16)
        return jnp.stack([full[k * LANES:(k + 1) * LANES, k * SCAN_LANES:(k + 1) * SCAN_LANES]
                          for k in range(SSM_BLOCKS)])

    def c_block(m):
        full = jnp.einsum("gip,gh->gphi", m, eye).reshape(D_STATE, D_SSM).astype(BF16)
        return jnp.stack([full[k * SCAN_LANES:(k + 1) * SCAN_LANES, k * LANES:(k + 1) * LANES]
                          for k in range(SSM_BLOCKS)])

    ar = jnp.real(a_bar).reshape(1, D_STATE)
    ai = jnp.imag(a_bar).reshape(1, D_STATE)
    half = SUBLANES // 2
    return {
        "w_in": w_in[l].astype(BF16),
        "wdw": jnp.repeat(w_dw[l], SUBLANES, axis=0),
        "bdw": b_dw[l][None], "clg": conv_ln_g[l][None], "clb": conv_ln_b[l][None],
        "a1": jnp.broadcast_to(ar, (SUBLANES, D_STATE)),
        "a2": jnp.concatenate([jnp.broadcast_to(-ai, (half, D_STATE)),
                               jnp.broadcast_to(ai, (half, D_STATE))], axis=0),
        "bre": b_block(jnp.real(b_bar)), "bim": b_block(jnp.imag(b_bar)),
        "cre": c_block(ssm_c_re[l]), "cim": c_block(-ssm_c_im[l]),
        "d": ssm_d[l][None], "wglu": ssm_w_glu[l].astype(BF16), "bglu": ssm_b_glu[l][None],
        "w_out": w_out[l].astype(BF16), "g1": ln1_g[l][None], "b1": ln1_b[l][None],
        "w_up": w_up, "b_up": b_up[l][:, None, :], "w_down": w_down, "layer": l,
        "g2": ln2_g[l][None], "b2": ln2_b[l][None],
    }


def kernel(x_prompt, x_sample, cache_conv, state_ssm_re, state_ssm_im, cache_mem_k, cache_mem_v, mem_prompt, w_in, w_dw, b_dw, conv_ln_g, conv_ln_b, ssm_a_re, ssm_a_im, ssm_b_re, ssm_b_im, ssm_c_re, ssm_c_im, ssm_d, ssm_log_dt, ssm_w_glu, ssm_b_glu, w_mem_k, w_mem_v, w_out, ln1_g, ln1_b, w_router, b_router, w_up, b_up, w_down, ln2_g, ln2_b):
    depth = w_in.shape[0]
    alpha = (2.0 * depth) ** 0.25
    bp, seq, _ = x_prompt.shape
    bs, dec_seq, _ = x_sample.shape
    assert bp == SEQS_PER_STREAM and bs % SEQS_PER_STREAM == 0
    lc_p = min(128, seq)
    lc_s = dec_seq
    assert seq % lc_p == 0 and lc_p % 16 == 0 and lc_s % 16 == 0

    mb, mm, _ = mem_prompt.shape
    mk_all, mv_all = _memkv_call(mem_prompt.reshape(mb * mm, D_MODEL),
                                 w_mem_k.astype(BF16), w_mem_v.astype(BF16))
    mk_all = mk_all.reshape(depth, mb, mm, N_MEM_HEADS, MEM_HEAD_DIM)
    mv_all = mv_all.reshape(depth, mb, mm, N_MEM_HEADS, MEM_HEAD_DIM)

    perm_p, permt_p = _perm_matrices(lc_p)
    perm_s, permt_s = _perm_matrices(lc_s)
    wr = w_router.T.astype(BF16)
    br = b_router[:, None]

    xp = x_prompt
    xs = x_sample
    zero_hist = jnp.zeros((1, HIST_ROWS, D_CONV), F32)
    zero_h = jnp.zeros((1, SUBLANES, D_STATE), F32)

    conv_p, re_p, im_p, conv_s, re_s, im_s = [], [], [], [], [], []
    moe_p = moe_s = None
    for l in range(depth):
        lw = _layer_params(l, w_in, w_dw, b_dw, conv_ln_g, conv_ln_b, ssm_a_re, ssm_a_im,
                           ssm_b_re, ssm_b_im, ssm_c_re, ssm_c_im, ssm_d, ssm_log_dt, ssm_w_glu,
                           ssm_b_glu, w_out, ln1_g, ln1_b, w_up, b_up, w_down, ln2_g, ln2_b)
        kp, vp = _pad_heads(mk_all[l], mv_all[l])
        xp, hist, hst, rows_p, meta_p, metat_p, cnt_p = _mixer_call(
            xp, moe_p, zero_hist, zero_h, kp, vp, perm_p, permt_p, lw, wr, br, lc=lc_p, alpha=alpha)
        moe_p = (_moe_rows(rows_p, metat_p, cnt_p, lw), meta_p, lw["g2"], lw["b2"])
        conv_p.append(_from_streams(hist, CONV_BUF))
        hr, hi = _unpack_state(hst)
        re_p.append(hr)
        im_p.append(hi)
        kp, vp = _pad_heads(cache_mem_k[l], cache_mem_v[l])
        xs, hist, hst, rows_s, meta_s, metat_s, cnt_s = _mixer_call(
            xs, moe_s, _to_streams(cache_conv[l]), _pack_state(state_ssm_re[l], state_ssm_im[l]),
            kp, vp, perm_s, permt_s, lw, wr, br, lc=lc_s, alpha=alpha)
        moe_s = (_moe_rows(rows_s, metat_s, cnt_s, lw), meta_s, lw["g2"], lw["b2"])
        conv_s.append(_from_streams(hist, CONV_BUF))
        hr, hi = _unpack_state(hst)
        re_s.append(hr)
        im_s.append(hi)

    y_prompt = _combine_call(xp, moe_p[0], moe_p[1], lw, alpha=alpha)
    y_sample = _combine_call(xs, moe_s[0], moe_s[1], lw, alpha=alpha)

    return (y_prompt, y_sample,
            jnp.stack(conv_p), jnp.stack(re_p), jnp.stack(im_p), mk_all, mv_all,
            jnp.stack(conv_s), jnp.stack(re_s), jnp.stack(im_s))
```

```python
import functools
import math

import numpy as np
import jax
import jax.numpy as jnp
from jax import lax
from jax.experimental import pallas as pl
from jax.experimental.pallas import tpu as pltpu
from jax.experimental.pallas import tpu_sc as plsc

F32 = jnp.float32
BF16 = jnp.bfloat16

D_MODEL = 1024
CONV_WIDTH = 31
CONV_BUF = CONV_WIDTH - 1
D_CONV = 384
D_SSM = 384
SSM_GROUP = 16
N_SSM_GROUPS = D_SSM // SSM_GROUP
SSM_STATE = 64
D_STATE = N_SSM_GROUPS * SSM_STATE
N_MEM = 256
N_MEM_HEADS = 4
MEM_HEAD_DIM = 64
D_ATT = N_MEM_HEADS * MEM_HEAD_DIM
D_IN = 2 * D_CONV + D_SSM + D_ATT
N_EXPERTS = 16
N_EXPERT_GROUPS = 4
EXPERTS_PER_GROUP = N_EXPERTS // N_EXPERT_GROUPS
D_EXPERT = 512
LN_EPS = 1e-5

SEQS_PER_STREAM = 4
HIST_ROWS = CONV_BUF * SEQS_PER_STREAM
SUBLANES = 8
LANES = 128
SCAN_LANES = 512
SSM_BLOCKS = D_STATE // SCAN_LANES
CONV_ROWS = 32
EXPERT_SUB_ROWS = 512
VMEM_LIMIT_BYTES = 56 * 1024 * 1024
HI_HALF_MASK = np.int32(-65536)
META_COLS = 8
SC_CORES = 2
SC_WORKERS = SC_CORES * 16
SC_MAX_CHUNK = 128


def _sigmoid(x):
    return 1.0 / (1.0 + jnp.exp(-x))


def _gelu_tanh(x):
    c = math.sqrt(2.0 / math.pi)
    return 0.5 * x * (1.0 + jnp.tanh(c * (x + 0.044715 * (x * x * x))))


def _layer_norm(z, g, b):
    mu = jnp.mean(z, axis=-1, keepdims=True)
    zc = z - mu
    var = jnp.mean(zc * zc, axis=-1, keepdims=True)
    return zc * lax.rsqrt(var + LN_EPS) * g + b


def _dot(a, b):
    return jnp.dot(a, b, preferred_element_type=F32)


def _mixer_kernel(*refs, tm, lc, alpha, fuse_in):
    refs = list(refs)
    if fuse_in:
        xprev_ref, yg_ref, metain_ref, g2p_ref, b2p_ref = refs[:5]
        refs = refs[5:]
    else:
        xprev_ref = refs.pop(0)
        xslab = refs.pop()
    (hist0_ref, h0_ref, k_ref, v_ref, perm_ref, permt_ref,
     w_in_ref, wdw_ref, bdw_ref, clg_ref, clb_ref, a1_ref, a2_ref,
     bre_ref, bim_ref, cre_ref, cim_ref, d_ref, wglu_ref, bglu_ref,
     wout_ref, g1_ref, b1_ref, wr_ref, br_ref, tri_ref, cnt0_ref,
     x1_ref, hist_out_ref, h_out_ref, xp_ref, meta_ref, metat_ref, cnt_ref,
     xpad0, xpad4, cy, ush, yim, bu_re, bu_im, hre, him, hcar, xin, running) = refs
    i = pl.program_id(1)

    @pl.when(i == 0)
    def _():
        xpad0[0:HIST_ROWS, :] = hist0_ref[0]
        hcar[...] = h0_ref[0]
        ush[...] = jnp.zeros_like(ush)

    @pl.when((i == 0) & (pl.program_id(0) == 0))
    def _():
        running[...] = cnt0_ref[...]

    if fuse_in:
        xin[...] = _moe_combine(xprev_ref[0], yg_ref, metain_ref[...], g2p_ref[...], b2p_ref[...], alpha)
    else:
        nslab = D_MODEL // LANES
        for s in range(SEQS_PER_STREAM):
            for j in range(nslab):
                xslab[j, pl.ds(s, lc, stride=SEQS_PER_STREAM), :] = xprev_ref[s, :, j * LANES:(j + 1) * LANES]
        xin[...] = jnp.concatenate([xslab[j] for j in range(nslab)], axis=1)
    x = xin[...]
    proj = _dot(x.astype(BF16), w_in_ref[...])

    g = proj[:, 0:D_CONV] * _sigmoid(proj[:, D_CONV:2 * D_CONV])
    xpad0[HIST_ROWS:HIST_ROWS + tm, :] = g
    xpad4[0:HIST_ROWS + tm - 4, :] = xpad0[4:HIST_ROWS + tm, :]

    nsub = CONV_ROWS // SUBLANES

    def conv_rows(rb, carry):
        r0 = pl.multiple_of(rb * CONV_ROWS, CONV_ROWS)
        accs = [jnp.broadcast_to(bdw_ref[...], (SUBLANES, D_CONV)) for _ in range(nsub)]
        for k in range(CONV_WIDTH):
            wk = wdw_ref[SUBLANES * k:SUBLANES * (k + 1), :]
            for sb in range(nsub):
                off = r0 + SEQS_PER_STREAM * k + SUBLANES * sb
                if k % 2 == 0:
                    xs = xpad0[pl.ds(pl.multiple_of(off, SUBLANES), SUBLANES), :]
                else:
                    xs = xpad4[pl.ds(pl.multiple_of(off - 4, SUBLANES), SUBLANES), :]
                accs[sb] = accs[sb] + xs * wk
        for sb in range(nsub):
            cy[pl.ds(pl.multiple_of(r0 + SUBLANES * sb, SUBLANES), SUBLANES), :] = accs[sb]
        return carry

    lax.fori_loop(0, tm // CONV_ROWS, conv_rows, 0, unroll=True)
    conv_n = _layer_norm(cy[...], clg_ref[...], clb_ref[...])
    cy[...] = conv_n * _sigmoid(conv_n)

    new_hist = xpad0[tm:tm + HIST_ROWS, :]
    xpad0[0:HIST_ROWS, :] = new_hist
    hist_out_ref[0] = new_hist

    u = proj[:, 2 * D_CONV:2 * D_CONV + D_SSM]
    ush[4:tm + 4, :] = u
    ub = u.astype(BF16)
    ub_sh = ush[...].astype(BF16)
    for m in range(SSM_BLOCKS):
        ch = slice(m * LANES, (m + 1) * LANES)
        st = slice(m * SCAN_LANES, (m + 1) * SCAN_LANES)
        bu_re[:, st] = _dot(ub[:, ch], bre_ref[m])
        bu_im[:, st] = _dot(ub_sh[:, ch], bim_ref[m])

    lo = lax.broadcasted_iota(jnp.int32, (SUBLANES, SCAN_LANES), 0) < 4
    for c in range(D_STATE // SCAN_LANES):
        cs = slice(c * SCAN_LANES, (c + 1) * SCAN_LANES)
        a1 = a1_ref[:, cs]
        a2 = a2_ref[:, cs]

        def scan_pair(j, carry, cs=cs, a1=a1, a2=a2):
            h_prev, im_cur = carry
            r = pl.multiple_of(j * SUBLANES, SUBLANES)
            re_cur = bu_re[pl.ds(r, SUBLANES), cs]
            im_next = bu_im[pl.ds(r + SUBLANES, SUBLANES), cs]
            p_even = jnp.where(lo, re_cur, im_cur)
            p_odd = jnp.where(lo, im_next, re_cur)
            h_even = a1 * pltpu.roll(h_prev, 4, 0) + a2 * h_prev + p_even
            h_odd = a1 * pltpu.roll(h_even, 4, 0) - a2 * h_even + p_odd
            hre[pl.ds(r, SUBLANES), cs] = jnp.where(lo, h_even, h_odd)
            him[pl.ds(r, SUBLANES), cs] = jnp.where(lo, h_prev, h_even)
            return h_odd, im_next

        h_last, _ = lax.fori_loop(0, lc // 2, scan_pair,
                                  (hcar[:, cs], bu_im[0:SUBLANES, cs]), unroll=True)
        him[tm:tm + SUBLANES, cs] = jnp.where(lo, h_last, 0.0)
        hcar[:, cs] = h_last
    h_out_ref[0] = hcar[...]

    y_re_blocks = []
    for m in range(SSM_BLOCKS):
        ch = slice(m * LANES, (m + 1) * LANES)
        st = slice(m * SCAN_LANES, (m + 1) * SCAN_LANES)
        y_re_blocks.append(_dot(hre[:, st].astype(BF16), cre_ref[m]))
        yim[:, ch] = _dot(him[:, st].astype(BF16), cim_ref[m])
    y_re = jnp.concatenate(y_re_blocks, axis=1)
    y = y_re + yim[4:tm + 4, :] + d_ref[...] * u
    y = _gelu_tanh(y)
    ssm_y = y * _sigmoid(_dot(y.astype(BF16), wglu_ref[...]) + bglu_ref[...])

    q = proj[:, 2 * D_CONV + D_SSM:D_IN].astype(BF16)
    q_seq = _dot(perm_ref[...], q)
    head_of_col = lax.shift_right_logical(lax.broadcasted_iota(jnp.int32, (1, D_ATT), 1), 6)
    head_masks = [jnp.where(head_of_col == h, 1.0, 0.0) for h in range(N_MEM_HEADS)]
    outs = []
    for s in range(SEQS_PER_STREAM):
        qs = q_seq[s * lc:(s + 1) * lc, :]
        ks = k_ref[0, s]
        vs = v_ref[0, s]
        acc = jnp.zeros((lc, D_ATT), F32)
        for h in range(N_MEM_HEADS):
            qh = (qs * head_masks[h]).astype(BF16)
            sc = _dot(qh, ks) * (MEM_HEAD_DIM ** -0.5)
            sc = sc - jnp.max(sc, axis=-1, keepdims=True)
            e = jnp.exp(sc)
            p = e * (1.0 / jnp.sum(e, axis=-1, keepdims=True))
            acc = acc + _dot(p.astype(BF16), vs) * head_masks[h]
        outs.append(acc)
    att_seq = jnp.concatenate(outs, axis=0).astype(BF16)
    att = _dot(permt_ref[...], att_seq).astype(BF16)

    mix = _dot(jnp.concatenate([cy[...].astype(BF16), ssm_y.astype(BF16), att], axis=1), wout_ref[...])
    x1 = _layer_norm(alpha * xin[...] + mix, g1_ref[...], b1_ref[...])
    x1_ref[0] = x1
    _route_rows(x1, wr_ref, br_ref, tri_ref, xp_ref, meta_ref, metat_ref, cnt_ref, running)


def _mixer_call(x, prev_moe, hist0, h0, kpad, vpad, perm, permt, lw, wr, br, cnt0, *, lc, alpha):
    fuse_in = prev_moe is not None
    if fuse_in:
        nq, rows, _ = x.shape
    else:
        nq, rows = x.shape[0] // SEQS_PER_STREAM, x.shape[1] * SEQS_PER_STREAM
    tm = lc * SEQS_PER_STREAM
    nt = rows // tm
    kern = functools.partial(_mixer_kernel, tm=tm, lc=lc, alpha=alpha, fuse_in=fuse_in)
    triu = jnp.asarray(np.triu(np.ones((tm, tm), np.float32), 1), BF16)

    def const(shape):
        return pl.BlockSpec(shape, lambda q, i: (0,) * len(shape))

    def flat(shape):
        return pl.BlockSpec(shape, lambda q, i: (0,) * (len(shape) - 2) + (q * nt + i, 0))

    operands = [x]
    if fuse_in:
        in_specs = [pl.BlockSpec((1, tm, D_MODEL), lambda q, i: (q, i, 0)),
                    flat((2, tm, D_MODEL // 2)), flat((tm, META_COLS)),
                    const((1, D_MODEL)), const((1, D_MODEL))]
        operands += list(prev_moe)
    else:
        in_specs = [pl.BlockSpec((SEQS_PER_STREAM, lc, D_MODEL), lambda q, i: (q, i, 0))]
    in_specs += [
        pl.BlockSpec((1, HIST_ROWS, D_CONV), lambda q, i: (q, 0, 0)),
        pl.BlockSpec((1, SUBLANES, D_STATE), lambda q, i: (q, 0, 0)),
        pl.BlockSpec((1, SEQS_PER_STREAM, N_MEM, D_ATT), lambda q, i: (q, 0, 0, 0)),
        pl.BlockSpec((1, SEQS_PER_STREAM, N_MEM, D_ATT), lambda q, i: (q, 0, 0, 0)),
        const((tm, tm)), const((tm, tm)),
        const((D_MODEL, D_IN)),
        const((CONV_WIDTH * SUBLANES, D_CONV)), const((1, D_CONV)), const((1, D_CONV)), const((1, D_CONV)),
        const((SUBLANES, D_STATE)), const((SUBLANES, D_STATE)),
        const((SSM_BLOCKS, LANES, SCAN_LANES)), const((SSM_BLOCKS, LANES, SCAN_LANES)),
        const((SSM_BLOCKS, SCAN_LANES, LANES)), const((SSM_BLOCKS, SCAN_LANES, LANES)),
        const((1, D_SSM)), const((D_SSM, D_SSM)), const((1, D_SSM)),
        const((D_MODEL, D_MODEL)), const((1, D_MODEL)), const((1, D_MODEL)),
        const((N_EXPERTS, D_MODEL)), const((N_EXPERTS, 1)), const((tm, tm)), const((N_EXPERTS, 1)),
    ]
    out_specs = [
        pl.BlockSpec((1, tm, D_MODEL), lambda q, i: (q, i, 0)),
        pl.BlockSpec((1, HIST_ROWS, D_CONV), lambda q, i: (q, 0, 0)),
        pl.BlockSpec((1, SUBLANES, D_STATE), lambda q, i: (q, 0, 0)),
        flat((tm, D_MODEL // 2)), flat((tm, META_COLS)),
        pl.BlockSpec((1, META_COLS, tm), lambda q, i: (q * nt + i, 0, 0)), const((N_EXPERTS, 1)),
    ]
    out_shape = [
        jax.ShapeDtypeStruct((nq, rows, D_MODEL), F32),
        jax.ShapeDtypeStruct((nq, HIST_ROWS, D_CONV), F32),
        jax.ShapeDtypeStruct((nq, SUBLANES, D_STATE), F32),
        jax.ShapeDtypeStruct((nq * rows, D_MODEL // 2), jnp.int32),
        jax.ShapeDtypeStruct((nq * rows, META_COLS), F32),
        jax.ShapeDtypeStruct((nq * nt, META_COLS, tm), F32),
        jax.ShapeDtypeStruct((N_EXPERTS, 1), F32),
    ]
    scratch = [
        pltpu.VMEM((HIST_ROWS + tm + SUBLANES, D_CONV), F32),
        pltpu.VMEM((HIST_ROWS + tm + SUBLANES, D_CONV), F32),
        pltpu.VMEM((tm, D_CONV), F32),
        pltpu.VMEM((tm + SUBLANES, D_SSM), F32),
        pltpu.VMEM((tm + SUBLANES, D_SSM), F32),
        pltpu.VMEM((tm, D_STATE), F32),
        pltpu.VMEM((tm + SUBLANES, D_STATE), F32),
        pltpu.VMEM((tm, D_STATE), F32),
        pltpu.VMEM((tm + SUBLANES, D_STATE), F32),
        pltpu.VMEM((SUBLANES, D_STATE), F32),
        pltpu.VMEM((tm, D_MODEL), F32),
        pltpu.VMEM((N_EXPERTS, 1), F32),
    ]
    if not fuse_in:
        scratch.append(pltpu.VMEM((D_MODEL // LANES, tm, LANES), F32))
    return pl.pallas_call(
        kern,
        grid=(nq, nt),
        in_specs=in_specs,
        out_specs=out_specs,
        out_shape=out_shape,
        scratch_shapes=scratch,
        compiler_params=pltpu.CompilerParams(
            dimension_semantics=("arbitrary", "arbitrary"),
            vmem_limit_bytes=VMEM_LIMIT_BYTES),
        name="mixer",
    )(*operands, hist0, h0, kpad, vpad, perm, permt,
      lw["w_in"], lw["wdw"], lw["bdw"], lw["clg"], lw["clb"], lw["a1"], lw["a2"],
      lw["bre"], lw["bim"], lw["cre"], lw["cim"], lw["d"], lw["wglu"], lw["bglu"],
      lw["w_out"], lw["g1"], lw["b1"], wr, br, triu, cnt0)


def _route(logits_t):
    m = jnp.max(logits_t, axis=0, keepdims=True)
    e = jnp.exp(logits_t - m)
    aff = e / jnp.sum(e, axis=0, keepdims=True)
    rows = [aff[j:j + 1, :] for j in range(N_EXPERTS)]

    scores = []
    for gi in range(N_EXPERT_GROUPS):
        a, b, c, d = rows[EXPERTS_PER_GROUP * gi:EXPERTS_PER_GROUP * (gi + 1)]
        hi1, lo1 = jnp.maximum(a, b), jnp.minimum(a, b)
        hi2, lo2 = jnp.maximum(c, d), jnp.minimum(c, d)
        scores.append(jnp.maximum(hi1, hi2) + jnp.maximum(jnp.minimum(hi1, hi2), jnp.maximum(lo1, lo2)))
    best = scores[0]
    sel = jnp.zeros_like(best)
    for gi in range(1, N_EXPERT_GROUPS):
        better = scores[gi] > best
        sel = jnp.where(better, float(gi), sel)
        best = jnp.where(better, scores[gi], best)

    hot1, hot2 = [], []
    for gi in range(N_EXPERT_GROUPS):
        vals = rows[EXPERTS_PER_GROUP * gi:EXPERTS_PER_GROUP * (gi + 1)]
        chosen = sel == float(gi)
        for j in range(EXPERTS_PER_GROUP):
            ahead = jnp.zeros_like(best)
            for k in range(EXPERTS_PER_GROUP):
                if k < j:
                    ahead = ahead + jnp.where(vals[k] >= vals[j], 1.0, 0.0)
                elif k > j:
                    ahead = ahead + jnp.where(vals[k] > vals[j], 1.0, 0.0)
            hot1.append(jnp.where(chosen, jnp.where(ahead == 0.0, 1.0, 0.0), 0.0))
            hot2.append(jnp.where(chosen, jnp.where(ahead == 1.0, 1.0, 0.0), 0.0))
    return jnp.concatenate(hot1, axis=0), jnp.concatenate(hot2, axis=0), aff


def _pack_halves(y):
    half = y.shape[1] // 2
    lo = lax.bitcast_convert_type(y[:, :half].astype(BF16).astype(F32), jnp.int32)
    hi = lax.bitcast_convert_type(y[:, half:].astype(BF16).astype(F32), jnp.int32)
    return lax.shift_right_logical(lo, 16) | (hi & HI_HALF_MASK)


def _unpack_halves(w):
    lo = lax.bitcast_convert_type(lax.shift_left(w, 16), F32)
    hi = lax.bitcast_convert_type(w & HI_HALF_MASK, F32)
    return lo, hi


def _route_rows(x, wr_ref, br_ref, triu_ref, xp_ref, meta_ref, metat_ref, cnt_ref, running):
    tm = x.shape[0]
    xp_ref[...] = _pack_halves(x)
    logits_t = lax.dot_general(wr_ref[...], x.astype(BF16), (((1,), (1,)), ((), ())),
                               preferred_element_type=F32) + br_ref[...]
    hot1, hot2, aff = _route(logits_t)
    eid = lax.broadcasted_iota(jnp.int32, (N_EXPERTS, tm), 0).astype(F32)
    both = hot1 + hot2
    before = _dot(both.astype(BF16), triu_ref[...]) + running[...]

    def pick(hot, vals):
        return jnp.sum(hot * vals, axis=0, keepdims=True)

    v1 = pick(hot1, aff)
    v2 = pick(hot2, aff)
    denom = v1 + v2
    meta_t = jnp.concatenate(
        [pick(hot1, eid), pick(hot2, eid), v1 / denom, v2 / denom, pick(hot1, before), pick(hot2, before),
         jnp.zeros((LANES - 6, tm), F32)], axis=0)
    metat_ref[0] = meta_t[0:META_COLS, :]
    meta_ref[...] = meta_t.T[:, 0:META_COLS]
    running[...] = running[...] + jnp.sum(both, axis=1, keepdims=True)
    cnt_ref[...] = running[...]


def _expert_kernel(te_ref, nt_ref, xs_ref, wup_ref, bup_ref, wdn_ref, ys_ref, wup_bf, wdn_bf):
    i = pl.program_id(0)
    prev = te_ref[jnp.maximum(i - 1, 0)]

    @pl.when((i == 0) | (te_ref[i] != prev))
    def _():
        wup_bf[...] = wup_ref[0, 0].astype(BF16)
        wdn_bf[...] = wdn_ref[0, 0].astype(BF16)

    @pl.when(i < nt_ref[0])
    def _():
        half = D_MODEL // 2
        rt = xs_ref.shape[0]
        sub = min(EXPERT_SUB_ROWS, rt)
        for r in range(rt // sub):
            rows = slice(r * sub, (r + 1) * sub)
            lo, hi = _unpack_halves(xs_ref[rows, :])
            h = (_dot(lo.astype(BF16), wup_bf[0:half, :])
                 + _dot(hi.astype(BF16), wup_bf[half:D_MODEL, :]) + bup_ref[0])
            ys_ref[rows, :] = _pack_halves(_dot(_gelu_tanh(h).astype(BF16), wdn_bf[...]))


def _expert_call(xs, tile_expert, n_tiles, lw, *, rt):
    layer = lw["layer"]
    rows = xs.shape[0]

    def row_map(i, te, nt):
        return (jnp.minimum(i, nt[0] - 1), 0)

    def w_map(i, te, nt):
        return (layer, te[i], 0, 0)

    return pl.pallas_call(
        _expert_kernel,
        grid_spec=pltpu.PrefetchScalarGridSpec(
            num_scalar_prefetch=2,
            grid=(rows // rt,),
            in_specs=[
                pl.BlockSpec((rt, D_MODEL // 2), row_map),
                pl.BlockSpec((1, 1, D_MODEL, D_EXPERT), w_map),
                pl.BlockSpec((1, 1, D_EXPERT), lambda i, te, nt: (te[i], 0, 0)),
                pl.BlockSpec((1, 1, D_EXPERT, D_MODEL), w_map),
            ],
            out_specs=pl.BlockSpec((rt, D_MODEL // 2), row_map),
            scratch_shapes=[pltpu.VMEM((D_MODEL, D_EXPERT), BF16),
                            pltpu.VMEM((D_EXPERT, D_MODEL), BF16)],
        ),
        out_shape=jax.ShapeDtypeStruct((rows, D_MODEL // 2), jnp.int32),
        compiler_params=pltpu.CompilerParams(
            dimension_semantics=("arbitrary",), vmem_limit_bytes=VMEM_LIMIT_BYTES),
        name="experts",
    )(tile_expert, n_tiles, xs, lw["w_up"], lw["b_up"], lw["w_down"])


def _moe_combine(x1, yg_ref, meta, g, b, alpha):
    g1 = meta[:, 2:3]
    g2 = meta[:, 3:4]
    lo1, hi1 = _unpack_halves(yg_ref[0])
    lo2, hi2 = _unpack_halves(yg_ref[1])
    moe = jnp.concatenate([g1 * lo1 + g2 * lo2, g1 * hi1 + g2 * hi2], axis=1)
    return _layer_norm(alpha * x1 + moe, g, b)


def _combine_kernel(x_ref, yg_ref, meta_ref, g2_ref, b2_ref, o_ref, yslab, *, alpha):
    y = _moe_combine(x_ref[...], yg_ref, meta_ref[...], g2_ref[...], b2_ref[...], alpha)
    nslab = D_MODEL // LANES
    steps = y.shape[0] // SEQS_PER_STREAM
    for j in range(nslab):
        yslab[j] = y[:, j * LANES:(j + 1) * LANES]
    for s in range(SEQS_PER_STREAM):
        for j in range(nslab):
            o_ref[s, :, j * LANES:(j + 1) * LANES] = yslab[j, pl.ds(s, steps, stride=SEQS_PER_STREAM), :]


def _combine_call(x1, yg, meta, lw, *, alpha):
    nq, srows, _ = x1.shape
    tmc = min(1024, srows)
    tps = srows // tmc
    steps = tmc // SEQS_PER_STREAM
    return pl.pallas_call(
        functools.partial(_combine_kernel, alpha=alpha),
        grid=(nq * tps,),
        in_specs=[
            pl.BlockSpec((tmc, D_MODEL), lambda i: (i, 0)),
            pl.BlockSpec((2, tmc, D_MODEL // 2), lambda i: (0, i, 0)),
            pl.BlockSpec((tmc, META_COLS), lambda i: (i, 0)),
            pl.BlockSpec((1, D_MODEL), lambda i: (0, 0)),
            pl.BlockSpec((1, D_MODEL), lambda i: (0, 0)),
        ],
        out_specs=pl.BlockSpec((SEQS_PER_STREAM, steps, D_MODEL), lambda i: (i // tps, i % tps, 0)),
        out_shape=jax.ShapeDtypeStruct((nq * SEQS_PER_STREAM, srows // SEQS_PER_STREAM, D_MODEL), F32),
        scratch_shapes=[pltpu.VMEM((D_MODEL // LANES, tmc, LANES), F32)],
        compiler_params=pltpu.CompilerParams(
            dimension_semantics=("arbitrary",), vmem_limit_bytes=VMEM_LIMIT_BYTES),
        name="combine",
    )(x1.reshape(nq * srows, D_MODEL), yg, meta, lw["g2"], lw["b2"])


def _sc_mesh():
    return plsc.VectorSubcoreMesh(core_axis_name="c", subcore_axis_name="s")


def _sc_chunk(rows):
    per_worker = rows // SC_WORKERS
    chunk = min(SC_MAX_CHUNK, per_worker)
    assert per_worker % chunk == 0 and chunk % 8 == 0
    return per_worker // chunk, chunk


def _sc_dispatch(xps, poss, n_slots):
    width = xps[0].shape[1]
    ngroups = len(xps)
    plans = [_sc_chunk(x.shape[0]) for x in xps]
    posrs = [p.reshape(2, SC_WORKERS, nch, ch) for p, (nch, ch) in zip(poss, plans)]
    scratch = []
    for _, ch in plans:
        scratch += [pltpu.VMEM((ch,), jnp.int32), pltpu.VMEM((ch,), jnp.int32),
                    pltpu.VMEM((ch, width), jnp.int32)]

    @functools.partial(
        pl.kernel, mesh=_sc_mesh(),
        out_type=jax.ShapeDtypeStruct((n_slots, width), jnp.int32),
        scratch_types=scratch + [pltpu.SemaphoreType.DMA],
        name="sc_dispatch")
    def k(*refs):
        x_hbms, pos_hbms, o_hbm = refs[:ngroups], refs[ngroups:2 * ngroups], refs[2 * ngroups]
        bufs, sem = refs[2 * ngroups + 1:-1], refs[-1]
        wid = lax.axis_index("s") * SC_CORES + lax.axis_index("c")
        for g, (nch, ch) in enumerate(plans):
            idx0, idx1, buf = bufs[3 * g:3 * g + 3]

            @pl.loop(0, nch)
            def _(c, g=g, nch=nch, ch=ch, idx0=idx0, idx1=idx1, buf=buf):
                base = (wid * nch + c) * ch
                pltpu.sync_copy(x_hbms[g].at[pl.ds(base, ch)], buf)
                pltpu.sync_copy(pos_hbms[g].at[0, wid, c], idx0)
                pltpu.sync_copy(pos_hbms[g].at[1, wid, c], idx1)
                pltpu.async_copy(buf, o_hbm.at[idx0], sem).wait()
                pltpu.async_copy(buf, o_hbm.at[idx1], sem).wait()

    return k(*xps, *posrs)


def _sc_gather(ys, poss):
    width = ys.shape[1]
    ngroups = len(poss)
    plans = [_sc_chunk(p.shape[1]) for p in poss]
    posrs = [p.reshape(2, SC_WORKERS, nch, ch) for p, (nch, ch) in zip(poss, plans)]
    scratch = []
    for _, ch in plans:
        scratch += [pltpu.VMEM((ch,), jnp.int32), pltpu.VMEM((ch, width), jnp.int32)]

    @functools.partial(
        pl.kernel, mesh=_sc_mesh(),
        out_type=[jax.ShapeDtypeStruct((2, p.shape[1], width), jnp.int32) for p in poss],
        scratch_types=scratch + [pltpu.SemaphoreType.DMA],
        name="sc_gather")
    def k(*refs):
        y_hbm, pos_hbms = refs[0], refs[1:1 + ngroups]
        o_hbms = refs[1 + ngroups:1 + 2 * ngroups]
        bufs, sem = refs[1 + 2 * ngroups:-1], refs[-1]
        wid = lax.axis_index("s") * SC_CORES + lax.axis_index("c")
        for g, (nch, ch) in enumerate(plans):
            idx, buf = bufs[2 * g:2 * g + 2]

            @pl.loop(0, nch)
            def _(c, g=g, nch=nch, ch=ch, idx=idx, buf=buf):
                base = (wid * nch + c) * ch
                for kk in range(2):
                    pltpu.sync_copy(pos_hbms[g].at[kk, wid, c], idx)
                    pltpu.async_copy(y_hbm.at[idx], buf, sem).wait()
                    pltpu.sync_copy(buf, o_hbms[g].at[kk, pl.ds(base, ch)])

    return k(ys, *posrs)


def _moe_rows(xps, meta_ts, counts, lw):
    total = sum(x.shape[0] for x in xps)
    rt = 512 if total >= 8192 else 128
    n_slots = 2 * total + N_EXPERTS * rt
    nt_max = n_slots // rt

    cnt = counts[:, 0].astype(jnp.int32)
    padded = ((cnt + rt - 1) // rt) * rt
    ends = jnp.cumsum(padded)
    offs = ends - padded
    experts = jnp.arange(N_EXPERTS, dtype=jnp.int32)[:, None]
    poss = []
    for x, meta_t in zip(xps, meta_ts):
        meta_t = meta_t.transpose(1, 0, 2).reshape(META_COLS, x.shape[0])
        eidx = meta_t[0:2].astype(jnp.int32)
        rank = meta_t[4:6].astype(jnp.int32)
        poss.append(jnp.sum(jnp.where(eidx[:, None, :] == experts, offs[:, None], 0), axis=1) + rank)
    n_tiles = ends[-1:] // rt
    tiles = jnp.minimum(jnp.arange(nt_max, dtype=jnp.int32), n_tiles[0] - 1)
    tile_expert = jnp.minimum(jnp.sum((ends // rt)[None, :] <= tiles[:, None], axis=1),
                              N_EXPERTS - 1).astype(jnp.int32)

    xs = _sc_dispatch(xps, poss, n_slots)
    ys = _expert_call(xs, tile_expert, n_tiles.astype(jnp.int32), lw, rt=rt)
    return _sc_gather(ys, poss)


def _memkv_kernel(m_ref, wk_ref, wv_ref, k_ref, v_ref):
    mb = m_ref[...].astype(BF16)
    k_ref[0] = _dot(mb, wk_ref[0])
    v_ref[0] = _dot(mb, wv_ref[0])


def _memkv_call(mem, wk, wv):
    depth = wk.shape[0]
    rows = mem.shape[0]
    return pl.pallas_call(
        _memkv_kernel,
        grid=(depth,),
        in_specs=[
            pl.BlockSpec((rows, D_MODEL), lambda l: (0, 0)),
            pl.BlockSpec((1, D_MODEL, D_ATT), lambda l: (l, 0, 0)),
            pl.BlockSpec((1, D_MODEL, D_ATT), lambda l: (l, 0, 0)),
        ],
        out_specs=[
            pl.BlockSpec((1, rows, D_ATT), lambda l: (l, 0, 0)),
            pl.BlockSpec((1, rows, D_ATT), lambda l: (l, 0, 0)),
        ],
        out_shape=[jax.ShapeDtypeStruct((depth, rows, D_ATT), F32)] * 2,
        compiler_params=pltpu.CompilerParams(
            dimension_semantics=("arbitrary",), vmem_limit_bytes=VMEM_LIMIT_BYTES),
        name="memkv",
    )(mem, wk, wv)


def _perm_matrices(lc):
    tm = lc * SEQS_PER_STREAM
    p = np.zeros((tm, tm), np.float32)
    for s in range(SEQS_PER_STREAM):
        for t in range(lc):
            p[s * lc + t, t * SEQS_PER_STREAM + s] = 1.0
    return jnp.asarray(p, BF16), jnp.asarray(p.T, BF16)


def _to_streams(a):
    b, l, c = a.shape
    q = b // SEQS_PER_STREAM
    return a.reshape(q, SEQS_PER_STREAM, l, c).transpose(0, 2, 1, 3).reshape(q, l * SEQS_PER_STREAM, c)


def _from_streams(a, l):
    q, _, c = a.shape
    return a.reshape(q, l, SEQS_PER_STREAM, c).transpose(0, 2, 1, 3).reshape(q * SEQS_PER_STREAM, l, c)


def _pack_state(re, im):
    b = re.shape[0]
    q = b // SEQS_PER_STREAM
    re = re.reshape(q, SEQS_PER_STREAM, D_STATE)
    im = im.reshape(q, SEQS_PER_STREAM, D_STATE)
    return jnp.concatenate([im, re], axis=1)


def _unpack_state(h):
    q = h.shape[0]
    im = h[:, 0:SEQS_PER_STREAM].reshape(q * SEQS_PER_STREAM, N_SSM_GROUPS, SSM_STATE)
    re = h[:, SEQS_PER_STREAM:].reshape(q * SEQS_PER_STREAM, N_SSM_GROUPS, SSM_STATE)
    return re, im


def _pad_heads(mk, mv):
    b = mk.shape[0]
    q = b // SEQS_PER_STREAM
    shape = (q, SEQS_PER_STREAM, N_MEM, D_ATT)
    return mk.reshape(shape).transpose(0, 1, 3, 2).astype(BF16), mv.reshape(shape).astype(BF16)


def _layer_params(l, w_in, w_dw, b_dw, conv_ln_g, conv_ln_b, ssm_a_re, ssm_a_im, ssm_b_re,
                  ssm_b_im, ssm_c_re, ssm_c_im, ssm_d, ssm_log_dt, ssm_w_glu, ssm_b_glu,
                  w_out, ln1_g, ln1_b, w_up, b_up, w_down, ln2_g, ln2_b):
    g, p = N_SSM_GROUPS, SSM_STATE
    a = lax.complex(ssm_a_re[l], ssm_a_im[l])
    dt = jnp.exp(ssm_log_dt[l])[:, None]
    a_bar = jnp.exp(a * dt)
    b_bar = ((a_bar - 1.0) / a)[..., None] * lax.complex(ssm_b_re[l], ssm_b_im[l])
    eye = jnp.eye(g, dtype=F32)

    def b_block(m):
        full = jnp.einsum("gpi,gh->gihp", m, eye).reshape(D_SSM, D_STATE).astype(BF16)
        return jnp.stack([full[k * LANES:(k + 1) * LANES, k * SCAN_LANES:(k + 1) * SCAN_LANES]
                          for k in range(SSM_BLOCKS)])

    def c_block(m):
        full = jnp.einsum("gip,gh->gphi", m, eye).reshape(D_STATE, D_SSM).astype(BF16)
        return jnp.stack([full[k * SCAN_LANES:(k + 1) * SCAN_LANES, k * LANES:(k + 1) * LANES]
                          for k in range(SSM_BLOCKS)])

    ar = jnp.real(a_bar).reshape(1, D_STATE)
    ai = jnp.imag(a_bar).reshape(1, D_STATE)
    half = SUBLANES // 2
    return {
        "w_in": w_in[l].astype(BF16),
        "wdw": jnp.repeat(w_dw[l], SUBLANES, axis=0),
        "bdw": b_dw[l][None], "clg": conv_ln_g[l][None], "clb": conv_ln_b[l][None],
        "a1": jnp.broadcast_to(ar, (SUBLANES, D_STATE)),
        "a2": jnp.concatenate([jnp.broadcast_to(-ai, (half, D_STATE)),
                               jnp.broadcast_to(ai, (half, D_STATE))], axis=0),
        "bre": b_block(jnp.real(b_bar)), "bim": b_block(jnp.imag(b_bar)),
        "cre": c_block(ssm_c_re[l]), "cim": c_block(-ssm_c_im[l]),
        "d": ssm_d[l][None], "wglu": ssm_w_glu[l].astype(BF16), "bglu": ssm_b_glu[l][None],
        "w_out": w_out[l].astype(BF16), "g1": ln1_g[l][None], "b1": ln1_b[l][None],
        "w_up": w_up, "b_up": b_up[l][:, None, :], "w_down": w_down, "layer": l,
        "g2": ln2_g[l][None], "b2": ln2_b[l][None],
    }


def kernel(x_prompt, x_sample, cache_conv, state_ssm_re, state_ssm_im, cache_mem_k, cache_mem_v, mem_prompt, w_in, w_dw, b_dw, conv_ln_g, conv_ln_b, ssm_a_re, ssm_a_im, ssm_b_re, ssm_b_im, ssm_c_re, ssm_c_im, ssm_d, ssm_log_dt, ssm_w_glu, ssm_b_glu, w_mem_k, w_mem_v, w_out, ln1_g, ln1_b, w_router, b_router, w_up, b_up, w_down, ln2_g, ln2_b):
    depth = w_in.shape[0]
    alpha = (2.0 * depth) ** 0.25
    bp, seq, _ = x_prompt.shape
    bs, dec_seq, _ = x_sample.shape
    assert bp == SEQS_PER_STREAM and bs % SEQS_PER_STREAM == 0
    lc_p = min(128, seq)
    lc_s = dec_seq
    assert seq % lc_p == 0 and lc_p % 16 == 0 and lc_s % 16 == 0

    mb, mm, _ = mem_prompt.shape
    mk_all, mv_all = _memkv_call(mem_prompt.reshape(mb * mm, D_MODEL),
                                 w_mem_k.astype(BF16), w_mem_v.astype(BF16))
    mk_all = mk_all.reshape(depth, mb, mm, N_MEM_HEADS, MEM_HEAD_DIM)
    mv_all = mv_all.reshape(depth, mb, mm, N_MEM_HEADS, MEM_HEAD_DIM)

    perm_p, permt_p = _perm_matrices(lc_p)
    perm_s, permt_s = _perm_matrices(lc_s)
    wr = w_router.T.astype(BF16)
    br = b_router[:, None]

    xp = x_prompt
    xs = x_sample
    zero_hist = jnp.zeros((1, HIST_ROWS, D_CONV), F32)
    zero_h = jnp.zeros((1, SUBLANES, D_STATE), F32)
    zero_cnt = jnp.zeros((N_EXPERTS, 1), F32)

    conv_p, re_p, im_p, conv_s, re_s, im_s = [], [], [], [], [], []
    moe_p = moe_s = None
    for l in range(depth):
        lw = _layer_params(l, w_in, w_dw, b_dw, conv_ln_g, conv_ln_b, ssm_a_re, ssm_a_im,
                           ssm_b_re, ssm_b_im, ssm_c_re, ssm_c_im, ssm_d, ssm_log_dt, ssm_w_glu,
                           ssm_b_glu, w_out, ln1_g, ln1_b, w_up, b_up, w_down, ln2_g, ln2_b)
        kp, vp = _pad_heads(mk_all[l], mv_all[l])
        xp, hist, hst, rows_p, meta_p, metat_p, cnt_p = _mixer_call(
            xp, moe_p, zero_hist, zero_h, kp, vp, perm_p, permt_p, lw, wr, br, zero_cnt,
            lc=lc_p, alpha=alpha)
        conv_p.append(_from_streams(hist, CONV_BUF))
        hr, hi = _unpack_state(hst)
        re_p.append(hr)
        im_p.append(hi)
        kp, vp = _pad_heads(cache_mem_k[l], cache_mem_v[l])
        xs, hist, hst, rows_s, meta_s, metat_s, cnt_s = _mixer_call(
            xs, moe_s, _to_streams(cache_conv[l]), _pack_state(state_ssm_re[l], state_ssm_im[l]),
            kp, vp, perm_s, permt_s, lw, wr, br, cnt_p, lc=lc_s, alpha=alpha)
        yg_p, yg_s = _moe_rows([rows_p, rows_s], [metat_p, metat_s], cnt_s, lw)
        moe_p = (yg_p, meta_p, lw["g2"], lw["b2"])
        moe_s = (yg_s, meta_s, lw["g2"], lw["b2"])
        conv_s.append(_from_streams(hist, CONV_BUF))
        hr, hi = _unpack_state(hst)
        re_s.append(hr)
        im_s.append(hi)

    y_prompt = _combine_call(xp, moe_p[0], moe_p[1], lw, alpha=alpha)
    y_sample = _combine_call(xs, moe_s[0], moe_s[1], lw, alpha=alpha)

    return (y_prompt, y_sample,
            jnp.stack(conv_p), jnp.stack(re_p), jnp.stack(im_p), mk_all, mv_all,
            jnp.stack(conv_s), jnp.stack(re_s), jnp.stack(im_s))
```

```python
import functools
import math

import numpy as np
import jax
import jax.numpy as jnp
from jax import lax
from jax.experimental import pallas as pl
from jax.experimental.pallas import tpu as pltpu
from jax.experimental.pallas import tpu_sc as plsc

F32 = jnp.float32
BF16 = jnp.bfloat16

D_MODEL = 1024
CONV_WIDTH = 31
CONV_BUF = CONV_WIDTH - 1
D_CONV = 384
D_SSM = 384
SSM_GROUP = 16
N_SSM_GROUPS = D_SSM // SSM_GROUP
SSM_STATE = 64
D_STATE = N_SSM_GROUPS * SSM_STATE
N_MEM = 256
N_MEM_HEADS = 4
MEM_HEAD_DIM = 64
D_ATT = N_MEM_HEADS * MEM_HEAD_DIM
D_IN = 2 * D_CONV + D_SSM + D_ATT
N_EXPERTS = 16
N_EXPERT_GROUPS = 4
EXPERTS_PER_GROUP = N_EXPERTS // N_EXPERT_GROUPS
D_EXPERT = 512
LN_EPS = 1e-5

SEQS_PER_STREAM = 4
HIST_ROWS = CONV_BUF * SEQS_PER_STREAM
SUBLANES = 8
LANES = 128
SCAN_LANES = 512
SSM_BLOCKS = D_STATE // SCAN_LANES
CONV_ROWS = 32
EXPERT_SUB_ROWS = 512
VMEM_LIMIT_BYTES = 56 * 1024 * 1024
HI_HALF_MASK = np.int32(-65536)
META_COLS = 8
SC_CORES = 2
SC_WORKERS = SC_CORES * 16
SC_MAX_CHUNK = 128


def _sigmoid(x):
    return 1.0 / (1.0 + jnp.exp(-x))


def _gelu_tanh(x):
    c = math.sqrt(2.0 / math.pi)
    return 0.5 * x * (1.0 + jnp.tanh(c * (x + 0.044715 * (x * x * x))))


def _layer_norm(z, g, b):
    mu = jnp.mean(z, axis=-1, keepdims=True)
    zc = z - mu
    var = jnp.mean(zc * zc, axis=-1, keepdims=True)
    return zc * lax.rsqrt(var + LN_EPS) * g + b


def _dot(a, b):
    return jnp.dot(a, b, preferred_element_type=F32)


def _mixer_kernel(*refs, tm, lc, alpha, fuse_in):
    refs = list(refs)
    if fuse_in:
        xprev_ref, yg_ref, metain_ref, g2p_ref, b2p_ref = refs[:5]
        refs = refs[5:]
    else:
        xprev_ref = refs.pop(0)
        xslab = refs.pop()
    (hist0_ref, h0_ref, k_ref, v_ref, perm_ref, permt_ref,
     w_in_ref, wdw_ref, bdw_ref, clg_ref, clb_ref, a1_ref, a2_ref,
     bre_ref, bim_ref, cre_ref, cim_ref, d_ref, wglu_ref, bglu_ref,
     wout_ref, g1_ref, b1_ref, wr_ref, br_ref, tri_ref, cnt0_ref,
     x1_ref, hist_out_ref, h_out_ref, xp_ref, meta_ref, metat_ref, cnt_ref,
     xpad0, xpad4, cy, ush, yim, bu_re, bu_im, hre, him, hcar, xin, running) = refs
    i = pl.program_id(1)

    @pl.when(i == 0)
    def _():
        xpad0[0:HIST_ROWS, :] = hist0_ref[0]
        hcar[...] = h0_ref[0]
        ush[...] = jnp.zeros_like(ush)

    @pl.when((i == 0) & (pl.program_id(0) == 0))
    def _():
        running[...] = cnt0_ref[...]

    if fuse_in:
        xin[...] = _moe_combine(xprev_ref[0], yg_ref, metain_ref[...], g2p_ref[...], b2p_ref[...], alpha)
    else:
        nslab = D_MODEL // LANES
        for s in range(SEQS_PER_STREAM):
            for j in range(nslab):
                xslab[j, pl.ds(s, lc, stride=SEQS_PER_STREAM), :] = xprev_ref[s, :, j * LANES:(j + 1) * LANES]
        xin[...] = jnp.concatenate([xslab[j] for j in range(nslab)], axis=1)
    x = xin[...]
    proj = _dot(x.astype(BF16), w_in_ref[...])

    g = proj[:, 0:D_CONV] * _sigmoid(proj[:, D_CONV:2 * D_CONV])
    xpad0[HIST_ROWS:HIST_ROWS + tm, :] = g
    xpad4[0:HIST_ROWS + tm - 4, :] = xpad0[4:HIST_ROWS + tm, :]

    nsub = CONV_ROWS // SUBLANES

    def conv_rows(rb, carry):
        r0 = pl.multiple_of(rb * CONV_ROWS, CONV_ROWS)
        accs = [jnp.broadcast_to(bdw_ref[...], (SUBLANES, D_CONV)) for _ in range(nsub)]
        for k in range(CONV_WIDTH):
            wk = wdw_ref[SUBLANES * k:SUBLANES * (k + 1), :]
            for sb in range(nsub):
                off = r0 + SEQS_PER_STREAM * k + SUBLANES * sb
                if k % 2 == 0:
                    xs = xpad0[pl.ds(pl.multiple_of(off, SUBLANES), SUBLANES), :]
                else:
                    xs = xpad4[pl.ds(pl.multiple_of(off - 4, SUBLANES), SUBLANES), :]
                accs[sb] = accs[sb] + xs * wk
        for sb in range(nsub):
            cy[pl.ds(pl.multiple_of(r0 + SUBLANES * sb, SUBLANES), SUBLANES), :] = accs[sb]
        return carry

    lax.fori_loop(0, tm // CONV_ROWS, conv_rows, 0, unroll=True)
    conv_n = _layer_norm(cy[...], clg_ref[...], clb_ref[...])
    cy[...] = conv_n * _sigmoid(conv_n)

    new_hist = xpad0[tm:tm + HIST_ROWS, :]
    xpad0[0:HIST_ROWS, :] = new_hist
    hist_out_ref[0] = new_hist

    u = proj[:, 2 * D_CONV:2 * D_CONV + D_SSM]
    ush[4:tm + 4, :] = u
    ub = u.astype(BF16)
    ub_sh = ush[...].astype(BF16)
    for m in range(SSM_BLOCKS):
        ch = slice(m * LANES, (m + 1) * LANES)
        st = slice(m * SCAN_LANES, (m + 1) * SCAN_LANES)
        bu_re[:, st] = _dot(ub[:, ch], bre_ref[m])
        bu_im[:, st] = _dot(ub_sh[:, ch], bim_ref[m])

    lo = lax.broadcasted_iota(jnp.int32, (SUBLANES, SCAN_LANES), 0) < 4
    for c in range(D_STATE // SCAN_LANES):
        cs = slice(c * SCAN_LANES, (c + 1) * SCAN_LANES)
        a1 = a1_ref[:, cs]
        a2 = a2_ref[:, cs]

        def scan_pair(j, carry, cs=cs, a1=a1, a2=a2):
            h_prev, im_cur = carry
            r = pl.multiple_of(j * SUBLANES, SUBLANES)
            re_cur = bu_re[pl.ds(r, SUBLANES), cs]
            im_next = bu_im[pl.ds(r + SUBLANES, SUBLANES), cs]
            p_even = jnp.where(lo, re_cur, im_cur)
            p_odd = jnp.where(lo, im_next, re_cur)
            h_even = a1 * pltpu.roll(h_prev, 4, 0) + a2 * h_prev + p_even
            h_odd = a1 * pltpu.roll(h_even, 4, 0) - a2 * h_even + p_odd
            hre[pl.ds(r, SUBLANES), cs] = jnp.where(lo, h_even, h_odd)
            him[pl.ds(r, SUBLANES), cs] = jnp.where(lo, h_prev, h_even)
            return h_odd, im_next

        h_last, _ = lax.fori_loop(0, lc // 2, scan_pair,
                                  (hcar[:, cs], bu_im[0:SUBLANES, cs]), unroll=True)
        him[tm:tm + SUBLANES, cs] = jnp.where(lo, h_last, 0.0)
        hcar[:, cs] = h_last
    h_out_ref[0] = hcar[...]

    y_re_blocks = []
    for m in range(SSM_BLOCKS):
        ch = slice(m * LANES, (m + 1) * LANES)
        st = slice(m * SCAN_LANES, (m + 1) * SCAN_LANES)
        y_re_blocks.append(_dot(hre[:, st].astype(BF16), cre_ref[m]))
        yim[:, ch] = _dot(him[:, st].astype(BF16), cim_ref[m])
    y_re = jnp.concatenate(y_re_blocks, axis=1)
    y = y_re + yim[4:tm + 4, :] + d_ref[...] * u
    y = _gelu_tanh(y)
    ssm_y = y * _sigmoid(_dot(y.astype(BF16), wglu_ref[...]) + bglu_ref[...])

    q = proj[:, 2 * D_CONV + D_SSM:D_IN].astype(BF16)
    q_seq = _dot(perm_ref[...], q)
    head_of_col = lax.shift_right_logical(lax.broadcasted_iota(jnp.int32, (1, D_ATT), 1), 6)
    head_masks = [jnp.where(head_of_col == h, 1.0, 0.0) for h in range(N_MEM_HEADS)]
    outs = []
    for s in range(SEQS_PER_STREAM):
        qs = q_seq[s * lc:(s + 1) * lc, :]
        ks = k_ref[0, s]
        vs = v_ref[0, s]
        acc = jnp.zeros((lc, D_ATT), F32)
        for h in range(N_MEM_HEADS):
            qh = (qs * head_masks[h]).astype(BF16)
            sc = _dot(qh, ks) * (MEM_HEAD_DIM ** -0.5)
            sc = sc - jnp.max(sc, axis=-1, keepdims=True)
            e = jnp.exp(sc)
            p = e * (1.0 / jnp.sum(e, axis=-1, keepdims=True))
            acc = acc + _dot(p.astype(BF16), vs) * head_masks[h]
        outs.append(acc)
    att_seq = jnp.concatenate(outs, axis=0).astype(BF16)
    att = _dot(permt_ref[...], att_seq).astype(BF16)

    mix = _dot(jnp.concatenate([cy[...].astype(BF16), ssm_y.astype(BF16), att], axis=1), wout_ref[...])
    x1 = _layer_norm(alpha * xin[...] + mix, g1_ref[...], b1_ref[...])
    x1_ref[0] = x1
    _route_rows(x1, wr_ref, br_ref, tri_ref, xp_ref, meta_ref, metat_ref, cnt_ref, running)


def _mixer_call(x, prev_moe, hist0, h0, kpad, vpad, perm, permt, lw, wr, br, cnt0, *, lc, alpha,
                tile_off=0, n_tiles=None):
    fuse_in = prev_moe is not None
    tm = lc * SEQS_PER_STREAM
    if fuse_in:
        nq, rows, _ = x.shape
    else:
        nq = x.shape[0] // SEQS_PER_STREAM
        rows = (x.shape[1] // lc if n_tiles is None else n_tiles) * tm
    nt = rows // tm
    kern = functools.partial(_mixer_kernel, tm=tm, lc=lc, alpha=alpha, fuse_in=fuse_in)
    triu = jnp.asarray(np.triu(np.ones((tm, tm), np.float32), 1), BF16)

    def const(shape):
        return pl.BlockSpec(shape, lambda q, i: (0,) * len(shape))

    def flat(shape):
        return pl.BlockSpec(shape, lambda q, i: (0,) * (len(shape) - 2) + (q * nt + i, 0))

    operands = [x]
    if fuse_in:
        in_specs = [pl.BlockSpec((1, tm, D_MODEL), lambda q, i: (q, i, 0)),
                    flat((2, tm, D_MODEL // 2)), flat((tm, META_COLS)),
                    const((1, D_MODEL)), const((1, D_MODEL))]
        operands += list(prev_moe)
    else:
        in_specs = [pl.BlockSpec((SEQS_PER_STREAM, lc, D_MODEL), lambda q, i: (q, i + tile_off, 0))]
    in_specs += [
        pl.BlockSpec((1, HIST_ROWS, D_CONV), lambda q, i: (q, 0, 0)),
        pl.BlockSpec((1, SUBLANES, D_STATE), lambda q, i: (q, 0, 0)),
        pl.BlockSpec((1, SEQS_PER_STREAM, N_MEM, D_ATT), lambda q, i: (q, 0, 0, 0)),
        pl.BlockSpec((1, SEQS_PER_STREAM, N_MEM, D_ATT), lambda q, i: (q, 0, 0, 0)),
        const((tm, tm)), const((tm, tm)),
        const((D_MODEL, D_IN)),
        const((CONV_WIDTH * SUBLANES, D_CONV)), const((1, D_CONV)), const((1, D_CONV)), const((1, D_CONV)),
        const((SUBLANES, D_STATE)), const((SUBLANES, D_STATE)),
        const((SSM_BLOCKS, LANES, SCAN_LANES)), const((SSM_BLOCKS, LANES, SCAN_LANES)),
        const((SSM_BLOCKS, SCAN_LANES, LANES)), const((SSM_BLOCKS, SCAN_LANES, LANES)),
        const((1, D_SSM)), const((D_SSM, D_SSM)), const((1, D_SSM)),
        const((D_MODEL, D_MODEL)), const((1, D_MODEL)), const((1, D_MODEL)),
        const((N_EXPERTS, D_MODEL)), const((N_EXPERTS, 1)), const((tm, tm)), const((N_EXPERTS, 1)),
    ]
    out_specs = [
        pl.BlockSpec((1, tm, D_MODEL), lambda q, i: (q, i, 0)),
        pl.BlockSpec((1, HIST_ROWS, D_CONV), lambda q, i: (q, 0, 0)),
        pl.BlockSpec((1, SUBLANES, D_STATE), lambda q, i: (q, 0, 0)),
        flat((tm, D_MODEL // 2)), flat((tm, META_COLS)),
        pl.BlockSpec((1, META_COLS, tm), lambda q, i: (q * nt + i, 0, 0)), const((N_EXPERTS, 1)),
    ]
    out_shape = [
        jax.ShapeDtypeStruct((nq, rows, D_MODEL), F32),
        jax.ShapeDtypeStruct((nq, HIST_ROWS, D_CONV), F32),
        jax.ShapeDtypeStruct((nq, SUBLANES, D_STATE), F32),
        jax.ShapeDtypeStruct((nq * rows, D_MODEL // 2), jnp.int32),
        jax.ShapeDtypeStruct((nq * rows, META_COLS), F32),
        jax.ShapeDtypeStruct((nq * nt, META_COLS, tm), F32),
        jax.ShapeDtypeStruct((N_EXPERTS, 1), F32),
    ]
    scratch = [
        pltpu.VMEM((HIST_ROWS + tm + SUBLANES, D_CONV), F32),
        pltpu.VMEM((HIST_ROWS + tm + SUBLANES, D_CONV), F32),
        pltpu.VMEM((tm, D_CONV), F32),
        pltpu.VMEM((tm + SUBLANES, D_SSM), F32),
        pltpu.VMEM((tm + SUBLANES, D_SSM), F32),
        pltpu.VMEM((tm, D_STATE), F32),
        pltpu.VMEM((tm + SUBLANES, D_STATE), F32),
        pltpu.VMEM((tm, D_STATE), F32),
        pltpu.VMEM((tm + SUBLANES, D_STATE), F32),
        pltpu.VMEM((SUBLANES, D_STATE), F32),
        pltpu.VMEM((tm, D_MODEL), F32),
        pltpu.VMEM((N_EXPERTS, 1), F32),
    ]
    if not fuse_in:
        scratch.append(pltpu.VMEM((D_MODEL // LANES, tm, LANES), F32))
    return pl.pallas_call(
        kern,
        grid=(nq, nt),
        in_specs=in_specs,
        out_specs=out_specs,
        out_shape=out_shape,
        scratch_shapes=scratch,
        compiler_params=pltpu.CompilerParams(
            dimension_semantics=("arbitrary", "arbitrary"),
            vmem_limit_bytes=VMEM_LIMIT_BYTES),
        name="mixer",
    )(*operands, hist0, h0, kpad, vpad, perm, permt,
      lw["w_in"], lw["wdw"], lw["bdw"], lw["clg"], lw["clb"], lw["a1"], lw["a2"],
      lw["bre"], lw["bim"], lw["cre"], lw["cim"], lw["d"], lw["wglu"], lw["bglu"],
      lw["w_out"], lw["g1"], lw["b1"], wr, br, triu, cnt0)


def _route(logits_t):
    m = jnp.max(logits_t, axis=0, keepdims=True)
    e = jnp.exp(logits_t - m)
    aff = e / jnp.sum(e, axis=0, keepdims=True)
    rows = [aff[j:j + 1, :] for j in range(N_EXPERTS)]

    scores = []
    for gi in range(N_EXPERT_GROUPS):
        a, b, c, d = rows[EXPERTS_PER_GROUP * gi:EXPERTS_PER_GROUP * (gi + 1)]
        hi1, lo1 = jnp.maximum(a, b), jnp.minimum(a, b)
        hi2, lo2 = jnp.maximum(c, d), jnp.minimum(c, d)
        scores.append(jnp.maximum(hi1, hi2) + jnp.maximum(jnp.minimum(hi1, hi2), jnp.maximum(lo1, lo2)))
    best = scores[0]
    sel = jnp.zeros_like(best)
    for gi in range(1, N_EXPERT_GROUPS):
        better = scores[gi] > best
        sel = jnp.where(better, float(gi), sel)
        best = jnp.where(better, scores[gi], best)

    hot1, hot2 = [], []
    for gi in range(N_EXPERT_GROUPS):
        vals = rows[EXPERTS_PER_GROUP * gi:EXPERTS_PER_GROUP * (gi + 1)]
        chosen = sel == float(gi)
        for j in range(EXPERTS_PER_GROUP):
            ahead = jnp.zeros_like(best)
            for k in range(EXPERTS_PER_GROUP):
                if k < j:
                    ahead = ahead + jnp.where(vals[k] >= vals[j], 1.0, 0.0)
                elif k > j:
                    ahead = ahead + jnp.where(vals[k] > vals[j], 1.0, 0.0)
            hot1.append(jnp.where(chosen, jnp.where(ahead == 0.0, 1.0, 0.0), 0.0))
            hot2.append(jnp.where(chosen, jnp.where(ahead == 1.0, 1.0, 0.0), 0.0))
    return jnp.concatenate(hot1, axis=0), jnp.concatenate(hot2, axis=0), aff


def _pack_halves(y):
    half = y.shape[1] // 2
    lo = lax.bitcast_convert_type(y[:, :half].astype(BF16).astype(F32), jnp.int32)
    hi = lax.bitcast_convert_type(y[:, half:].astype(BF16).astype(F32), jnp.int32)
    return lax.shift_right_logical(lo, 16) | (hi & HI_HALF_MASK)


def _unpack_halves(w):
    lo = lax.bitcast_convert_type(lax.shift_left(w, 16), F32)
    hi = lax.bitcast_convert_type(w & HI_HALF_MASK, F32)
    return lo, hi


def _route_rows(x, wr_ref, br_ref, triu_ref, xp_ref, meta_ref, metat_ref, cnt_ref, running):
    tm = x.shape[0]
    xp_ref[...] = _pack_halves(x)
    logits_t = lax.dot_general(wr_ref[...], x.astype(BF16), (((1,), (1,)), ((), ())),
                               preferred_element_type=F32) + br_ref[...]
    hot1, hot2, aff = _route(logits_t)
    eid = lax.broadcasted_iota(jnp.int32, (N_EXPERTS, tm), 0).astype(F32)
    both = hot1 + hot2
    before = _dot(both.astype(BF16), triu_ref[...]) + running[...]

    def pick(hot, vals):
        return jnp.sum(hot * vals, axis=0, keepdims=True)

    v1 = pick(hot1, aff)
    v2 = pick(hot2, aff)
    denom = v1 + v2
    meta_t = jnp.concatenate(
        [pick(hot1, eid), pick(hot2, eid), v1 / denom, v2 / denom, pick(hot1, before), pick(hot2, before),
         jnp.zeros((LANES - 6, tm), F32)], axis=0)
    metat_ref[0] = meta_t[0:META_COLS, :]
    meta_ref[...] = meta_t.T[:, 0:META_COLS]
    running[...] = running[...] + jnp.sum(both, axis=1, keepdims=True)
    cnt_ref[...] = running[...]


def _expert_kernel(te_ref, nt_ref, xs_ref, wup_ref, bup_ref, wdn_ref, ys_ref, wup_bf, wdn_bf):
    i = pl.program_id(0)
    prev = te_ref[jnp.maximum(i - 1, 0)]

    @pl.when((i == 0) | (te_ref[i] != prev))
    def _():
        wup_bf[...] = wup_ref[0, 0].astype(BF16)
        wdn_bf[...] = wdn_ref[0, 0].astype(BF16)

    @pl.when(i < nt_ref[0])
    def _():
        half = D_MODEL // 2
        rt = xs_ref.shape[0]
        sub = min(EXPERT_SUB_ROWS, rt)
        for r in range(rt // sub):
            rows = slice(r * sub, (r + 1) * sub)
            lo, hi = _unpack_halves(xs_ref[rows, :])
            h = (_dot(lo.astype(BF16), wup_bf[0:half, :])
                 + _dot(hi.astype(BF16), wup_bf[half:D_MODEL, :]) + bup_ref[0])
            ys_ref[rows, :] = _pack_halves(_dot(_gelu_tanh(h).astype(BF16), wdn_bf[...]))


def _expert_call(xs, tile_expert, n_tiles, lw, *, rt):
    layer = lw["layer"]
    rows = xs.shape[0]

    def row_map(i, te, nt):
        return (jnp.minimum(i, nt[0] - 1), 0)

    def w_map(i, te, nt):
        return (layer, te[i], 0, 0)

    return pl.pallas_call(
        _expert_kernel,
        grid_spec=pltpu.PrefetchScalarGridSpec(
            num_scalar_prefetch=2,
            grid=(rows // rt,),
            in_specs=[
                pl.BlockSpec((rt, D_MODEL // 2), row_map),
                pl.BlockSpec((1, 1, D_MODEL, D_EXPERT), w_map),
                pl.BlockSpec((1, 1, D_EXPERT), lambda i, te, nt: (te[i], 0, 0)),
                pl.BlockSpec((1, 1, D_EXPERT, D_MODEL), w_map),
            ],
            out_specs=pl.BlockSpec((rt, D_MODEL // 2), row_map),
            scratch_shapes=[pltpu.VMEM((D_MODEL, D_EXPERT), BF16),
                            pltpu.VMEM((D_EXPERT, D_MODEL), BF16)],
        ),
        out_shape=jax.ShapeDtypeStruct((rows, D_MODEL // 2), jnp.int32),
        compiler_params=pltpu.CompilerParams(
            dimension_semantics=("arbitrary",), vmem_limit_bytes=VMEM_LIMIT_BYTES),
        name="experts",
    )(tile_expert, n_tiles, xs, lw["w_up"], lw["b_up"], lw["w_down"])


def _moe_combine(x1, yg_ref, meta, g, b, alpha):
    g1 = meta[:, 2:3]
    g2 = meta[:, 3:4]
    lo1, hi1 = _unpack_halves(yg_ref[0])
    lo2, hi2 = _unpack_halves(yg_ref[1])
    moe = jnp.concatenate([g1 * lo1 + g2 * lo2, g1 * hi1 + g2 * hi2], axis=1)
    return _layer_norm(alpha * x1 + moe, g, b)


def _combine_kernel(x_ref, yg_ref, meta_ref, g2_ref, b2_ref, *rest, alpha):
    o_ref, yslab = rest[-2:]
    y = _moe_combine(x_ref[...], yg_ref, meta_ref[...], g2_ref[...], b2_ref[...], alpha)
    nslab = D_MODEL // LANES
    steps = y.shape[0] // SEQS_PER_STREAM
    for j in range(nslab):
        yslab[j] = y[:, j * LANES:(j + 1) * LANES]
    for s in range(SEQS_PER_STREAM):
        for j in range(nslab):
            o_ref[s, :, j * LANES:(j + 1) * LANES] = yslab[j, pl.ds(s, steps, stride=SEQS_PER_STREAM), :]


def _combine_call(x1, yg, meta, lw, *, alpha, seq_len, step_off=0, prev_out=None):
    nq, srows, _ = x1.shape
    tmc = min(1024, srows)
    tps = srows // tmc
    steps = tmc // SEQS_PER_STREAM
    blk_off = step_off // steps
    in_specs = [
        pl.BlockSpec((tmc, D_MODEL), lambda i: (i, 0)),
        pl.BlockSpec((2, tmc, D_MODEL // 2), lambda i: (0, i, 0)),
        pl.BlockSpec((tmc, META_COLS), lambda i: (i, 0)),
        pl.BlockSpec((1, D_MODEL), lambda i: (0, 0)),
        pl.BlockSpec((1, D_MODEL), lambda i: (0, 0)),
    ]
    operands = [x1.reshape(nq * srows, D_MODEL), yg, meta, lw["g2"], lw["b2"]]
    aliases = {}
    if prev_out is not None:
        in_specs.append(pl.BlockSpec(memory_space=pl.ANY))
        operands.append(prev_out)
        aliases = {len(operands) - 1: 0}
    return pl.pallas_call(
        functools.partial(_combine_kernel, alpha=alpha),
        grid=(nq * tps,),
        in_specs=in_specs,
        out_specs=pl.BlockSpec((SEQS_PER_STREAM, steps, D_MODEL),
                               lambda i: (i // tps, i % tps + blk_off, 0)),
        out_shape=jax.ShapeDtypeStruct((nq * SEQS_PER_STREAM, seq_len, D_MODEL), F32),
        scratch_shapes=[pltpu.VMEM((D_MODEL // LANES, tmc, LANES), F32)],
        input_output_aliases=aliases,
        compiler_params=pltpu.CompilerParams(
            dimension_semantics=("arbitrary",), vmem_limit_bytes=VMEM_LIMIT_BYTES),
        name="combine",
    )(*operands)


def _sc_mesh():
    return plsc.VectorSubcoreMesh(core_axis_name="c", subcore_axis_name="s")


def _sc_chunk(rows):
    per_worker = rows // SC_WORKERS
    chunk = min(SC_MAX_CHUNK, per_worker)
    assert per_worker % chunk == 0 and chunk % 8 == 0
    return per_worker // chunk, chunk


def _sc_dispatch(xps, poss, n_slots):
    width = xps[0].shape[1]
    ngroups = len(xps)
    plans = [_sc_chunk(x.shape[0]) for x in xps]
    posrs = [p.reshape(2, SC_WORKERS, nch, ch) for p, (nch, ch) in zip(poss, plans)]
    scratch = []
    for _, ch in plans:
        scratch += [pltpu.VMEM((ch,), jnp.int32), pltpu.VMEM((ch,), jnp.int32),
                    pltpu.VMEM((ch, width), jnp.int32)]

    @functools.partial(
        pl.kernel, mesh=_sc_mesh(),
        out_type=jax.ShapeDtypeStruct((n_slots, width), jnp.int32),
        scratch_types=scratch + [pltpu.SemaphoreType.DMA],
        name="sc_dispatch")
    def k(*refs):
        x_hbms, pos_hbms, o_hbm = refs[:ngroups], refs[ngroups:2 * ngroups], refs[2 * ngroups]
        bufs, sem = refs[2 * ngroups + 1:-1], refs[-1]
        wid = lax.axis_index("s") * SC_CORES + lax.axis_index("c")
        for g, (nch, ch) in enumerate(plans):
            idx0, idx1, buf = bufs[3 * g:3 * g + 3]

            @pl.loop(0, nch)
            def _(c, g=g, nch=nch, ch=ch, idx0=idx0, idx1=idx1, buf=buf):
                base = (wid * nch + c) * ch
                pltpu.sync_copy(x_hbms[g].at[pl.ds(base, ch)], buf)
                pltpu.sync_copy(pos_hbms[g].at[0, wid, c], idx0)
                pltpu.sync_copy(pos_hbms[g].at[1, wid, c], idx1)
                pltpu.async_copy(buf, o_hbm.at[idx0], sem).wait()
                pltpu.async_copy(buf, o_hbm.at[idx1], sem).wait()

    return k(*xps, *posrs)


def _sc_gather(ys, poss):
    width = ys.shape[1]
    ngroups = len(poss)
    plans = [_sc_chunk(p.shape[1]) for p in poss]
    posrs = [p.reshape(2, SC_WORKERS, nch, ch) for p, (nch, ch) in zip(poss, plans)]
    scratch = []
    for _, ch in plans:
        scratch += [pltpu.VMEM((ch,), jnp.int32), pltpu.VMEM((ch, width), jnp.int32)]

    @functools.partial(
        pl.kernel, mesh=_sc_mesh(),
        out_type=[jax.ShapeDtypeStruct((2, p.shape[1], width), jnp.int32) for p in poss],
        scratch_types=scratch + [pltpu.SemaphoreType.DMA],
        name="sc_gather")
    def k(*refs):
        y_hbm, pos_hbms = refs[0], refs[1:1 + ngroups]
        o_hbms = refs[1 + ngroups:1 + 2 * ngroups]
        bufs, sem = refs[1 + 2 * ngroups:-1], refs[-1]
        wid = lax.axis_index("s") * SC_CORES + lax.axis_index("c")
        for g, (nch, ch) in enumerate(plans):
            idx, buf = bufs[2 * g:2 * g + 2]

            @pl.loop(0, nch)
            def _(c, g=g, nch=nch, ch=ch, idx=idx, buf=buf):
                base = (wid * nch + c) * ch
                for kk in range(2):
                    pltpu.sync_copy(pos_hbms[g].at[kk, wid, c], idx)
                    pltpu.async_copy(y_hbm.at[idx], buf, sem).wait()
                    pltpu.sync_copy(buf, o_hbms[g].at[kk, pl.ds(base, ch)])

    return k(ys, *posrs)


def _moe_rows(xps, meta_ts, counts, lw):
    total = sum(x.shape[0] for x in xps)
    rt = 512 if total >= 8192 else 128
    n_slots = 2 * total + N_EXPERTS * rt
    nt_max = n_slots // rt

    cnt = counts[:, 0].astype(jnp.int32)
    padded = ((cnt + rt - 1) // rt) * rt
    ends = jnp.cumsum(padded)
    offs = ends - padded
    experts = jnp.arange(N_EXPERTS, dtype=jnp.int32)[:, None]
    poss = []
    for x, meta_t in zip(xps, meta_ts):
        meta_t = meta_t.transpose(1, 0, 2).reshape(META_COLS, x.shape[0])
        eidx = meta_t[0:2].astype(jnp.int32)
        rank = meta_t[4:6].astype(jnp.int32)
        poss.append(jnp.sum(jnp.where(eidx[:, None, :] == experts, offs[:, None], 0), axis=1) + rank)
    n_tiles = ends[-1:] // rt
    tiles = jnp.minimum(jnp.arange(nt_max, dtype=jnp.int32), n_tiles[0] - 1)
    tile_expert = jnp.minimum(jnp.sum((ends // rt)[None, :] <= tiles[:, None], axis=1),
                              N_EXPERTS - 1).astype(jnp.int32)

    xs = _sc_dispatch(xps, poss, n_slots)
    ys = _expert_call(xs, tile_expert, n_tiles.astype(jnp.int32), lw, rt=rt)
    return _sc_gather(ys, poss)


def _memkv_kernel(m_ref, wk_ref, wv_ref, k_ref, v_ref):
    mb = m_ref[...].astype(BF16)
    k_ref[0] = _dot(mb, wk_ref[0])
    v_ref[0] = _dot(mb, wv_ref[0])


def _memkv_call(mem, wk, wv):
    depth = wk.shape[0]
    rows = mem.shape[0]
    return pl.pallas_call(
        _memkv_kernel,
        grid=(depth,),
        in_specs=[
            pl.BlockSpec((rows, D_MODEL), lambda l: (0, 0)),
            pl.BlockSpec((1, D_MODEL, D_ATT), lambda l: (l, 0, 0)),
            pl.BlockSpec((1, D_MODEL, D_ATT), lambda l: (l, 0, 0)),
        ],
        out_specs=[
            pl.BlockSpec((1, rows, D_ATT), lambda l: (l, 0, 0)),
            pl.BlockSpec((1, rows, D_ATT), lambda l: (l, 0, 0)),
        ],
        out_shape=[jax.ShapeDtypeStruct((depth, rows, D_ATT), F32)] * 2,
        compiler_params=pltpu.CompilerParams(
            dimension_semantics=("arbitrary",), vmem_limit_bytes=VMEM_LIMIT_BYTES),
        name="memkv",
    )(mem, wk, wv)


def _perm_matrices(lc):
    tm = lc * SEQS_PER_STREAM
    p = np.zeros((tm, tm), np.float32)
    for s in range(SEQS_PER_STREAM):
        for t in range(lc):
            p[s * lc + t, t * SEQS_PER_STREAM + s] = 1.0
    return jnp.asarray(p, BF16), jnp.asarray(p.T, BF16)


def _to_streams(a):
    b, l, c = a.shape
    q = b // SEQS_PER_STREAM
    return a.reshape(q, SEQS_PER_STREAM, l, c).transpose(0, 2, 1, 3).reshape(q, l * SEQS_PER_STREAM, c)


def _from_streams(a, l):
    q, _, c = a.shape
    return a.reshape(q, l, SEQS_PER_STREAM, c).transpose(0, 2, 1, 3).reshape(q * SEQS_PER_STREAM, l, c)


def _pack_state(re, im):
    b = re.shape[0]
    q = b // SEQS_PER_STREAM
    re = re.reshape(q, SEQS_PER_STREAM, D_STATE)
    im = im.reshape(q, SEQS_PER_STREAM, D_STATE)
    return jnp.concatenate([im, re], axis=1)


def _unpack_state(h):
    q = h.shape[0]
    im = h[:, 0:SEQS_PER_STREAM].reshape(q * SEQS_PER_STREAM, N_SSM_GROUPS, SSM_STATE)
    re = h[:, SEQS_PER_STREAM:].reshape(q * SEQS_PER_STREAM, N_SSM_GROUPS, SSM_STATE)
    return re, im


def _pad_heads(mk, mv):
    b = mk.shape[0]
    q = b // SEQS_PER_STREAM
    shape = (q, SEQS_PER_STREAM, N_MEM, D_ATT)
    return mk.reshape(shape).transpose(0, 1, 3, 2).astype(BF16), mv.reshape(shape).astype(BF16)


def _layer_params(l, w_in, w_dw, b_dw, conv_ln_g, conv_ln_b, ssm_a_re, ssm_a_im, ssm_b_re,
                  ssm_b_im, ssm_c_re, ssm_c_im, ssm_d, ssm_log_dt, ssm_w_glu, ssm_b_glu,
                  w_out, ln1_g, ln1_b, w_up, b_up, w_down, ln2_g, ln2_b):
    g, p = N_SSM_GROUPS, SSM_STATE
    a = lax.complex(ssm_a_re[l], ssm_a_im[l])
    dt = jnp.exp(ssm_log_dt[l])[:, None]
    a_bar = jnp.exp(a * dt)
    b_bar = ((a_bar - 1.0) / a)[..., None] * lax.complex(ssm_b_re[l], ssm_b_im[l])
    eye = jnp.eye(g, dtype=F32)

    def b_block(m):
        full = jnp.einsum("gpi,gh->gihp", m, eye).reshape(D_SSM, D_STATE).astype(BF16)
        return jnp.stack([full[k * LANES:(k + 1) * LANES, k * SCAN_LANES:(k + 1) * SCAN_LANES]
                          for k in range(SSM_BLOCKS)])

    def c_block(m):
        full = jnp.einsum("gip,gh->gphi", m, eye).reshape(D_STATE, D_SSM).astype(BF16)
        return jnp.stack([full[k * SCAN_LANES:(k + 1) * SCAN_LANES, k * LANES:(k + 1) * LANES]
                          for k in range(SSM_BLOCKS)])

    ar = jnp.real(a_bar).reshape(1, D_STATE)
    ai = jnp.imag(a_bar).reshape(1, D_STATE)
    half = SUBLANES // 2
    return {
        "w_in": w_in[l].astype(BF16),
        "wdw": jnp.repeat(w_dw[l], SUBLANES, axis=0),
        "bdw": b_dw[l][None], "clg": conv_ln_g[l][None], "clb": conv_ln_b[l][None],
        "a1": jnp.broadcast_to(ar, (SUBLANES, D_STATE)),
        "a2": jnp.concatenate([jnp.broadcast_to(-ai, (half, D_STATE)),
                               jnp.broadcast_to(ai, (half, D_STATE))], axis=0),
        "bre": b_block(jnp.real(b_bar)), "bim": b_block(jnp.imag(b_bar)),
        "cre": c_block(ssm_c_re[l]), "cim": c_block(-ssm_c_im[l]),
        "d": ssm_d[l][None], "wglu": ssm_w_glu[l].astype(BF16), "bglu": ssm_b_glu[l][None],
        "w_out": w_out[l].astype(BF16), "g1": ln1_g[l][None], "b1": ln1_b[l][None],
        "w_up": w_up, "b_up": b_up[l][:, None, :], "w_down": w_down, "layer": l,
        "g2": ln2_g[l][None], "b2": ln2_b[l][None],
    }


def kernel(x_prompt, x_sample, cache_conv, state_ssm_re, state_ssm_im, cache_mem_k, cache_mem_v, mem_prompt, w_in, w_dw, b_dw, conv_ln_g, conv_ln_b, ssm_a_re, ssm_a_im, ssm_b_re, ssm_b_im, ssm_c_re, ssm_c_im, ssm_d, ssm_log_dt, ssm_w_glu, ssm_b_glu, w_mem_k, w_mem_v, w_out, ln1_g, ln1_b, w_router, b_router, w_up, b_up, w_down, ln2_g, ln2_b):
    depth = w_in.shape[0]
    alpha = (2.0 * depth) ** 0.25
    bp, seq, _ = x_prompt.shape
    bs, dec_seq, _ = x_sample.shape
    assert bp == SEQS_PER_STREAM and bs % SEQS_PER_STREAM == 0
    lc_p = min(128, seq)
    lc_s = dec_seq
    assert seq % lc_p == 0 and lc_p % 16 == 0 and lc_s % 16 == 0

    mb, mm, _ = mem_prompt.shape
    mk_all, mv_all = _memkv_call(mem_prompt.reshape(mb * mm, D_MODEL),
                                 w_mem_k.astype(BF16), w_mem_v.astype(BF16))
    mk_all = mk_all.reshape(depth, mb, mm, N_MEM_HEADS, MEM_HEAD_DIM)
    mv_all = mv_all.reshape(depth, mb, mm, N_MEM_HEADS, MEM_HEAD_DIM)

    perm_p, permt_p = _perm_matrices(lc_p)
    perm_s, permt_s = _perm_matrices(lc_s)
    wr = w_router.T.astype(BF16)
    br = b_router[:, None]

    tiles_p = seq // lc_p
    tiles_a = max(tiles_p // 2, 1)
    tiles_b = tiles_p - tiles_a
    steps_a = tiles_a * lc_p
    xa = xb = x_prompt
    xs = x_sample
    zero_hist = jnp.zeros((1, HIST_ROWS, D_CONV), F32)
    zero_h = jnp.zeros((1, SUBLANES, D_STATE), F32)
    zero_cnt = jnp.zeros((N_EXPERTS, 1), F32)

    conv_p, re_p, im_p, conv_s, re_s, im_s = [], [], [], [], [], []
    moe_a = moe_b = moe_s = None
    for l in range(depth):
        lw = _layer_params(l, w_in, w_dw, b_dw, conv_ln_g, conv_ln_b, ssm_a_re, ssm_a_im,
                           ssm_b_re, ssm_b_im, ssm_c_re, ssm_c_im, ssm_d, ssm_log_dt, ssm_w_glu,
                           ssm_b_glu, w_out, ln1_g, ln1_b, w_up, b_up, w_down, ln2_g, ln2_b)
        kp, vp = _pad_heads(mk_all[l], mv_all[l])
        xa, hist, hst, rows_a, meta_a, metat_a, cnt_a = _mixer_call(
            xa, moe_a, zero_hist, zero_h, kp, vp, perm_p, permt_p, lw, wr, br, zero_cnt,
            lc=lc_p, alpha=alpha, tile_off=0, n_tiles=tiles_a)
        (yg_a,) = _moe_rows([rows_a], [metat_a], cnt_a, lw)
        moe_a = (yg_a, meta_a, lw["g2"], lw["b2"])
        xb, hist, hst, rows_b, meta_b, metat_b, cnt_b = _mixer_call(
            xb, moe_b, hist, hst, kp, vp, perm_p, permt_p, lw, wr, br, zero_cnt,
            lc=lc_p, alpha=alpha, tile_off=tiles_a, n_tiles=tiles_b)
        conv_p.append(_from_streams(hist, CONV_BUF))
        hr, hi = _unpack_state(hst)
        re_p.append(hr)
        im_p.append(hi)
        kp, vp = _pad_heads(cache_mem_k[l], cache_mem_v[l])
        xs, hist, hst, rows_s, meta_s, metat_s, cnt_s = _mixer_call(
            xs, moe_s, _to_streams(cache_conv[l]), _pack_state(state_ssm_re[l], state_ssm_im[l]),
            kp, vp, perm_s, permt_s, lw, wr, br, cnt_b, lc=lc_s, alpha=alpha)
        yg_b, yg_s = _moe_rows([rows_b, rows_s], [metat_b, metat_s], cnt_s, lw)
        moe_b = (yg_b, meta_b, lw["g2"], lw["b2"])
        moe_s = (yg_s, meta_s, lw["g2"], lw["b2"])
        conv_s.append(_from_streams(hist, CONV_BUF))
        hr, hi = _unpack_state(hst)
        re_s.append(hr)
        im_s.append(hi)

    y_prompt = _combine_call(xa, moe_a[0], moe_a[1], lw, alpha=alpha, seq_len=seq)
    y_prompt = _combine_call(xb, moe_b[0], moe_b[1], lw, alpha=alpha, seq_len=seq, step_off=steps_a,
                             prev_out=y_prompt)
    y_sample = _combine_call(xs, moe_s[0], moe_s[1], lw, alpha=alpha, seq_len=dec_seq)

    return (y_prompt, y_sample,
            jnp.stack(conv_p), jnp.stack(re_p), jnp.stack(im_p), mk_all, mv_all,
            jnp.stack(conv_s), jnp.stack(re_s), jnp.stack(im_s))
```

```python
import functools
import math

import numpy as np
import jax
import jax.numpy as jnp
from jax import lax
from jax.experimental import pallas as pl
from jax.experimental.pallas import tpu as pltpu
from jax.experimental.pallas import tpu_sc as plsc

F32 = jnp.float32
BF16 = jnp.bfloat16

D_MODEL = 1024
CONV_WIDTH = 31
CONV_BUF = CONV_WIDTH - 1
D_CONV = 384
D_SSM = 384
SSM_GROUP = 16
N_SSM_GROUPS = D_SSM // SSM_GROUP
SSM_STATE = 64
D_STATE = N_SSM_GROUPS * SSM_STATE
N_MEM = 256
N_MEM_HEADS = 4
MEM_HEAD_DIM = 64
D_ATT = N_MEM_HEADS * MEM_HEAD_DIM
D_IN = 2 * D_CONV + D_SSM + D_ATT
N_EXPERTS = 16
N_EXPERT_GROUPS = 4
EXPERTS_PER_GROUP = N_EXPERTS // N_EXPERT_GROUPS
D_EXPERT = 512
LN_EPS = 1e-5

SEQS_PER_STREAM = 4
HIST_ROWS = CONV_BUF * SEQS_PER_STREAM
SUBLANES = 8
LANES = 128
SCAN_LANES = 512
SSM_BLOCKS = D_STATE // SCAN_LANES
CONV_ROWS = 32
EXPERT_SUB_ROWS = 512
VMEM_LIMIT_BYTES = 56 * 1024 * 1024
HI_HALF_MASK = np.int32(-65536)
META_COLS = 8
SC_CORES = 2
SC_WORKERS = SC_CORES * 16
SC_MAX_CHUNK = 128


def _sigmoid(x):
    return 1.0 / (1.0 + jnp.exp(-x))


def _gelu_tanh(x):
    c = math.sqrt(2.0 / math.pi)
    return 0.5 * x * (1.0 + jnp.tanh(c * (x + 0.044715 * (x * x * x))))


def _layer_norm(z, g, b):
    mu = jnp.mean(z, axis=-1, keepdims=True)
    zc = z - mu
    var = jnp.mean(zc * zc, axis=-1, keepdims=True)
    return zc * lax.rsqrt(var + LN_EPS) * g + b


def _dot(a, b):
    return jnp.dot(a, b, preferred_element_type=F32)


def _mixer_kernel(*refs, tm, lc, alpha, fuse_in):
    refs = list(refs)
    if fuse_in:
        xprev_ref, yg_ref, metain_ref, g2p_ref, b2p_ref = refs[:5]
        refs = refs[5:]
    else:
        xprev_ref = refs.pop(0)
        xslab = refs.pop()
    (hist0_ref, h0_ref, k_ref, v_ref, perm_ref, permt_ref,
     w_in_ref, wdw_ref, bdw_ref, clg_ref, clb_ref, a1_ref, a2_ref,
     bre_ref, bim_ref, cre_ref, cim_ref, d_ref, wglu_ref, bglu_ref,
     wout_ref, g1_ref, b1_ref, wr_ref, br_ref, tri_ref, cnt0_ref,
     x1_ref, hist_out_ref, h_out_ref, xp_ref, meta_ref, metat_ref, cnt_ref,
     xpad0, xpad4, cy, ush, yim, bu_re, bu_im, hre, him, hcar, xin, running) = refs
    i = pl.program_id(1)

    @pl.when(i == 0)
    def _():
        xpad0[0:HIST_ROWS, :] = hist0_ref[0]
        hcar[...] = h0_ref[0]
        ush[...] = jnp.zeros_like(ush)

    @pl.when((i == 0) & (pl.program_id(0) == 0))
    def _():
        running[...] = cnt0_ref[...]

    if fuse_in:
        xin[...] = _moe_combine(xprev_ref[0], yg_ref, metain_ref[...], g2p_ref[...], b2p_ref[...], alpha)
    else:
        nslab = D_MODEL // LANES
        for s in range(SEQS_PER_STREAM):
            for j in range(nslab):
                xslab[j, pl.ds(s, lc, stride=SEQS_PER_STREAM), :] = xprev_ref[s, :, j * LANES:(j + 1) * LANES]
        xin[...] = jnp.concatenate([xslab[j] for j in range(nslab)], axis=1)
    x = xin[...]
    proj = _dot(x.astype(BF16), w_in_ref[...])

    g = proj[:, 0:D_CONV] * _sigmoid(proj[:, D_CONV:2 * D_CONV])
    xpad0[HIST_ROWS:HIST_ROWS + tm, :] = g
    xpad4[0:HIST_ROWS + tm - 4, :] = xpad0[4:HIST_ROWS + tm, :]

    nsub = CONV_ROWS // SUBLANES

    def conv_rows(rb, carry):
        r0 = pl.multiple_of(rb * CONV_ROWS, CONV_ROWS)
        accs = [jnp.broadcast_to(bdw_ref[...], (SUBLANES, D_CONV)) for _ in range(nsub)]
        for k in range(CONV_WIDTH):
            wk = wdw_ref[SUBLANES * k:SUBLANES * (k + 1), :]
            for sb in range(nsub):
                off = r0 + SEQS_PER_STREAM * k + SUBLANES * sb
                if k % 2 == 0:
                    xs = xpad0[pl.ds(pl.multiple_of(off, SUBLANES), SUBLANES), :]
                else:
                    xs = xpad4[pl.ds(pl.multiple_of(off - 4, SUBLANES), SUBLANES), :]
                accs[sb] = accs[sb] + xs * wk
        for sb in range(nsub):
            cy[pl.ds(pl.multiple_of(r0 + SUBLANES * sb, SUBLANES), SUBLANES), :] = accs[sb]
        return carry

    lax.fori_loop(0, tm // CONV_ROWS, conv_rows, 0, unroll=True)
    conv_n = _layer_norm(cy[...], clg_ref[...], clb_ref[...])
    cy[...] = conv_n * _sigmoid(conv_n)

    new_hist = xpad0[tm:tm + HIST_ROWS, :]
    xpad0[0:HIST_ROWS, :] = new_hist
    hist_out_ref[0] = new_hist

    u = proj[:, 2 * D_CONV:2 * D_CONV + D_SSM]
    ush[4:tm + 4, :] = u
    ub = u.astype(BF16)
    ub_sh = ush[...].astype(BF16)
    for m in range(SSM_BLOCKS):
        ch = slice(m * LANES, (m + 1) * LANES)
        st = slice(m * SCAN_LANES, (m + 1) * SCAN_LANES)
        bu_re[:, st] = _dot(ub[:, ch], bre_ref[m])
        bu_im[:, st] = _dot(ub_sh[:, ch], bim_ref[m])

    lo = lax.broadcasted_iota(jnp.int32, (SUBLANES, SCAN_LANES), 0) < 4
    for c in range(D_STATE // SCAN_LANES):
        cs = slice(c * SCAN_LANES, (c + 1) * SCAN_LANES)
        a1 = a1_ref[:, cs]
        a2 = a2_ref[:, cs]

        def scan_pair(j, carry, cs=cs, a1=a1, a2=a2):
            h_prev, im_cur = carry
            r = pl.multiple_of(j * SUBLANES, SUBLANES)
            re_cur = bu_re[pl.ds(r, SUBLANES), cs]
            im_next = bu_im[pl.ds(r + SUBLANES, SUBLANES), cs]
            p_even = jnp.where(lo, re_cur, im_cur)
            p_odd = jnp.where(lo, im_next, re_cur)
            h_even = a1 * pltpu.roll(h_prev, 4, 0) + a2 * h_prev + p_even
            h_odd = a1 * pltpu.roll(h_even, 4, 0) - a2 * h_even + p_odd
            hre[pl.ds(r, SUBLANES), cs] = jnp.where(lo, h_even, h_odd)
            him[pl.ds(r, SUBLANES), cs] = jnp.where(lo, h_prev, h_even)
            return h_odd, im_next

        h_last, _ = lax.fori_loop(0, lc // 2, scan_pair,
                                  (hcar[:, cs], bu_im[0:SUBLANES, cs]), unroll=True)
        him[tm:tm + SUBLANES, cs] = jnp.where(lo, h_last, 0.0)
        hcar[:, cs] = h_last
    h_out_ref[0] = hcar[...]

    y_re_blocks = []
    for m in range(SSM_BLOCKS):
        ch = slice(m * LANES, (m + 1) * LANES)
        st = slice(m * SCAN_LANES, (m + 1) * SCAN_LANES)
        y_re_blocks.append(_dot(hre[:, st].astype(BF16), cre_ref[m]))
        yim[:, ch] = _dot(him[:, st].astype(BF16), cim_ref[m])
    y_re = jnp.concatenate(y_re_blocks, axis=1)
    y = y_re + yim[4:tm + 4, :] + d_ref[...] * u
    y = _gelu_tanh(y)
    ssm_y = y * _sigmoid(_dot(y.astype(BF16), wglu_ref[...]) + bglu_ref[...])

    q = proj[:, 2 * D_CONV + D_SSM:D_IN].astype(BF16)
    q_seq = _dot(perm_ref[...], q)
    head_of_col = lax.shift_right_logical(lax.broadcasted_iota(jnp.int32, (1, D_ATT), 1), 6)
    head_masks = [jnp.where(head_of_col == h, 1.0, 0.0) for h in range(N_MEM_HEADS)]
    outs = []
    for s in range(SEQS_PER_STREAM):
        qs = q_seq[s * lc:(s + 1) * lc, :]
        ks = k_ref[0, s]
        vs = v_ref[0, s]
        acc = jnp.zeros((lc, D_ATT), F32)
        for h in range(N_MEM_HEADS):
            qh = (qs * head_masks[h]).astype(BF16)
            sc = _dot(qh, ks) * (MEM_HEAD_DIM ** -0.5)
            sc = sc - jnp.max(sc, axis=-1, keepdims=True)
            e = jnp.exp(sc)
            p = e * (1.0 / jnp.sum(e, axis=-1, keepdims=True))
            acc = acc + _dot(p.astype(BF16), vs) * head_masks[h]
        outs.append(acc)
    att_seq = jnp.concatenate(outs, axis=0).astype(BF16)
    att = _dot(permt_ref[...], att_seq).astype(BF16)

    mix = _dot(jnp.concatenate([cy[...].astype(BF16), ssm_y.astype(BF16), att], axis=1), wout_ref[...])
    x1 = _layer_norm(alpha * xin[...] + mix, g1_ref[...], b1_ref[...])
    x1_ref[0] = x1
    _route_rows(x1, wr_ref, br_ref, tri_ref, xp_ref, meta_ref, metat_ref, cnt_ref, running)


def _mixer_call(x, prev_moe, hist0, h0, kpad, vpad, perm, permt, lw, wr, br, cnt0, *, lc, alpha,
                tile_off=0, n_tiles=None):
    fuse_in = prev_moe is not None
    tm = lc * SEQS_PER_STREAM
    if fuse_in:
        nq, rows, _ = x.shape
    else:
        nq = x.shape[0] // SEQS_PER_STREAM
        rows = (x.shape[1] // lc if n_tiles is None else n_tiles) * tm
    nt = rows // tm
    kern = functools.partial(_mixer_kernel, tm=tm, lc=lc, alpha=alpha, fuse_in=fuse_in)
    triu = jnp.asarray(np.triu(np.ones((tm, tm), np.float32), 1), BF16)

    def const(shape):
        return pl.BlockSpec(shape, lambda q, i: (0,) * len(shape))

    def flat(shape):
        return pl.BlockSpec(shape, lambda q, i: (0,) * (len(shape) - 2) + (q * nt + i, 0))

    operands = [x]
    if fuse_in:
        in_specs = [pl.BlockSpec((1, tm, D_MODEL), lambda q, i: (q, i, 0)),
                    flat((2, tm, D_MODEL // 2)), flat((tm, META_COLS)),
                    const((1, D_MODEL)), const((1, D_MODEL))]
        operands += list(prev_moe)
    else:
        in_specs = [pl.BlockSpec((SEQS_PER_STREAM, lc, D_MODEL), lambda q, i: (q, i + tile_off, 0))]
    in_specs += [
        pl.BlockSpec((1, HIST_ROWS, D_CONV), lambda q, i: (q, 0, 0)),
        pl.BlockSpec((1, SUBLANES, D_STATE), lambda q, i: (q, 0, 0)),
        pl.BlockSpec((1, SEQS_PER_STREAM, N_MEM, D_ATT), lambda q, i: (q, 0, 0, 0)),
        pl.BlockSpec((1, SEQS_PER_STREAM, N_MEM, D_ATT), lambda q, i: (q, 0, 0, 0)),
        const((tm, tm)), const((tm, tm)),
        const((D_MODEL, D_IN)),
        const((CONV_WIDTH * SUBLANES, D_CONV)), const((1, D_CONV)), const((1, D_CONV)), const((1, D_CONV)),
        const((SUBLANES, D_STATE)), const((SUBLANES, D_STATE)),
        const((SSM_BLOCKS, LANES, SCAN_LANES)), const((SSM_BLOCKS, LANES, SCAN_LANES)),
        const((SSM_BLOCKS, SCAN_LANES, LANES)), const((SSM_BLOCKS, SCAN_LANES, LANES)),
        const((1, D_SSM)), const((D_SSM, D_SSM)), const((1, D_SSM)),
        const((D_MODEL, D_MODEL)), const((1, D_MODEL)), const((1, D_MODEL)),
        const((N_EXPERTS, D_MODEL)), const((N_EXPERTS, 1)), const((tm, tm)), const((N_EXPERTS, 1)),
    ]
    out_specs = [
        pl.BlockSpec((1, tm, D_MODEL), lambda q, i: (q, i, 0)),
        pl.BlockSpec((1, HIST_ROWS, D_CONV), lambda q, i: (q, 0, 0)),
        pl.BlockSpec((1, SUBLANES, D_STATE), lambda q, i: (q, 0, 0)),
        flat((tm, D_MODEL // 2)), flat((tm, META_COLS)),
        pl.BlockSpec((1, META_COLS, tm), lambda q, i: (q * nt + i, 0, 0)), const((N_EXPERTS, 1)),
    ]
    out_shape = [
        jax.ShapeDtypeStruct((nq, rows, D_MODEL), F32),
        jax.ShapeDtypeStruct((nq, HIST_ROWS, D_CONV), F32),
        jax.ShapeDtypeStruct((nq, SUBLANES, D_STATE), F32),
        jax.ShapeDtypeStruct((nq * rows, D_MODEL // 2), jnp.int32),
        jax.ShapeDtypeStruct((nq * rows, META_COLS), F32),
        jax.ShapeDtypeStruct((nq * nt, META_COLS, tm), F32),
        jax.ShapeDtypeStruct((N_EXPERTS, 1), F32),
    ]
    scratch = [
        pltpu.VMEM((HIST_ROWS + tm + SUBLANES, D_CONV), F32),
        pltpu.VMEM((HIST_ROWS + tm + SUBLANES, D_CONV), F32),
        pltpu.VMEM((tm, D_CONV), F32),
        pltpu.VMEM((tm + SUBLANES, D_SSM), F32),
        pltpu.VMEM((tm + SUBLANES, D_SSM), F32),
        pltpu.VMEM((tm, D_STATE), F32),
        pltpu.VMEM((tm + SUBLANES, D_STATE), F32),
        pltpu.VMEM((tm, D_STATE), F32),
        pltpu.VMEM((tm + SUBLANES, D_STATE), F32),
        pltpu.VMEM((SUBLANES, D_STATE), F32),
        pltpu.VMEM((tm, D_MODEL), F32),
        pltpu.VMEM((N_EXPERTS, 1), F32),
    ]
    if not fuse_in:
        scratch.append(pltpu.VMEM((D_MODEL // LANES, tm, LANES), F32))
    return pl.pallas_call(
        kern,
        grid=(nq, nt),
        in_specs=in_specs,
        out_specs=out_specs,
        out_shape=out_shape,
        scratch_shapes=scratch,
        compiler_params=pltpu.CompilerParams(
            dimension_semantics=("arbitrary", "arbitrary"),
            vmem_limit_bytes=VMEM_LIMIT_BYTES),
        name="mixer",
    )(*operands, hist0, h0, kpad, vpad, perm, permt,
      lw["w_in"], lw["wdw"], lw["bdw"], lw["clg"], lw["clb"], lw["a1"], lw["a2"],
      lw["bre"], lw["bim"], lw["cre"], lw["cim"], lw["d"], lw["wglu"], lw["bglu"],
      lw["w_out"], lw["g1"], lw["b1"], wr, br, triu, cnt0)


def _route(logits_t):
    m = jnp.max(logits_t, axis=0, keepdims=True)
    e = jnp.exp(logits_t - m)
    aff = e / jnp.sum(e, axis=0, keepdims=True)
    rows = [aff[j:j + 1, :] for j in range(N_EXPERTS)]

    scores = []
    for gi in range(N_EXPERT_GROUPS):
        a, b, c, d = rows[EXPERTS_PER_GROUP * gi:EXPERTS_PER_GROUP * (gi + 1)]
        hi1, lo1 = jnp.maximum(a, b), jnp.minimum(a, b)
        hi2, lo2 = jnp.maximum(c, d), jnp.minimum(c, d)
        scores.append(jnp.maximum(hi1, hi2) + jnp.maximum(jnp.minimum(hi1, hi2), jnp.maximum(lo1, lo2)))
    best = scores[0]
    sel = jnp.zeros_like(best)
    for gi in range(1, N_EXPERT_GROUPS):
        better = scores[gi] > best
        sel = jnp.where(better, float(gi), sel)
        best = jnp.where(better, scores[gi], best)

    hot1, hot2 = [], []
    for gi in range(N_EXPERT_GROUPS):
        vals = rows[EXPERTS_PER_GROUP * gi:EXPERTS_PER_GROUP * (gi + 1)]
        chosen = sel == float(gi)
        for j in range(EXPERTS_PER_GROUP):
            ahead = jnp.zeros_like(best)
            for k in range(EXPERTS_PER_GROUP):
                if k < j:
                    ahead = ahead + jnp.where(vals[k] >= vals[j], 1.0, 0.0)
                elif k > j:
                    ahead = ahead + jnp.where(vals[k] > vals[j], 1.0, 0.0)
            hot1.append(jnp.where(chosen, jnp.where(ahead == 0.0, 1.0, 0.0), 0.0))
            hot2.append(jnp.where(chosen, jnp.where(ahead == 1.0, 1.0, 0.0), 0.0))
    return jnp.concatenate(hot1, axis=0), jnp.concatenate(hot2, axis=0), aff


def _pack_halves(y):
    half = y.shape[1] // 2
    lo = lax.bitcast_convert_type(y[:, :half].astype(BF16).astype(F32), jnp.int32)
    hi = lax.bitcast_convert_type(y[:, half:].astype(BF16).astype(F32), jnp.int32)
    return lax.shift_right_logical(lo, 16) | (hi & HI_HALF_MASK)


def _unpack_halves(w):
    lo = lax.bitcast_convert_type(lax.shift_left(w, 16), F32)
    hi = lax.bitcast_convert_type(w & HI_HALF_MASK, F32)
    return lo, hi


def _route_rows(x, wr_ref, br_ref, triu_ref, xp_ref, meta_ref, metat_ref, cnt_ref, running):
    tm = x.shape[0]
    xp_ref[...] = _pack_halves(x)
    logits_t = lax.dot_general(wr_ref[...], x.astype(BF16), (((1,), (1,)), ((), ())),
                               preferred_element_type=F32) + br_ref[...]
    hot1, hot2, aff = _route(logits_t)
    eid = lax.broadcasted_iota(jnp.int32, (N_EXPERTS, tm), 0).astype(F32)
    both = hot1 + hot2
    before = _dot(both.astype(BF16), triu_ref[...]) + running[...]

    def pick(hot, vals):
        return jnp.sum(hot * vals, axis=0, keepdims=True)

    v1 = pick(hot1, aff)
    v2 = pick(hot2, aff)
    denom = v1 + v2
    meta_t = jnp.concatenate(
        [pick(hot1, eid), pick(hot2, eid), v1 / denom, v2 / denom, pick(hot1, before), pick(hot2, before),
         jnp.zeros((LANES - 6, tm), F32)], axis=0)
    metat_ref[0] = meta_t[0:META_COLS, :]
    meta_ref[...] = meta_t.T[:, 0:META_COLS]
    running[...] = running[...] + jnp.sum(both, axis=1, keepdims=True)
    cnt_ref[...] = running[...]


def _expert_kernel(te_ref, nxt_ref, first_ref, nt_ref, xs_ref, wup_hbm, bup_ref, wdn_hbm, ys_ref,
                   wup_f32, wdn_f32, wup_bf, wdn_bf, sems, slot_ref, *, layer):
    i = pl.program_id(0)

    def weight_copies(e, slot):
        return (pltpu.make_async_copy(wup_hbm.at[layer, e], wup_f32.at[slot], sems.at[0, slot]),
                pltpu.make_async_copy(wdn_hbm.at[layer, e], wdn_f32.at[slot], sems.at[1, slot]))

    @pl.when(i == 0)
    def _():
        slot_ref[0] = 0
        for c in weight_copies(te_ref[0], 0):
            c.start()

    @pl.when(first_ref[i] == 1)
    def _():
        slot = slot_ref[0]
        for c in weight_copies(te_ref[i], slot):
            c.wait()
        wup_bf[...] = wup_f32[slot].astype(BF16)
        wdn_bf[...] = wdn_f32[slot].astype(BF16)

        @pl.when(nxt_ref[i] >= 0)
        def _():
            for c in weight_copies(nxt_ref[i], 1 - slot):
                c.start()

        slot_ref[0] = 1 - slot

    @pl.when(i < nt_ref[0])
    def _():
        half = D_MODEL // 2
        rt = xs_ref.shape[0]
        sub = min(EXPERT_SUB_ROWS, rt)
        for r in range(rt // sub):
            rows = slice(r * sub, (r + 1) * sub)
            lo, hi = _unpack_halves(xs_ref[rows, :])
            h = (_dot(lo.astype(BF16), wup_bf[0:half, :])
                 + _dot(hi.astype(BF16), wup_bf[half:D_MODEL, :]) + bup_ref[0])
            ys_ref[rows, :] = _pack_halves(_dot(_gelu_tanh(h).astype(BF16), wdn_bf[...]))


def _expert_call(xs, tile_expert, next_expert, first_tile, n_tiles, lw, *, rt):
    rows = xs.shape[0]

    def row_map(i, te, nxt, first, nt):
        return (jnp.minimum(i, nt[0] - 1), 0)

    return pl.pallas_call(
        functools.partial(_expert_kernel, layer=lw["layer"]),
        grid_spec=pltpu.PrefetchScalarGridSpec(
            num_scalar_prefetch=4,
            grid=(rows // rt,),
            in_specs=[
                pl.BlockSpec((rt, D_MODEL // 2), row_map),
                pl.BlockSpec(memory_space=pl.ANY),
                pl.BlockSpec((1, 1, D_EXPERT), lambda i, te, nxt, first, nt: (te[i], 0, 0)),
                pl.BlockSpec(memory_space=pl.ANY),
            ],
            out_specs=pl.BlockSpec((rt, D_MODEL // 2), row_map),
            scratch_shapes=[pltpu.VMEM((2, D_MODEL, D_EXPERT), F32),
                            pltpu.VMEM((2, D_EXPERT, D_MODEL), F32),
                            pltpu.VMEM((D_MODEL, D_EXPERT), BF16),
                            pltpu.VMEM((D_EXPERT, D_MODEL), BF16),
                            pltpu.SemaphoreType.DMA((2, 2)),
                            pltpu.SMEM((1,), jnp.int32)],
        ),
        out_shape=jax.ShapeDtypeStruct((rows, D_MODEL // 2), jnp.int32),
        compiler_params=pltpu.CompilerParams(
            dimension_semantics=("arbitrary",), vmem_limit_bytes=VMEM_LIMIT_BYTES),
        name="experts",
    )(tile_expert, next_expert, first_tile, n_tiles, xs, lw["w_up"], lw["b_up"], lw["w_down"])


def _moe_combine(x1, yg_ref, meta, g, b, alpha):
    g1 = meta[:, 2:3]
    g2 = meta[:, 3:4]
    lo1, hi1 = _unpack_halves(yg_ref[0])
    lo2, hi2 = _unpack_halves(yg_ref[1])
    moe = jnp.concatenate([g1 * lo1 + g2 * lo2, g1 * hi1 + g2 * hi2], axis=1)
    return _layer_norm(alpha * x1 + moe, g, b)


def _combine_kernel(x_ref, yg_ref, meta_ref, g2_ref, b2_ref, *rest, alpha):
    o_ref, yslab = rest[-2:]
    y = _moe_combine(x_ref[...], yg_ref, meta_ref[...], g2_ref[...], b2_ref[...], alpha)
    nslab = D_MODEL // LANES
    steps = y.shape[0] // SEQS_PER_STREAM
    for j in range(nslab):
        yslab[j] = y[:, j * LANES:(j + 1) * LANES]
    for s in range(SEQS_PER_STREAM):
        for j in range(nslab):
            o_ref[s, :, j * LANES:(j + 1) * LANES] = yslab[j, pl.ds(s, steps, stride=SEQS_PER_STREAM), :]


def _combine_call(x1, yg, meta, lw, *, alpha, seq_len, step_off=0, prev_out=None):
    nq, srows, _ = x1.shape
    tmc = min(1024, srows)
    tps = srows // tmc
    steps = tmc // SEQS_PER_STREAM
    blk_off = step_off // steps
    in_specs = [
        pl.BlockSpec((tmc, D_MODEL), lambda i: (i, 0)),
        pl.BlockSpec((2, tmc, D_MODEL // 2), lambda i: (0, i, 0)),
        pl.BlockSpec((tmc, META_COLS), lambda i: (i, 0)),
        pl.BlockSpec((1, D_MODEL), lambda i: (0, 0)),
        pl.BlockSpec((1, D_MODEL), lambda i: (0, 0)),
    ]
    operands = [x1.reshape(nq * srows, D_MODEL), yg, meta, lw["g2"], lw["b2"]]
    aliases = {}
    if prev_out is not None:
        in_specs.append(pl.BlockSpec(memory_space=pl.ANY))
        operands.append(prev_out)
        aliases = {len(operands) - 1: 0}
    return pl.pallas_call(
        functools.partial(_combine_kernel, alpha=alpha),
        grid=(nq * tps,),
        in_specs=in_specs,
        out_specs=pl.BlockSpec((SEQS_PER_STREAM, steps, D_MODEL),
                               lambda i: (i // tps, i % tps + blk_off, 0)),
        out_shape=jax.ShapeDtypeStruct((nq * SEQS_PER_STREAM, seq_len, D_MODEL), F32),
        scratch_shapes=[pltpu.VMEM((D_MODEL // LANES, tmc, LANES), F32)],
        input_output_aliases=aliases,
        compiler_params=pltpu.CompilerParams(
            dimension_semantics=("arbitrary",), vmem_limit_bytes=VMEM_LIMIT_BYTES),
        name="combine",
    )(*operands)


def _sc_mesh():
    return plsc.VectorSubcoreMesh(core_axis_name="c", subcore_axis_name="s")


def _sc_chunk(rows):
    per_worker = rows // SC_WORKERS
    chunk = min(SC_MAX_CHUNK, per_worker)
    assert per_worker % chunk == 0 and chunk % 8 == 0
    return per_worker // chunk, chunk


def _sc_dispatch(xps, poss, n_slots):
    width = xps[0].shape[1]
    ngroups = len(xps)
    plans = [_sc_chunk(x.shape[0]) for x in xps]
    posrs = [p.reshape(2, SC_WORKERS, nch, ch) for p, (nch, ch) in zip(poss, plans)]
    scratch = []
    for _, ch in plans:
        scratch += [pltpu.VMEM((ch,), jnp.int32), pltpu.VMEM((ch,), jnp.int32),
                    pltpu.VMEM((ch, width), jnp.int32)]

    @functools.partial(
        pl.kernel, mesh=_sc_mesh(),
        out_type=jax.ShapeDtypeStruct((n_slots, width), jnp.int32),
        scratch_types=scratch + [pltpu.SemaphoreType.DMA],
        name="sc_dispatch")
    def k(*refs):
        x_hbms, pos_hbms, o_hbm = refs[:ngroups], refs[ngroups:2 * ngroups], refs[2 * ngroups]
        bufs, sem = refs[2 * ngroups + 1:-1], refs[-1]
        wid = lax.axis_index("s") * SC_CORES + lax.axis_index("c")
        for g, (nch, ch) in enumerate(plans):
            idx0, idx1, buf = bufs[3 * g:3 * g + 3]

            @pl.loop(0, nch)
            def _(c, g=g, nch=nch, ch=ch, idx0=idx0, idx1=idx1, buf=buf):
                base = (wid * nch + c) * ch
                pltpu.sync_copy(x_hbms[g].at[pl.ds(base, ch)], buf)
                pltpu.sync_copy(pos_hbms[g].at[0, wid, c], idx0)
                pltpu.sync_copy(pos_hbms[g].at[1, wid, c], idx1)
                pltpu.async_copy(buf, o_hbm.at[idx0], sem).wait()
                pltpu.async_copy(buf, o_hbm.at[idx1], sem).wait()

    return k(*xps, *posrs)


def _sc_gather(ys, poss):
    width = ys.shape[1]
    ngroups = len(poss)
    plans = [_sc_chunk(p.shape[1]) for p in poss]
    posrs = [p.reshape(2, SC_WORKERS, nch, ch) for p, (nch, ch) in zip(poss, plans)]
    scratch = []
    for _, ch in plans:
        scratch += [pltpu.VMEM((ch,), jnp.int32), pltpu.VMEM((ch, width), jnp.int32)]

    @functools.partial(
        pl.kernel, mesh=_sc_mesh(),
        out_type=[jax.ShapeDtypeStruct((2, p.shape[1], width), jnp.int32) for p in poss],
        scratch_types=scratch + [pltpu.SemaphoreType.DMA],
        name="sc_gather")
    def k(*refs):
        y_hbm, pos_hbms = refs[0], refs[1:1 + ngroups]
        o_hbms = refs[1 + ngroups:1 + 2 * ngroups]
        bufs, sem = refs[1 + 2 * ngroups:-1], refs[-1]
        wid = lax.axis_index("s") * SC_CORES + lax.axis_index("c")
        for g, (nch, ch) in enumerate(plans):
            idx, buf = bufs[2 * g:2 * g + 2]

            @pl.loop(0, nch)
            def _(c, g=g, nch=nch, ch=ch, idx=idx, buf=buf):
                base = (wid * nch + c) * ch
                for kk in range(2):
                    pltpu.sync_copy(pos_hbms[g].at[kk, wid, c], idx)
                    pltpu.async_copy(y_hbm.at[idx], buf, sem).wait()
                    pltpu.sync_copy(buf, o_hbms[g].at[kk, pl.ds(base, ch)])

    return k(ys, *posrs)


def _moe_rows(xps, meta_ts, counts, lw):
    total = sum(x.shape[0] for x in xps)
    rt = 512 if total >= 8192 else 128
    n_slots = 2 * total + N_EXPERTS * rt
    nt_max = n_slots // rt

    cnt = counts[:, 0].astype(jnp.int32)
    padded = ((cnt + rt - 1) // rt) * rt
    ends = jnp.cumsum(padded)
    offs = ends - padded
    experts = jnp.arange(N_EXPERTS, dtype=jnp.int32)[:, None]
    poss = []
    for x, meta_t in zip(xps, meta_ts):
        meta_t = meta_t.transpose(1, 0, 2).reshape(META_COLS, x.shape[0])
        eidx = meta_t[0:2].astype(jnp.int32)
        rank = meta_t[4:6].astype(jnp.int32)
        poss.append(jnp.sum(jnp.where(eidx[:, None, :] == experts, offs[:, None], 0), axis=1) + rank)
    n_tiles = ends[-1:] // rt
    tiles = jnp.minimum(jnp.arange(nt_max, dtype=jnp.int32), n_tiles[0] - 1)
    tile_expert = jnp.minimum(jnp.sum((ends // rt)[None, :] <= tiles[:, None], axis=1),
                              N_EXPERTS - 1).astype(jnp.int32)

    first_tile = jnp.concatenate([jnp.ones((1,), jnp.int32),
                                  (tile_expert[1:] != tile_expert[:-1]).astype(jnp.int32)])
    eid = jnp.arange(N_EXPERTS, dtype=jnp.int32)
    later = (eid[None, :] > eid[:, None]) & (cnt[None, :] > 0)
    next_of = jnp.min(jnp.where(later, eid[None, :], N_EXPERTS), axis=1)
    next_of = jnp.where(next_of == N_EXPERTS, -1, next_of)
    next_expert = jnp.sum(jnp.where(tile_expert[:, None] == eid[None, :], next_of[None, :], 0),
                          axis=1).astype(jnp.int32)

    xs = _sc_dispatch(xps, poss, n_slots)
    ys = _expert_call(xs, tile_expert, next_expert, first_tile, n_tiles.astype(jnp.int32), lw, rt=rt)
    return _sc_gather(ys, poss)


def _memkv_kernel(m_ref, wk_ref, wv_ref, k_ref, v_ref):
    mb = m_ref[...].astype(BF16)
    k_ref[0] = _dot(mb, wk_ref[0])
    v_ref[0] = _dot(mb, wv_ref[0])


def _memkv_call(mem, wk, wv):
    depth = wk.shape[0]
    rows = mem.shape[0]
    return pl.pallas_call(
        _memkv_kernel,
        grid=(depth,),
        in_specs=[
            pl.BlockSpec((rows, D_MODEL), lambda l: (0, 0)),
            pl.BlockSpec((1, D_MODEL, D_ATT), lambda l: (l, 0, 0)),
            pl.BlockSpec((1, D_MODEL, D_ATT), lambda l: (l, 0, 0)),
        ],
        out_specs=[
            pl.BlockSpec((1, rows, D_ATT), lambda l: (l, 0, 0)),
            pl.BlockSpec((1, rows, D_ATT), lambda l: (l, 0, 0)),
        ],
        out_shape=[jax.ShapeDtypeStruct((depth, rows, D_ATT), F32)] * 2,
        compiler_params=pltpu.CompilerParams(
            dimension_semantics=("arbitrary",), vmem_limit_bytes=VMEM_LIMIT_BYTES),
        name="memkv",
    )(mem, wk, wv)


def _perm_matrices(lc):
    tm = lc * SEQS_PER_STREAM
    p = np.zeros((tm, tm), np.float32)
    for s in range(SEQS_PER_STREAM):
        for t in range(lc):
            p[s * lc + t, t * SEQS_PER_STREAM + s] = 1.0
    return jnp.asarray(p, BF16), jnp.asarray(p.T, BF16)


def _to_streams(a):
    b, l, c = a.shape
    q = b // SEQS_PER_STREAM
    return a.reshape(q, SEQS_PER_STREAM, l, c).transpose(0, 2, 1, 3).reshape(q, l * SEQS_PER_STREAM, c)


def _from_streams(a, l):
    q, _, c = a.shape
    return a.reshape(q, l, SEQS_PER_STREAM, c).transpose(0, 2, 1, 3).reshape(q * SEQS_PER_STREAM, l, c)


def _pack_state(re, im):
    b = re.shape[0]
    q = b // SEQS_PER_STREAM
    re = re.reshape(q, SEQS_PER_STREAM, D_STATE)
    im = im.reshape(q, SEQS_PER_STREAM, D_STATE)
    return jnp.concatenate([im, re], axis=1)


def _unpack_state(h):
    q = h.shape[0]
    im = h[:, 0:SEQS_PER_STREAM].reshape(q * SEQS_PER_STREAM, N_SSM_GROUPS, SSM_STATE)
    re = h[:, SEQS_PER_STREAM:].reshape(q * SEQS_PER_STREAM, N_SSM_GROUPS, SSM_STATE)
    return re, im


def _pad_heads(mk, mv):
    b = mk.shape[0]
    q = b // SEQS_PER_STREAM
    shape = (q, SEQS_PER_STREAM, N_MEM, D_ATT)
    return mk.reshape(shape).transpose(0, 1, 3, 2).astype(BF16), mv.reshape(shape).astype(BF16)


def _layer_params(l, w_in, w_dw, b_dw, conv_ln_g, conv_ln_b, ssm_a_re, ssm_a_im, ssm_b_re,
                  ssm_b_im, ssm_c_re, ssm_c_im, ssm_d, ssm_log_dt, ssm_w_glu, ssm_b_glu,
                  w_out, ln1_g, ln1_b, w_up, b_up, w_down, ln2_g, ln2_b):
    g, p = N_SSM_GROUPS, SSM_STATE
    a = lax.complex(ssm_a_re[l], ssm_a_im[l])
    dt = jnp.exp(ssm_log_dt[l])[:, None]
    a_bar = jnp.exp(a * dt)
    b_bar = ((a_bar - 1.0) / a)[..., None] * lax.complex(ssm_b_re[l], ssm_b_im[l])
    eye = jnp.eye(g, dtype=F32)

    def b_block(m):
        full = jnp.einsum("gpi,gh->gihp", m, eye).reshape(D_SSM, D_STATE).astype(BF16)
        return jnp.stack([full[k * LANES:(k + 1) * LANES, k * SCAN_LANES:(k + 1) * SCAN_LANES]
                          for k in range(SSM_BLOCKS)])

    def c_block(m):
        full = jnp.einsum("gip,gh->gphi", m, eye).reshape(D_STATE, D_SSM).astype(BF16)
        return jnp.stack([full[k * SCAN_LANES:(k + 1) * SCAN_LANES, k * LANES:(k + 1) * LANES]
                          for k in range(SSM_BLOCKS)])

    ar = jnp.real(a_bar).reshape(1, D_STATE)
    ai = jnp.imag(a_bar).reshape(1, D_STATE)
    half = SUBLANES // 2
    return {
        "w_in": w_in[l].astype(BF16),
        "wdw": jnp.repeat(w_dw[l], SUBLANES, axis=0),
        "bdw": b_dw[l][None], "clg": conv_ln_g[l][None], "clb": conv_ln_b[l][None],
        "a1": jnp.broadcast_to(ar, (SUBLANES, D_STATE)),
        "a2": jnp.concatenate([jnp.broadcast_to(-ai, (half, D_STATE)),
                               jnp.broadcast_to(ai, (half, D_STATE))], axis=0),
        "bre": b_block(jnp.real(b_bar)), "bim": b_block(jnp.imag(b_bar)),
        "cre": c_block(ssm_c_re[l]), "cim": c_block(-ssm_c_im[l]),
        "d": ssm_d[l][None], "wglu": ssm_w_glu[l].astype(BF16), "bglu": ssm_b_glu[l][None],
        "w_out": w_out[l].astype(BF16), "g1": ln1_g[l][None], "b1": ln1_b[l][None],
        "w_up": w_up, "b_up": b_up[l][:, None, :], "w_down": w_down, "layer": l,
        "g2": ln2_g[l][None], "b2": ln2_b[l][None],
    }


def kernel(x_prompt, x_sample, cache_conv, state_ssm_re, state_ssm_im, cache_mem_k, cache_mem_v, mem_prompt, w_in, w_dw, b_dw, conv_ln_g, conv_ln_b, ssm_a_re, ssm_a_im, ssm_b_re, ssm_b_im, ssm_c_re, ssm_c_im, ssm_d, ssm_log_dt, ssm_w_glu, ssm_b_glu, w_mem_k, w_mem_v, w_out, ln1_g, ln1_b, w_router, b_router, w_up, b_up, w_down, ln2_g, ln2_b):
    depth = w_in.shape[0]
    alpha = (2.0 * depth) ** 0.25
    bp, seq, _ = x_prompt.shape
    bs, dec_seq, _ = x_sample.shape
    assert bp == SEQS_PER_STREAM and bs % SEQS_PER_STREAM == 0
    lc_p = min(128, seq)
    lc_s = dec_seq
    assert seq % lc_p == 0 and lc_p % 16 == 0 and lc_s % 16 == 0

    mb, mm, _ = mem_prompt.shape
    mk_all, mv_all = _memkv_call(mem_prompt.reshape(mb * mm, D_MODEL),
                                 w_mem_k.astype(BF16), w_mem_v.astype(BF16))
    mk_all = mk_all.reshape(depth, mb, mm, N_MEM_HEADS, MEM_HEAD_DIM)
    mv_all = mv_all.reshape(depth, mb, mm, N_MEM_HEADS, MEM_HEAD_DIM)

    perm_p, permt_p = _perm_matrices(lc_p)
    perm_s, permt_s = _perm_matrices(lc_s)
    wr = w_router.T.astype(BF16)
    br = b_router[:, None]

    tiles_p = seq // lc_p
    tiles_a = max(tiles_p // 2, 1)
    tiles_b = tiles_p - tiles_a
    steps_a = tiles_a * lc_p
    xa = xb = x_prompt
    xs = x_sample
    zero_hist = jnp.zeros((1, HIST_ROWS, D_CONV), F32)
    zero_h = jnp.zeros((1, SUBLANES, D_STATE), F32)
    zero_cnt = jnp.zeros((N_EXPERTS, 1), F32)

    conv_p, re_p, im_p, conv_s, re_s, im_s = [], [], [], [], [], []
    moe_a = moe_b = moe_s = None
    for l in range(depth):
        lw = _layer_params(l, w_in, w_dw, b_dw, conv_ln_g, conv_ln_b, ssm_a_re, ssm_a_im,
                           ssm_b_re, ssm_b_im, ssm_c_re, ssm_c_im, ssm_d, ssm_log_dt, ssm_w_glu,
                           ssm_b_glu, w_out, ln1_g, ln1_b, w_up, b_up, w_down, ln2_g, ln2_b)
        kp, vp = _pad_heads(mk_all[l], mv_all[l])
        xa, hist, hst, rows_a, meta_a, metat_a, cnt_a = _mixer_call(
            xa, moe_a, zero_hist, zero_h, kp, vp, perm_p, permt_p, lw, wr, br, zero_cnt,
            lc=lc_p, alpha=alpha, tile_off=0, n_tiles=tiles_a)
        (yg_a,) = _moe_rows([rows_a], [metat_a], cnt_a, lw)
        moe_a = (yg_a, meta_a, lw["g2"], lw["b2"])
        xb, hist, hst, rows_b, meta_b, metat_b, cnt_b = _mixer_call(
            xb, moe_b, hist, hst, kp, vp, perm_p, permt_p, lw, wr, br, zero_cnt,
            lc=lc_p, alpha=alpha, tile_off=tiles_a, n_tiles=tiles_b)
        conv_p.append(_from_streams(hist, CONV_BUF))
        hr, hi = _unpack_state(hst)
        re_p.append(hr)
        im_p.append(hi)
        kp, vp = _pad_heads(cache_mem_k[l], cache_mem_v[l])
        xs, hist, hst, rows_s, meta_s, metat_s, cnt_s = _mixer_call(
            xs, moe_s, _to_streams(cache_conv[l]), _pack_state(state_ssm_re[l], state_ssm_im[l]),
            kp, vp, perm_s, permt_s, lw, wr, br, cnt_b, lc=lc_s, alpha=alpha)
        yg_b, yg_s = _moe_rows([rows_b, rows_s], [metat_b, metat_s], cnt_s, lw)
        moe_b = (yg_b, meta_b, lw["g2"], lw["b2"])
        moe_s = (yg_s, meta_s, lw["g2"], lw["b2"])
        conv_s.append(_from_streams(hist, CONV_BUF))
        hr, hi = _unpack_state(hst)
        re_s.append(hr)
        im_s.append(hi)

    y_prompt = _combine_call(xa, moe_a[0], moe_a[1], lw, alpha=alpha, seq_len=seq)
    y_prompt = _combine_call(xb, moe_b[0], moe_b[1], lw, alpha=alpha, seq_len=seq, step_off=steps_a,
                             prev_out=y_prompt)
    y_sample = _combine_call(xs, moe_s[0], moe_s[1], lw, alpha=alpha, seq_len=dec_seq)

    return (y_prompt, y_sample,
            jnp.stack(conv_p), jnp.stack(re_p), jnp.stack(im_p), mk_all, mv_all,
            jnp.stack(conv_s), jnp.stack(re_s), jnp.stack(im_s))
```

```python
import functools
import math

import numpy as np
import jax
import jax.numpy as jnp
from jax import lax
from jax.experimental import pallas as pl
from jax.experimental.pallas import tpu as pltpu
from jax.experimental.pallas import tpu_sc as plsc

F32 = jnp.float32
BF16 = jnp.bfloat16

D_MODEL = 1024
CONV_WIDTH = 31
CONV_BUF = CONV_WIDTH - 1
D_CONV = 384
D_SSM = 384
SSM_GROUP = 16
N_SSM_GROUPS = D_SSM // SSM_GROUP
SSM_STATE = 64
D_STATE = N_SSM_GROUPS * SSM_STATE
N_MEM = 256
N_MEM_HEADS = 4
MEM_HEAD_DIM = 64
D_ATT = N_MEM_HEADS * MEM_HEAD_DIM
D_IN = 2 * D_CONV + D_SSM + D_ATT
N_EXPERTS = 16
N_EXPERT_GROUPS = 4
EXPERTS_PER_GROUP = N_EXPERTS // N_EXPERT_GROUPS
D_EXPERT = 512
LN_EPS = 1e-5

SEQS_PER_STREAM = 4
HIST_ROWS = CONV_BUF * SEQS_PER_STREAM
SUBLANES = 8
LANES = 128
SCAN_LANES = 512
SSM_BLOCKS = D_STATE // SCAN_LANES
CONV_ROWS = 32
EXPERT_SUB_ROWS = 512
VMEM_LIMIT_BYTES = 56 * 1024 * 1024
HI_HALF_MASK = np.int32(-65536)
META_COLS = 8
SC_CORES = 2
SC_WORKERS = SC_CORES * 16
SC_MAX_CHUNK = 128


def _sigmoid(x):
    return 1.0 / (1.0 + jnp.exp(-x))


def _gelu_tanh(x):
    c = math.sqrt(2.0 / math.pi)
    return 0.5 * x * (1.0 + jnp.tanh(c * (x + 0.044715 * (x * x * x))))


def _layer_norm(z, g, b):
    mu = jnp.mean(z, axis=-1, keepdims=True)
    zc = z - mu
    var = jnp.mean(zc * zc, axis=-1, keepdims=True)
    return zc * lax.rsqrt(var + LN_EPS) * g + b


def _dot(a, b):
    return jnp.dot(a, b, preferred_element_type=F32)


def _mixer_kernel(*refs, tm, lc, alpha, fuse_in):
    refs = list(refs)
    if fuse_in:
        xprev_ref, yg_ref, metain_ref, g2p_ref, b2p_ref = refs[:5]
        refs = refs[5:]
    else:
        xprev_ref = refs.pop(0)
        xslab = refs.pop()
    (hist0_ref, h0_ref, k_ref, v_ref, perm_ref, permt_ref,
     w_in_ref, wdw_ref, bdw_ref, clg_ref, clb_ref, a1_ref, a2_ref,
     bre_ref, bim_ref, cre_ref, cim_ref, d_ref, wglu_ref, bglu_ref,
     wout_ref, g1_ref, b1_ref, wr_ref, br_ref, tri_ref, cnt0_ref,
     x1_ref, hist_out_ref, h_out_ref, xp_ref, meta_ref, metat_ref, cnt_ref,
     xpad0, xpad4, cy, ush, yim, bu_re, bu_im, hre, him, hcar, xin, running) = refs
    i = pl.program_id(1)

    @pl.when(i == 0)
    def _():
        xpad0[0:HIST_ROWS, :] = hist0_ref[0]
        hcar[...] = h0_ref[0]
        ush[...] = jnp.zeros_like(ush)

    @pl.when((i == 0) & (pl.program_id(0) == 0))
    def _():
        running[...] = cnt0_ref[...]

    if fuse_in:
        xin[...] = _moe_combine(xprev_ref[0], yg_ref, metain_ref[...], g2p_ref[...], b2p_ref[...], alpha)
    else:
        nslab = D_MODEL // LANES
        for s in range(SEQS_PER_STREAM):
            for j in range(nslab):
                xslab[j, pl.ds(s, lc, stride=SEQS_PER_STREAM), :] = xprev_ref[s, :, j * LANES:(j + 1) * LANES]
        xin[...] = jnp.concatenate([xslab[j] for j in range(nslab)], axis=1)
    x = xin[...]
    proj = _dot(x.astype(BF16), w_in_ref[...])

    g = proj[:, 0:D_CONV] * _sigmoid(proj[:, D_CONV:2 * D_CONV])
    xpad0[HIST_ROWS:HIST_ROWS + tm, :] = g
    xpad4[0:HIST_ROWS + tm - 4, :] = xpad0[4:HIST_ROWS + tm, :]

    nsub = CONV_ROWS // SUBLANES

    def conv_rows(rb, carry):
        r0 = pl.multiple_of(rb * CONV_ROWS, CONV_ROWS)
        accs = [jnp.broadcast_to(bdw_ref[...], (SUBLANES, D_CONV)) for _ in range(nsub)]
        for k in range(CONV_WIDTH):
            wk = wdw_ref[SUBLANES * k:SUBLANES * (k + 1), :]
            for sb in range(nsub):
                off = r0 + SEQS_PER_STREAM * k + SUBLANES * sb
                if k % 2 == 0:
                    xs = xpad0[pl.ds(pl.multiple_of(off, SUBLANES), SUBLANES), :]
                else:
                    xs = xpad4[pl.ds(pl.multiple_of(off - 4, SUBLANES), SUBLANES), :]
                accs[sb] = accs[sb] + xs * wk
        for sb in range(nsub):
            cy[pl.ds(pl.multiple_of(r0 + SUBLANES * sb, SUBLANES), SUBLANES), :] = accs[sb]
        return carry

    lax.fori_loop(0, tm // CONV_ROWS, conv_rows, 0, unroll=True)
    conv_n = _layer_norm(cy[...], clg_ref[...], clb_ref[...])
    cy[...] = conv_n * _sigmoid(conv_n)

    new_hist = xpad0[tm:tm + HIST_ROWS, :]
    xpad0[0:HIST_ROWS, :] = new_hist
    hist_out_ref[0] = new_hist

    u = proj[:, 2 * D_CONV:2 * D_CONV + D_SSM]
    ush[4:tm + 4, :] = u
    ub = u.astype(BF16)
    ub_sh = ush[...].astype(BF16)
    for m in range(SSM_BLOCKS):
        ch = slice(m * LANES, (m + 1) * LANES)
        st = slice(m * SCAN_LANES, (m + 1) * SCAN_LANES)
        bu_re[:, st] = _dot(ub[:, ch], bre_ref[m])
        bu_im[:, st] = _dot(ub_sh[:, ch], bim_ref[m])

    lo = lax.broadcasted_iota(jnp.int32, (SUBLANES, SCAN_LANES), 0) < 4
    for c in range(D_STATE // SCAN_LANES):
        cs = slice(c * SCAN_LANES, (c + 1) * SCAN_LANES)
        a1 = a1_ref[:, cs]
        a2 = a2_ref[:, cs]

        def scan_pair(j, carry, cs=cs, a1=a1, a2=a2):
            h_prev, im_cur = carry
            r = pl.multiple_of(j * SUBLANES, SUBLANES)
            re_cur = bu_re[pl.ds(r, SUBLANES), cs]
            im_next = bu_im[pl.ds(r + SUBLANES, SUBLANES), cs]
            p_even = jnp.where(lo, re_cur, im_cur)
            p_odd = jnp.where(lo, im_next, re_cur)
            h_even = a1 * pltpu.roll(h_prev, 4, 0) + a2 * h_prev + p_even
            h_odd = a1 * pltpu.roll(h_even, 4, 0) - a2 * h_even + p_odd
            hre[pl.ds(r, SUBLANES), cs] = jnp.where(lo, h_even, h_odd)
            him[pl.ds(r, SUBLANES), cs] = jnp.where(lo, h_prev, h_even)
            return h_odd, im_next

        h_last, _ = lax.fori_loop(0, lc // 2, scan_pair,
                                  (hcar[:, cs], bu_im[0:SUBLANES, cs]), unroll=True)
        him[tm:tm + SUBLANES, cs] = jnp.where(lo, h_last, 0.0)
        hcar[:, cs] = h_last
    h_out_ref[0] = hcar[...]

    y_re_blocks = []
    for m in range(SSM_BLOCKS):
        ch = slice(m * LANES, (m + 1) * LANES)
        st = slice(m * SCAN_LANES, (m + 1) * SCAN_LANES)
        y_re_blocks.append(_dot(hre[:, st].astype(BF16), cre_ref[m]))
        yim[:, ch] = _dot(him[:, st].astype(BF16), cim_ref[m])
    y_re = jnp.concatenate(y_re_blocks, axis=1)
    y = y_re + yim[4:tm + 4, :] + d_ref[...] * u
    y = _gelu_tanh(y)
    ssm_y = y * _sigmoid(_dot(y.astype(BF16), wglu_ref[...]) + bglu_ref[...])

    q = proj[:, 2 * D_CONV + D_SSM:D_IN].astype(BF16)
    q_seq = _dot(perm_ref[...], q)
    head_of_col = lax.shift_right_logical(lax.broadcasted_iota(jnp.int32, (1, D_ATT), 1), 6)
    head_masks = [jnp.where(head_of_col == h, 1.0, 0.0) for h in range(N_MEM_HEADS)]
    outs = []
    for s in range(SEQS_PER_STREAM):
        qs = q_seq[s * lc:(s + 1) * lc, :]
        ks = k_ref[0, s]
        vs = v_ref[0, s]
        acc = jnp.zeros((lc, D_ATT), F32)
        for h in range(N_MEM_HEADS):
            qh = (qs * head_masks[h]).astype(BF16)
            sc = _dot(qh, ks) * (MEM_HEAD_DIM ** -0.5)
            sc = sc - jnp.max(sc, axis=-1, keepdims=True)
            e = jnp.exp(sc)
            p = e * (1.0 / jnp.sum(e, axis=-1, keepdims=True))
            acc = acc + _dot(p.astype(BF16), vs) * head_masks[h]
        outs.append(acc)
    att_seq = jnp.concatenate(outs, axis=0).astype(BF16)
    att = _dot(permt_ref[...], att_seq).astype(BF16)

    mix = _dot(jnp.concatenate([cy[...].astype(BF16), ssm_y.astype(BF16), att], axis=1), wout_ref[...])
    x1 = _layer_norm(alpha * xin[...] + mix, g1_ref[...], b1_ref[...])
    x1_ref[0] = x1
    _route_rows(x1, wr_ref, br_ref, tri_ref, xp_ref, meta_ref, metat_ref, cnt_ref, running)


def _mixer_call(x, prev_moe, hist0, h0, kpad, vpad, perm, permt, lw, wr, br, cnt0, *, lc, alpha,
                tile_off=0, n_tiles=None, state_off=0, kv_off=0):
    fuse_in = prev_moe is not None
    tm = lc * SEQS_PER_STREAM
    if fuse_in:
        nq, rows, _ = x.shape
    else:
        nq = x.shape[0] // SEQS_PER_STREAM
        rows = (x.shape[1] // lc if n_tiles is None else n_tiles) * tm
    nt = rows // tm
    kern = functools.partial(_mixer_kernel, tm=tm, lc=lc, alpha=alpha, fuse_in=fuse_in)
    triu = jnp.asarray(np.triu(np.ones((tm, tm), np.float32), 1), BF16)

    def const(shape):
        return pl.BlockSpec(shape, lambda q, i: (0,) * len(shape))

    def flat(shape):
        return pl.BlockSpec(shape, lambda q, i: (0,) * (len(shape) - 2) + (q * nt + i, 0))

    layer = lw["layer"]

    def layered(shape, l=layer):
        return pl.BlockSpec((None,) + shape, lambda q, i: (l,) + (0,) * len(shape))

    operands = [x]
    if fuse_in:
        yg_prev, meta_prev = prev_moe
        in_specs = [pl.BlockSpec((1, tm, D_MODEL), lambda q, i: (q, i, 0)),
                    flat((2, tm, D_MODEL // 2)), flat((tm, META_COLS)),
                    layered((1, D_MODEL), layer - 1), layered((1, D_MODEL), layer - 1)]
        operands += [yg_prev, meta_prev, lw["g2"], lw["b2"]]
    else:
        in_specs = [pl.BlockSpec((SEQS_PER_STREAM, lc, D_MODEL), lambda q, i: (q, i + tile_off, 0))]
    in_specs += [
        pl.BlockSpec((1, HIST_ROWS, D_CONV), lambda q, i: (q + state_off, 0, 0)),
        pl.BlockSpec((1, SUBLANES, D_STATE), lambda q, i: (q + state_off, 0, 0)),
        pl.BlockSpec((1, SEQS_PER_STREAM, N_MEM, D_ATT), lambda q, i: (q + kv_off, 0, 0, 0)),
        pl.BlockSpec((1, SEQS_PER_STREAM, N_MEM, D_ATT), lambda q, i: (q + kv_off, 0, 0, 0)),
        const((tm, tm)), const((tm, tm)),
        layered((D_MODEL, D_IN)),
        layered((CONV_WIDTH * SUBLANES, D_CONV)), layered((1, D_CONV)), layered((1, D_CONV)),
        layered((1, D_CONV)),
        layered((SUBLANES, D_STATE)), layered((SUBLANES, D_STATE)),
        layered((SSM_BLOCKS, LANES, SCAN_LANES)), layered((SSM_BLOCKS, LANES, SCAN_LANES)),
        layered((SSM_BLOCKS, SCAN_LANES, LANES)), layered((SSM_BLOCKS, SCAN_LANES, LANES)),
        layered((1, D_SSM)), layered((D_SSM, D_SSM)), layered((1, D_SSM)),
        layered((D_MODEL, D_MODEL)), layered((1, D_MODEL)), layered((1, D_MODEL)),
        const((N_EXPERTS, D_MODEL)), const((N_EXPERTS, 1)), const((tm, tm)), const((N_EXPERTS, 1)),
    ]
    out_specs = [
        pl.BlockSpec((1, tm, D_MODEL), lambda q, i: (q, i, 0)),
        pl.BlockSpec((1, HIST_ROWS, D_CONV), lambda q, i: (q, 0, 0)),
        pl.BlockSpec((1, SUBLANES, D_STATE), lambda q, i: (q, 0, 0)),
        flat((tm, D_MODEL // 2)), flat((tm, META_COLS)),
        pl.BlockSpec((1, META_COLS, tm), lambda q, i: (q * nt + i, 0, 0)), const((N_EXPERTS, 1)),
    ]
    out_shape = [
        jax.ShapeDtypeStruct((nq, rows, D_MODEL), F32),
        jax.ShapeDtypeStruct((nq, HIST_ROWS, D_CONV), F32),
        jax.ShapeDtypeStruct((nq, SUBLANES, D_STATE), F32),
        jax.ShapeDtypeStruct((nq * rows, D_MODEL // 2), jnp.int32),
        jax.ShapeDtypeStruct((nq * rows, META_COLS), F32),
        jax.ShapeDtypeStruct((nq * nt, META_COLS, tm), F32),
        jax.ShapeDtypeStruct((N_EXPERTS, 1), F32),
    ]
    scratch = [
        pltpu.VMEM((HIST_ROWS + tm + SUBLANES, D_CONV), F32),
        pltpu.VMEM((HIST_ROWS + tm + SUBLANES, D_CONV), F32),
        pltpu.VMEM((tm, D_CONV), F32),
        pltpu.VMEM((tm + SUBLANES, D_SSM), F32),
        pltpu.VMEM((tm + SUBLANES, D_SSM), F32),
        pltpu.VMEM((tm, D_STATE), F32),
        pltpu.VMEM((tm + SUBLANES, D_STATE), F32),
        pltpu.VMEM((tm, D_STATE), F32),
        pltpu.VMEM((tm + SUBLANES, D_STATE), F32),
        pltpu.VMEM((SUBLANES, D_STATE), F32),
        pltpu.VMEM((tm, D_MODEL), F32),
        pltpu.VMEM((N_EXPERTS, 1), F32),
    ]
    if not fuse_in:
        scratch.append(pltpu.VMEM((D_MODEL // LANES, tm, LANES), F32))
    return pl.pallas_call(
        kern,
        grid=(nq, nt),
        in_specs=in_specs,
        out_specs=out_specs,
        out_shape=out_shape,
        scratch_shapes=scratch,
        compiler_params=pltpu.CompilerParams(
            dimension_semantics=("arbitrary", "arbitrary"),
            vmem_limit_bytes=VMEM_LIMIT_BYTES),
        name="mixer",
    )(*operands, hist0, h0, kpad, vpad, perm, permt,
      lw["w_in"], lw["wdw"], lw["bdw"], lw["clg"], lw["clb"], lw["a1"], lw["a2"],
      lw["bre"], lw["bim"], lw["cre"], lw["cim"], lw["d"], lw["wglu"], lw["bglu"],
      lw["w_out"], lw["g1"], lw["b1"], wr, br, triu, cnt0)


def _route(logits_t):
    m = jnp.max(logits_t, axis=0, keepdims=True)
    e = jnp.exp(logits_t - m)
    aff = e / jnp.sum(e, axis=0, keepdims=True)
    rows = [aff[j:j + 1, :] for j in range(N_EXPERTS)]

    scores = []
    for gi in range(N_EXPERT_GROUPS):
        a, b, c, d = rows[EXPERTS_PER_GROUP * gi:EXPERTS_PER_GROUP * (gi + 1)]
        hi1, lo1 = jnp.maximum(a, b), jnp.minimum(a, b)
        hi2, lo2 = jnp.maximum(c, d), jnp.minimum(c, d)
        scores.append(jnp.maximum(hi1, hi2) + jnp.maximum(jnp.minimum(hi1, hi2), jnp.maximum(lo1, lo2)))
    best = scores[0]
    sel = jnp.zeros_like(best)
    for gi in range(1, N_EXPERT_GROUPS):
        better = scores[gi] > best
        sel = jnp.where(better, float(gi), sel)
        best = jnp.where(better, scores[gi], best)

    hot1, hot2 = [], []
    for gi in range(N_EXPERT_GROUPS):
        vals = rows[EXPERTS_PER_GROUP * gi:EXPERTS_PER_GROUP * (gi + 1)]
        chosen = sel == float(gi)
        for j in range(EXPERTS_PER_GROUP):
            ahead = jnp.zeros_like(best)
            for k in range(EXPERTS_PER_GROUP):
                if k < j:
                    ahead = ahead + jnp.where(vals[k] >= vals[j], 1.0, 0.0)
                elif k > j:
                    ahead = ahead + jnp.where(vals[k] > vals[j], 1.0, 0.0)
            hot1.append(jnp.where(chosen, jnp.where(ahead == 0.0, 1.0, 0.0), 0.0))
            hot2.append(jnp.where(chosen, jnp.where(ahead == 1.0, 1.0, 0.0), 0.0))
    return jnp.concatenate(hot1, axis=0), jnp.concatenate(hot2, axis=0), aff


def _pack_halves(y):
    half = y.shape[1] // 2
    lo = lax.bitcast_convert_type(y[:, :half].astype(BF16).astype(F32), jnp.int32)
    hi = lax.bitcast_convert_type(y[:, half:].astype(BF16).astype(F32), jnp.int32)
    return lax.shift_right_logical(lo, 16) | (hi & HI_HALF_MASK)


def _unpack_halves(w):
    lo = lax.bitcast_convert_type(lax.shift_left(w, 16), F32)
    hi = lax.bitcast_convert_type(w & HI_HALF_MASK, F32)
    return lo, hi


def _route_rows(x, wr_ref, br_ref, triu_ref, xp_ref, meta_ref, metat_ref, cnt_ref, running):
    tm = x.shape[0]
    xp_ref[...] = _pack_halves(x)
    logits_t = lax.dot_general(wr_ref[...], x.astype(BF16), (((1,), (1,)), ((), ())),
                               preferred_element_type=F32) + br_ref[...]
    hot1, hot2, aff = _route(logits_t)
    eid = lax.broadcasted_iota(jnp.int32, (N_EXPERTS, tm), 0).astype(F32)
    both = hot1 + hot2
    before = _dot(both.astype(BF16), triu_ref[...]) + running[...]

    def pick(hot, vals):
        return jnp.sum(hot * vals, axis=0, keepdims=True)

    v1 = pick(hot1, aff)
    v2 = pick(hot2, aff)
    denom = v1 + v2
    meta_t = jnp.concatenate(
        [pick(hot1, eid), pick(hot2, eid), v1 / denom, v2 / denom, pick(hot1, before), pick(hot2, before),
         jnp.zeros((LANES - 6, tm), F32)], axis=0)
    metat_ref[0] = meta_t[0:META_COLS, :]
    meta_ref[...] = meta_t.T[:, 0:META_COLS]
    running[...] = running[...] + jnp.sum(both, axis=1, keepdims=True)
    cnt_ref[...] = running[...]


def _expert_kernel(te_ref, nxt_ref, first_ref, nt_ref, xs_ref, wup_hbm, bup_ref, wdn_hbm, ys_ref,
                   wup_f32, wdn_f32, wup_bf, wdn_bf, sems, slot_ref, *, layer):
    i = pl.program_id(0)

    def weight_copies(e, slot):
        return (pltpu.make_async_copy(wup_hbm.at[layer, e], wup_f32.at[slot], sems.at[0, slot]),
                pltpu.make_async_copy(wdn_hbm.at[layer, e], wdn_f32.at[slot], sems.at[1, slot]))

    @pl.when(i == 0)
    def _():
        slot_ref[0] = 0
        for c in weight_copies(te_ref[0], 0):
            c.start()

    @pl.when(first_ref[i] == 1)
    def _():
        slot = slot_ref[0]
        for c in weight_copies(te_ref[i], slot):
            c.wait()
        wup_bf[...] = wup_f32[slot].astype(BF16)
        wdn_bf[...] = wdn_f32[slot].astype(BF16)

        @pl.when(nxt_ref[i] >= 0)
        def _():
            for c in weight_copies(nxt_ref[i], 1 - slot):
                c.start()

        slot_ref[0] = 1 - slot

    @pl.when(i < nt_ref[0])
    def _():
        half = D_MODEL // 2
        rt = xs_ref.shape[0]
        sub = min(EXPERT_SUB_ROWS, rt)
        for r in range(rt // sub):
            rows = slice(r * sub, (r + 1) * sub)
            lo, hi = _unpack_halves(xs_ref[rows, :])
            h = (_dot(lo.astype(BF16), wup_bf[0:half, :])
                 + _dot(hi.astype(BF16), wup_bf[half:D_MODEL, :]) + bup_ref[0])
            ys_ref[rows, :] = _pack_halves(_dot(_gelu_tanh(h).astype(BF16), wdn_bf[...]))


def _expert_call(xs, tile_expert, next_expert, first_tile, n_tiles, lw, *, rt):
    rows = xs.shape[0]

    def row_map(i, te, nxt, first, nt):
        return (jnp.minimum(i, nt[0] - 1), 0)

    return pl.pallas_call(
        functools.partial(_expert_kernel, layer=lw["layer"]),
        grid_spec=pltpu.PrefetchScalarGridSpec(
            num_scalar_prefetch=4,
            grid=(rows // rt,),
            in_specs=[
                pl.BlockSpec((rt, D_MODEL // 2), row_map),
                pl.BlockSpec(memory_space=pl.ANY),
                pl.BlockSpec((None, 1, 1, D_EXPERT),
                             lambda i, te, nxt, first, nt: (lw["layer"], te[i], 0, 0)),
                pl.BlockSpec(memory_space=pl.ANY),
            ],
            out_specs=pl.BlockSpec((rt, D_MODEL // 2), row_map),
            scratch_shapes=[pltpu.VMEM((2, D_MODEL, D_EXPERT), F32),
                            pltpu.VMEM((2, D_EXPERT, D_MODEL), F32),
                            pltpu.VMEM((D_MODEL, D_EXPERT), BF16),
                            pltpu.VMEM((D_EXPERT, D_MODEL), BF16),
                            pltpu.SemaphoreType.DMA((2, 2)),
                            pltpu.SMEM((1,), jnp.int32)],
        ),
        out_shape=jax.ShapeDtypeStruct((rows, D_MODEL // 2), jnp.int32),
        compiler_params=pltpu.CompilerParams(
            dimension_semantics=("arbitrary",), vmem_limit_bytes=VMEM_LIMIT_BYTES),
        name="experts",
    )(tile_expert, next_expert, first_tile, n_tiles, xs, lw["w_up"], lw["b_up"], lw["w_down"])


def _moe_combine(x1, yg_ref, meta, g, b, alpha):
    g1 = meta[:, 2:3]
    g2 = meta[:, 3:4]
    lo1, hi1 = _unpack_halves(yg_ref[0])
    lo2, hi2 = _unpack_halves(yg_ref[1])
    moe = jnp.concatenate([g1 * lo1 + g2 * lo2, g1 * hi1 + g2 * hi2], axis=1)
    return _layer_norm(alpha * x1 + moe, g, b)


def _combine_kernel(x_ref, yg_ref, meta_ref, g2_ref, b2_ref, *rest, alpha):
    o_ref, yslab = rest[-2:]
    y = _moe_combine(x_ref[...], yg_ref, meta_ref[...], g2_ref[...], b2_ref[...], alpha)
    nslab = D_MODEL // LANES
    steps = y.shape[0] // SEQS_PER_STREAM
    for j in range(nslab):
        yslab[j] = y[:, j * LANES:(j + 1) * LANES]
    for s in range(SEQS_PER_STREAM):
        for j in range(nslab):
            o_ref[s, :, j * LANES:(j + 1) * LANES] = yslab[j, pl.ds(s, steps, stride=SEQS_PER_STREAM), :]


def _combine_call(x1, yg, meta, lw, *, alpha, seq_len, step_off=0, prev_out=None):
    nq, srows, _ = x1.shape
    tmc = min(1024, srows)
    tps = srows // tmc
    steps = tmc // SEQS_PER_STREAM
    blk_off = step_off // steps
    in_specs = [
        pl.BlockSpec((tmc, D_MODEL), lambda i: (i, 0)),
        pl.BlockSpec((2, tmc, D_MODEL // 2), lambda i: (0, i, 0)),
        pl.BlockSpec((tmc, META_COLS), lambda i: (i, 0)),
        pl.BlockSpec((None, 1, D_MODEL), lambda i: (lw["layer"], 0, 0)),
        pl.BlockSpec((None, 1, D_MODEL), lambda i: (lw["layer"], 0, 0)),
    ]
    operands = [x1.reshape(nq * srows, D_MODEL), yg, meta, lw["g2"], lw["b2"]]
    aliases = {}
    if prev_out is not None:
        in_specs.append(pl.BlockSpec(memory_space=pl.ANY))
        operands.append(prev_out)
        aliases = {len(operands) - 1: 0}
    return pl.pallas_call(
        functools.partial(_combine_kernel, alpha=alpha),
        grid=(nq * tps,),
        in_specs=in_specs,
        out_specs=pl.BlockSpec((SEQS_PER_STREAM, steps, D_MODEL),
                               lambda i: (i // tps, i % tps + blk_off, 0)),
        out_shape=jax.ShapeDtypeStruct((nq * SEQS_PER_STREAM, seq_len, D_MODEL), F32),
        scratch_shapes=[pltpu.VMEM((D_MODEL // LANES, tmc, LANES), F32)],
        input_output_aliases=aliases,
        compiler_params=pltpu.CompilerParams(
            dimension_semantics=("arbitrary",), vmem_limit_bytes=VMEM_LIMIT_BYTES),
        name="combine",
    )(*operands)


def _sc_mesh():
    return plsc.VectorSubcoreMesh(core_axis_name="c", subcore_axis_name="s")


def _sc_chunk(rows):
    per_worker = rows // SC_WORKERS
    chunk = min(SC_MAX_CHUNK, per_worker)
    assert per_worker % chunk == 0 and chunk % 8 == 0
    return per_worker // chunk, chunk


def _sc_dispatch(xps, poss, n_slots):
    width = xps[0].shape[1]
    ngroups = len(xps)
    plans = [_sc_chunk(x.shape[0]) for x in xps]
    posrs = [p.reshape(2, SC_WORKERS, nch, ch) for p, (nch, ch) in zip(poss, plans)]
    scratch = []
    for _, ch in plans:
        scratch += [pltpu.VMEM((ch,), jnp.int32), pltpu.VMEM((ch,), jnp.int32),
                    pltpu.VMEM((ch, width), jnp.int32)]

    @functools.partial(
        pl.kernel, mesh=_sc_mesh(),
        out_type=jax.ShapeDtypeStruct((n_slots, width), jnp.int32),
        scratch_types=scratch + [pltpu.SemaphoreType.DMA],
        name="sc_dispatch")
    def k(*refs):
        x_hbms, pos_hbms, o_hbm = refs[:ngroups], refs[ngroups:2 * ngroups], refs[2 * ngroups]
        bufs, sem = refs[2 * ngroups + 1:-1], refs[-1]
        wid = lax.axis_index("s") * SC_CORES + lax.axis_index("c")
        for g, (nch, ch) in enumerate(plans):
            idx0, idx1, buf = bufs[3 * g:3 * g + 3]

            @pl.loop(0, nch)
            def _(c, g=g, nch=nch, ch=ch, idx0=idx0, idx1=idx1, buf=buf):
                base = (wid * nch + c) * ch
                pltpu.sync_copy(x_hbms[g].at[pl.ds(base, ch)], buf)
                pltpu.sync_copy(pos_hbms[g].at[0, wid, c], idx0)
                pltpu.sync_copy(pos_hbms[g].at[1, wid, c], idx1)
                pltpu.async_copy(buf, o_hbm.at[idx0], sem).wait()
                pltpu.async_copy(buf, o_hbm.at[idx1], sem).wait()

    return k(*xps, *posrs)


def _sc_gather(ys, poss):
    width = ys.shape[1]
    ngroups = len(poss)
    plans = [_sc_chunk(p.shape[1]) for p in poss]
    posrs = [p.reshape(2, SC_WORKERS, nch, ch) for p, (nch, ch) in zip(poss, plans)]
    scratch = []
    for _, ch in plans:
        scratch += [pltpu.VMEM((ch,), jnp.int32), pltpu.VMEM((ch, width), jnp.int32)]

    @functools.partial(
        pl.kernel, mesh=_sc_mesh(),
        out_type=[jax.ShapeDtypeStruct((2, p.shape[1], width), jnp.int32) for p in poss],
        scratch_types=scratch + [pltpu.SemaphoreType.DMA],
        name="sc_gather")
    def k(*refs):
        y_hbm, pos_hbms = refs[0], refs[1:1 + ngroups]
        o_hbms = refs[1 + ngroups:1 + 2 * ngroups]
        bufs, sem = refs[1 + 2 * ngroups:-1], refs[-1]
        wid = lax.axis_index("s") * SC_CORES + lax.axis_index("c")
        for g, (nch, ch) in enumerate(plans):
            idx, buf = bufs[2 * g:2 * g + 2]

            @pl.loop(0, nch)
            def _(c, g=g, nch=nch, ch=ch, idx=idx, buf=buf):
                base = (wid * nch + c) * ch
                for kk in range(2):
                    pltpu.sync_copy(pos_hbms[g].at[kk, wid, c], idx)
                    pltpu.async_copy(y_hbm.at[idx], buf, sem).wait()
                    pltpu.sync_copy(buf, o_hbms[g].at[kk, pl.ds(base, ch)])

    return k(ys, *posrs)


def _moe_rows(xps, meta_ts, counts, lw):
    total = sum(x.shape[0] for x in xps)
    rt = 512 if total >= 8192 else 128
    n_slots = 2 * total + N_EXPERTS * rt
    nt_max = n_slots // rt

    cnt = counts[:, 0].astype(jnp.int32)
    padded = ((cnt + rt - 1) // rt) * rt
    ends = jnp.cumsum(padded)
    offs = ends - padded
    experts = jnp.arange(N_EXPERTS, dtype=jnp.int32)[:, None]
    poss = []
    for x, meta_t in zip(xps, meta_ts):
        meta_t = meta_t.transpose(1, 0, 2).reshape(META_COLS, x.shape[0])
        eidx = meta_t[0:2].astype(jnp.int32)
        rank = meta_t[4:6].astype(jnp.int32)
        poss.append(jnp.sum(jnp.where(eidx[:, None, :] == experts, offs[:, None], 0), axis=1) + rank)
    n_tiles = ends[-1:] // rt
    tiles = jnp.minimum(jnp.arange(nt_max, dtype=jnp.int32), n_tiles[0] - 1)
    tile_expert = jnp.minimum(jnp.sum((ends // rt)[None, :] <= tiles[:, None], axis=1),
                              N_EXPERTS - 1).astype(jnp.int32)

    first_tile = jnp.concatenate([jnp.ones((1,), jnp.int32),
                                  (tile_expert[1:] != tile_expert[:-1]).astype(jnp.int32)])
    eid = jnp.arange(N_EXPERTS, dtype=jnp.int32)
    later = (eid[None, :] > eid[:, None]) & (cnt[None, :] > 0)
    next_of = jnp.min(jnp.where(later, eid[None, :], N_EXPERTS), axis=1)
    next_of = jnp.where(next_of == N_EXPERTS, -1, next_of)
    next_expert = jnp.sum(jnp.where(tile_expert[:, None] == eid[None, :], next_of[None, :], 0),
                          axis=1).astype(jnp.int32)

    xs = _sc_dispatch(xps, poss, n_slots)
    ys = _expert_call(xs, tile_expert, next_expert, first_tile, n_tiles.astype(jnp.int32), lw, rt=rt)
    return _sc_gather(ys, poss)


def _memkv_kernel(m_ref, wk_ref, wv_ref, k_ref, v_ref):
    mb = m_ref[...].astype(BF16)
    k_ref[0] = _dot(mb, wk_ref[0])
    v_ref[0] = _dot(mb, wv_ref[0])


def _memkv_call(mem, wk, wv):
    depth = wk.shape[0]
    rows = mem.shape[0]
    return pl.pallas_call(
        _memkv_kernel,
        grid=(depth,),
        in_specs=[
            pl.BlockSpec((rows, D_MODEL), lambda l: (0, 0)),
            pl.BlockSpec((1, D_MODEL, D_ATT), lambda l: (l, 0, 0)),
            pl.BlockSpec((1, D_MODEL, D_ATT), lambda l: (l, 0, 0)),
        ],
        out_specs=[
            pl.BlockSpec((1, rows, D_ATT), lambda l: (l, 0, 0)),
            pl.BlockSpec((1, rows, D_ATT), lambda l: (l, 0, 0)),
        ],
        out_shape=[jax.ShapeDtypeStruct((depth, rows, D_ATT), F32)] * 2,
        compiler_params=pltpu.CompilerParams(
            dimension_semantics=("arbitrary",), vmem_limit_bytes=VMEM_LIMIT_BYTES),
        name="memkv",
    )(mem, wk, wv)


def _perm_matrices(lc):
    tm = lc * SEQS_PER_STREAM
    p = np.zeros((tm, tm), np.float32)
    for s in range(SEQS_PER_STREAM):
        for t in range(lc):
            p[s * lc + t, t * SEQS_PER_STREAM + s] = 1.0
    return jnp.asarray(p, BF16), jnp.asarray(p.T, BF16)


def _to_streams(a):
    b, l, c = a.shape
    q = b // SEQS_PER_STREAM
    return a.reshape(q, SEQS_PER_STREAM, l, c).transpose(0, 2, 1, 3).reshape(q, l * SEQS_PER_STREAM, c)


def _from_streams(a, l):
    q, _, c = a.shape
    return a.reshape(q, l, SEQS_PER_STREAM, c).transpose(0, 2, 1, 3).reshape(q * SEQS_PER_STREAM, l, c)


def _pack_state(re, im):
    b = re.shape[0]
    q = b // SEQS_PER_STREAM
    re = re.reshape(q, SEQS_PER_STREAM, D_STATE)
    im = im.reshape(q, SEQS_PER_STREAM, D_STATE)
    return jnp.concatenate([im, re], axis=1)


def _unpack_state(h):
    q = h.shape[0]
    im = h[:, 0:SEQS_PER_STREAM].reshape(q * SEQS_PER_STREAM, N_SSM_GROUPS, SSM_STATE)
    re = h[:, SEQS_PER_STREAM:].reshape(q * SEQS_PER_STREAM, N_SSM_GROUPS, SSM_STATE)
    return re, im


def _pad_heads(mk, mv):
    b = mk.shape[0]
    q = b // SEQS_PER_STREAM
    shape = (q, SEQS_PER_STREAM, N_MEM, D_ATT)
    return mk.reshape(shape).transpose(0, 1, 3, 2).astype(BF16), mv.reshape(shape).astype(BF16)


def _model_params(w_in, w_dw, b_dw, conv_ln_g, conv_ln_b, ssm_a_re, ssm_a_im, ssm_b_re,
                  ssm_b_im, ssm_c_re, ssm_c_im, ssm_d, ssm_log_dt, ssm_w_glu, ssm_b_glu,
                  w_out, ln1_g, ln1_b, w_up, b_up, w_down, ln2_g, ln2_b):
    depth = w_in.shape[0]
    a = lax.complex(ssm_a_re, ssm_a_im)
    dt = jnp.exp(ssm_log_dt)[..., None]
    a_bar = jnp.exp(a * dt)
    b_bar = ((a_bar - 1.0) / a)[..., None] * lax.complex(ssm_b_re, ssm_b_im)
    eye = jnp.eye(N_SSM_GROUPS, dtype=F32)

    def b_block(m):
        full = jnp.einsum("lgpi,gh->lgihp", m, eye).reshape(depth, D_SSM, D_STATE).astype(BF16)
        return jnp.stack([full[:, k * LANES:(k + 1) * LANES, k * SCAN_LANES:(k + 1) * SCAN_LANES]
                          for k in range(SSM_BLOCKS)], axis=1)

    def c_block(m):
        full = jnp.einsum("lgip,gh->lgphi", m, eye).reshape(depth, D_STATE, D_SSM).astype(BF16)
        return jnp.stack([full[:, k * SCAN_LANES:(k + 1) * SCAN_LANES, k * LANES:(k + 1) * LANES]
                          for k in range(SSM_BLOCKS)], axis=1)

    ar = jnp.real(a_bar).reshape(depth, 1, D_STATE)
    ai = jnp.imag(a_bar).reshape(depth, 1, D_STATE)
    half = SUBLANES // 2
    return {
        "w_in": w_in.astype(BF16),
        "wdw": jnp.repeat(w_dw, SUBLANES, axis=1),
        "bdw": b_dw[:, None], "clg": conv_ln_g[:, None], "clb": conv_ln_b[:, None],
        "a1": jnp.broadcast_to(ar, (depth, SUBLANES, D_STATE)),
        "a2": jnp.concatenate([jnp.broadcast_to(-ai, (depth, half, D_STATE)),
                               jnp.broadcast_to(ai, (depth, half, D_STATE))], axis=1),
        "bre": b_block(jnp.real(b_bar)), "bim": b_block(jnp.imag(b_bar)),
        "cre": c_block(ssm_c_re), "cim": c_block(-ssm_c_im),
        "d": ssm_d[:, None], "wglu": ssm_w_glu.astype(BF16), "bglu": ssm_b_glu[:, None],
        "w_out": w_out.astype(BF16), "g1": ln1_g[:, None], "b1": ln1_b[:, None],
        "w_up": w_up, "b_up": b_up[:, :, None, :], "w_down": w_down,
        "g2": ln2_g[:, None], "b2": ln2_b[:, None],
    }


def kernel(x_prompt, x_sample, cache_conv, state_ssm_re, state_ssm_im, cache_mem_k, cache_mem_v, mem_prompt, w_in, w_dw, b_dw, conv_ln_g, conv_ln_b, ssm_a_re, ssm_a_im, ssm_b_re, ssm_b_im, ssm_c_re, ssm_c_im, ssm_d, ssm_log_dt, ssm_w_glu, ssm_b_glu, w_mem_k, w_mem_v, w_out, ln1_g, ln1_b, w_router, b_router, w_up, b_up, w_down, ln2_g, ln2_b):
    depth = w_in.shape[0]
    alpha = (2.0 * depth) ** 0.25
    bp, seq, _ = x_prompt.shape
    bs, dec_seq, _ = x_sample.shape
    assert bp == SEQS_PER_STREAM and bs % SEQS_PER_STREAM == 0
    lc_p = min(128, seq)
    lc_s = dec_seq
    assert seq % lc_p == 0 and lc_p % 16 == 0 and lc_s % 16 == 0

    mb, mm, _ = mem_prompt.shape
    mk_all, mv_all = _memkv_call(mem_prompt.reshape(mb * mm, D_MODEL),
                                 w_mem_k.astype(BF16), w_mem_v.astype(BF16))
    mk_all = mk_all.reshape(depth, mb, mm, N_MEM_HEADS, MEM_HEAD_DIM)
    mv_all = mv_all.reshape(depth, mb, mm, N_MEM_HEADS, MEM_HEAD_DIM)

    perm_p, permt_p = _perm_matrices(lc_p)
    perm_s, permt_s = _perm_matrices(lc_s)
    wr = w_router.T.astype(BF16)
    br = b_router[:, None]

    tiles_p = seq // lc_p
    tiles_a = max(tiles_p // 2, 1)
    tiles_b = tiles_p - tiles_a
    steps_a = tiles_a * lc_p
    xa = xb = x_prompt
    xs = x_sample
    zero_hist = jnp.zeros((1, HIST_ROWS, D_CONV), F32)
    zero_h = jnp.zeros((1, SUBLANES, D_STATE), F32)
    zero_cnt = jnp.zeros((N_EXPERTS, 1), F32)

    params = _model_params(w_in, w_dw, b_dw, conv_ln_g, conv_ln_b, ssm_a_re, ssm_a_im, ssm_b_re,
                           ssm_b_im, ssm_c_re, ssm_c_im, ssm_d, ssm_log_dt, ssm_w_glu, ssm_b_glu,
                           w_out, ln1_g, ln1_b, w_up, b_up, w_down, ln2_g, ln2_b)
    nq_p = bp // SEQS_PER_STREAM
    nq_s = bs // SEQS_PER_STREAM
    head_shape = (mm, N_MEM_HEADS, MEM_HEAD_DIM)
    kp_p, vp_p = _pad_heads(mk_all.reshape((depth * mb,) + head_shape),
                            mv_all.reshape((depth * mb,) + head_shape))
    kp_s, vp_s = _pad_heads(cache_mem_k.reshape((depth * bs,) + head_shape),
                            cache_mem_v.reshape((depth * bs,) + head_shape))
    hist_s = _to_streams(cache_conv.reshape(depth * bs, CONV_BUF, D_CONV))
    h_s = _pack_state(state_ssm_re.reshape(depth * bs, N_SSM_GROUPS, SSM_STATE),
                      state_ssm_im.reshape(depth * bs, N_SSM_GROUPS, SSM_STATE))

    hists_p, hsts_p, hists_s, hsts_s = [], [], [], []
    moe_a = moe_b = moe_s = None
    for l in range(depth):
        lw = dict(params, layer=l)
        xa, hist, hst, rows_a, meta_a, metat_a, cnt_a = _mixer_call(
            xa, moe_a, zero_hist, zero_h, kp_p, vp_p, perm_p, permt_p, lw, wr, br, zero_cnt,
            lc=lc_p, alpha=alpha, tile_off=0, n_tiles=tiles_a, kv_off=l * nq_p)
        (yg_a,) = _moe_rows([rows_a], [metat_a], cnt_a, lw)
        moe_a = (yg_a, meta_a)
        xb, hist, hst, rows_b, meta_b, metat_b, cnt_b = _mixer_call(
            xb, moe_b, hist, hst, kp_p, vp_p, perm_p, permt_p, lw, wr, br, zero_cnt,
            lc=lc_p, alpha=alpha, tile_off=tiles_a, n_tiles=tiles_b, kv_off=l * nq_p)
        hists_p.append(hist)
        hsts_p.append(hst)
        xs, hist, hst, rows_s, meta_s, metat_s, cnt_s = _mixer_call(
            xs, moe_s, hist_s, h_s, kp_s, vp_s, perm_s, permt_s, lw, wr, br, cnt_b,
            lc=lc_s, alpha=alpha, state_off=l * nq_s, kv_off=l * nq_s)
        hists_s.append(hist)
        hsts_s.append(hst)
        yg_b, yg_s = _moe_rows([rows_b, rows_s], [metat_b, metat_s], cnt_s, lw)
        moe_b = (yg_b, meta_b)
        moe_s = (yg_s, meta_s)

    y_prompt = _combine_call(xa, moe_a[0], moe_a[1], lw, alpha=alpha, seq_len=seq)
    y_prompt = _combine_call(xb, moe_b[0], moe_b[1], lw, alpha=alpha, seq_len=seq, step_off=steps_a,
                             prev_out=y_prompt)
    y_sample = _combine_call(xs, moe_s[0], moe_s[1], lw, alpha=alpha, seq_len=dec_seq)

    def states(hists, hsts, batch):
        conv = _from_streams(jnp.concatenate(hists, axis=0), CONV_BUF)
        re, im = _unpack_state(jnp.concatenate(hsts, axis=0))
        return (conv.reshape(depth, batch, CONV_BUF, D_CONV),
                re.reshape(depth, batch, N_SSM_GROUPS, SSM_STATE),
                im.reshape(depth, batch, N_SSM_GROUPS, SSM_STATE))

    conv_p, re_p, im_p = states(hists_p, hsts_p, bp)
    conv_s, re_s, im_s = states(hists_s, hsts_s, bs)
    return (y_prompt, y_sample, conv_p, re_p, im_p, mk_all, mv_all, conv_s, re_s, im_s)
```

```python
import functools
import math

import numpy as np
import jax
import jax.numpy as jnp
from jax import lax
from jax.experimental import pallas as pl
from jax.experimental.pallas import tpu as pltpu
from jax.experimental.pallas import tpu_sc as plsc

F32 = jnp.float32
BF16 = jnp.bfloat16

D_MODEL = 1024
CONV_WIDTH = 31
CONV_BUF = CONV_WIDTH - 1
D_CONV = 384
D_SSM = 384
SSM_GROUP = 16
N_SSM_GROUPS = D_SSM // SSM_GROUP
SSM_STATE = 64
D_STATE = N_SSM_GROUPS * SSM_STATE
N_MEM = 256
N_MEM_HEADS = 4
MEM_HEAD_DIM = 64
D_ATT = N_MEM_HEADS * MEM_HEAD_DIM
D_IN = 2 * D_CONV + D_SSM + D_ATT
N_EXPERTS = 16
N_EXPERT_GROUPS = 4
EXPERTS_PER_GROUP = N_EXPERTS // N_EXPERT_GROUPS
D_EXPERT = 512
LN_EPS = 1e-5

SEQS_PER_STREAM = 4
HIST_ROWS = CONV_BUF * SEQS_PER_STREAM
SUBLANES = 8
LANES = 128
SCAN_LANES = 512
SSM_BLOCKS = D_STATE // SCAN_LANES
CONV_ROWS = 32
EXPERT_SUB_ROWS = 512
VMEM_LIMIT_BYTES = 56 * 1024 * 1024
HI_HALF_MASK = np.int32(-65536)
META_COLS = 8
SC_CORES = 2
SC_WORKERS = SC_CORES * 16
SC_MAX_CHUNK = 128


def _sigmoid(x):
    return 1.0 / (1.0 + jnp.exp(-x))


def _gelu_tanh(x):
    c = math.sqrt(2.0 / math.pi)
    return 0.5 * x * (1.0 + jnp.tanh(c * (x + 0.044715 * (x * x * x))))


def _layer_norm(z, g, b):
    mu = jnp.mean(z, axis=-1, keepdims=True)
    zc = z - mu
    var = jnp.mean(zc * zc, axis=-1, keepdims=True)
    return zc * lax.rsqrt(var + LN_EPS) * g + b


def _dot(a, b):
    return jnp.dot(a, b, preferred_element_type=F32)


def _mixer_kernel(*refs, tm, lc, alpha, fuse_in):
    refs = list(refs)
    if fuse_in:
        xprev_ref, yg_ref, metain_ref, g2p_ref, b2p_ref = refs[:5]
        refs = refs[5:]
    else:
        xprev_ref = refs.pop(0)
        xslab = refs.pop()
    (hist0_ref, h0_ref, k_ref, v_ref, perm_ref, permt_ref,
     w_in_ref, wdw_ref, bdw_ref, clg_ref, clb_ref, a1_ref, a2_ref,
     bre_ref, bim_ref, cre_ref, cim_ref, d_ref, wglu_ref, bglu_ref,
     wout_ref, g1_ref, b1_ref, wr_ref, br_ref, tri_ref, cnt0_ref,
     x1_ref, hist_out_ref, h_out_ref, xp_ref, meta_ref, metat_ref, cnt_ref,
     xpad0, xpad4, cy, ush, yim, bu_re, bu_im, hre, him, hcar, xin, running) = refs
    i = pl.program_id(1)

    @pl.when(i == 0)
    def _():
        xpad0[0:HIST_ROWS, :] = hist0_ref[0]
        hcar[...] = h0_ref[0]
        ush[...] = jnp.zeros_like(ush)

    @pl.when((i == 0) & (pl.program_id(0) == 0))
    def _():
        running[...] = cnt0_ref[...]

    if fuse_in:
        xin[...] = _moe_combine(xprev_ref[0], yg_ref, metain_ref[...], g2p_ref[...], b2p_ref[...], alpha)
    else:
        nslab = D_MODEL // LANES
        for s in range(SEQS_PER_STREAM):
            for j in range(nslab):
                xslab[j, pl.ds(s, lc, stride=SEQS_PER_STREAM), :] = xprev_ref[s, :, j * LANES:(j + 1) * LANES]
        xin[...] = jnp.concatenate([xslab[j] for j in range(nslab)], axis=1)
    x = xin[...]
    proj = _dot(x.astype(BF16), w_in_ref[...])

    g = proj[:, 0:D_CONV] * _sigmoid(proj[:, D_CONV:2 * D_CONV])
    xpad0[HIST_ROWS:HIST_ROWS + tm, :] = g
    xpad4[0:HIST_ROWS + tm - 4, :] = xpad0[4:HIST_ROWS + tm, :]

    nsub = CONV_ROWS // SUBLANES

    def conv_rows(rb, carry):
        r0 = pl.multiple_of(rb * CONV_ROWS, CONV_ROWS)
        accs = [jnp.broadcast_to(bdw_ref[...], (SUBLANES, D_CONV)) for _ in range(nsub)]
        for k in range(CONV_WIDTH):
            wk = wdw_ref[SUBLANES * k:SUBLANES * (k + 1), :]
            for sb in range(nsub):
                off = r0 + SEQS_PER_STREAM * k + SUBLANES * sb
                if k % 2 == 0:
                    xs = xpad0[pl.ds(pl.multiple_of(off, SUBLANES), SUBLANES), :]
                else:
                    xs = xpad4[pl.ds(pl.multiple_of(off - 4, SUBLANES), SUBLANES), :]
                accs[sb] = accs[sb] + xs * wk
        for sb in range(nsub):
            cy[pl.ds(pl.multiple_of(r0 + SUBLANES * sb, SUBLANES), SUBLANES), :] = accs[sb]
        return carry

    lax.fori_loop(0, tm // CONV_ROWS, conv_rows, 0, unroll=True)
    conv_n = _layer_norm(cy[...], clg_ref[...], clb_ref[...])
    cy[...] = conv_n * _sigmoid(conv_n)

    new_hist = xpad0[tm:tm + HIST_ROWS, :]
    xpad0[0:HIST_ROWS, :] = new_hist
    hist_out_ref[0] = new_hist

    u = proj[:, 2 * D_CONV:2 * D_CONV + D_SSM]
    ush[4:tm + 4, :] = u
    ub = u.astype(BF16)
    ub_sh = ush[...].astype(BF16)
    for m in range(SSM_BLOCKS):
        ch = slice(m * LANES, (m + 1) * LANES)
        st = slice(m * SCAN_LANES, (m + 1) * SCAN_LANES)
        bu_re[:, st] = _dot(ub[:, ch], bre_ref[m])
        bu_im[:, st] = _dot(ub_sh[:, ch], bim_ref[m])

    lo = lax.broadcasted_iota(jnp.int32, (SUBLANES, SCAN_LANES), 0) < 4
    for c in range(D_STATE // SCAN_LANES):
        cs = slice(c * SCAN_LANES, (c + 1) * SCAN_LANES)
        a1 = a1_ref[:, cs]
        a2 = a2_ref[:, cs]

        def scan_pair(j, carry, cs=cs, a1=a1, a2=a2):
            h_prev, im_cur = carry
            r = pl.multiple_of(j * SUBLANES, SUBLANES)
            re_cur = bu_re[pl.ds(r, SUBLANES), cs]
            im_next = bu_im[pl.ds(r + SUBLANES, SUBLANES), cs]
            p_even = jnp.where(lo, re_cur, im_cur)
            p_odd = jnp.where(lo, im_next, re_cur)
            h_even = a1 * pltpu.roll(h_prev, 4, 0) + a2 * h_prev + p_even
            h_odd = a1 * pltpu.roll(h_even, 4, 0) - a2 * h_even + p_odd
            hre[pl.ds(r, SUBLANES), cs] = jnp.where(lo, h_even, h_odd)
            him[pl.ds(r, SUBLANES), cs] = jnp.where(lo, h_prev, h_even)
            return h_odd, im_next

        h_last, _ = lax.fori_loop(0, lc // 2, scan_pair,
                                  (hcar[:, cs], bu_im[0:SUBLANES, cs]), unroll=True)
        him[tm:tm + SUBLANES, cs] = jnp.where(lo, h_last, 0.0)
        hcar[:, cs] = h_last
    h_out_ref[0] = hcar[...]

    y_re_blocks = []
    for m in range(SSM_BLOCKS):
        ch = slice(m * LANES, (m + 1) * LANES)
        st = slice(m * SCAN_LANES, (m + 1) * SCAN_LANES)
        y_re_blocks.append(_dot(hre[:, st].astype(BF16), cre_ref[m]))
        yim[:, ch] = _dot(him[:, st].astype(BF16), cim_ref[m])
    y_re = jnp.concatenate(y_re_blocks, axis=1)
    y = y_re + yim[4:tm + 4, :] + d_ref[...] * u
    y = _gelu_tanh(y)
    ssm_y = y * _sigmoid(_dot(y.astype(BF16), wglu_ref[...]) + bglu_ref[...])

    q = proj[:, 2 * D_CONV + D_SSM:D_IN].astype(BF16)
    q_seq = _dot(perm_ref[...], q)
    head_of_col = lax.shift_right_logical(lax.broadcasted_iota(jnp.int32, (1, D_ATT), 1), 6)
    head_masks = [jnp.where(head_of_col == h, 1.0, 0.0) for h in range(N_MEM_HEADS)]
    outs = []
    for s in range(SEQS_PER_STREAM):
        qs = q_seq[s * lc:(s + 1) * lc, :]
        ks = k_ref[0, s]
        vs = v_ref[0, s]
        acc = jnp.zeros((lc, D_ATT), F32)
        for h in range(N_MEM_HEADS):
            qh = (qs * head_masks[h]).astype(BF16)
            sc = _dot(qh, ks) * (MEM_HEAD_DIM ** -0.5)
            sc = sc - jnp.max(sc, axis=-1, keepdims=True)
            e = jnp.exp(sc)
            p = e * (1.0 / jnp.sum(e, axis=-1, keepdims=True))
            acc = acc + _dot(p.astype(BF16), vs) * head_masks[h]
        outs.append(acc)
    att_seq = jnp.concatenate(outs, axis=0).astype(BF16)
    att = _dot(permt_ref[...], att_seq).astype(BF16)

    mix = _dot(jnp.concatenate([cy[...].astype(BF16), ssm_y.astype(BF16), att], axis=1), wout_ref[...])
    x1 = _layer_norm(alpha * xin[...] + mix, g1_ref[...], b1_ref[...])
    x1_ref[0] = x1
    _route_rows(x1, wr_ref, br_ref, tri_ref, xp_ref, meta_ref, metat_ref, cnt_ref, running)


def _mixer_call(x, prev_moe, hist0, h0, kpad, vpad, perm, permt, lw, wr, br, cnt0, *, lc, alpha,
                tile_off=0, n_tiles=None, state_off=0, kv_off=0):
    fuse_in = prev_moe is not None
    tm = lc * SEQS_PER_STREAM
    if fuse_in:
        nq, rows, _ = x.shape
    else:
        nq = x.shape[0] // SEQS_PER_STREAM
        rows = (x.shape[1] // lc if n_tiles is None else n_tiles) * tm
    nt = rows // tm
    kern = functools.partial(_mixer_kernel, tm=tm, lc=lc, alpha=alpha, fuse_in=fuse_in)
    triu = jnp.asarray(np.triu(np.ones((tm, tm), np.float32), 1), BF16)

    def const(shape):
        return pl.BlockSpec(shape, lambda q, i: (0,) * len(shape))

    def flat(shape):
        return pl.BlockSpec(shape, lambda q, i: (0,) * (len(shape) - 2) + (q * nt + i, 0))

    layer = lw["layer"]

    def layered(shape, l=layer):
        return pl.BlockSpec((None,) + shape, lambda q, i: (l,) + (0,) * len(shape))

    operands = [x]
    if fuse_in:
        yg_prev, meta_prev = prev_moe
        in_specs = [pl.BlockSpec((1, tm, D_MODEL), lambda q, i: (q, i, 0)),
                    flat((2, tm, D_MODEL // 2)), flat((tm, META_COLS)),
                    layered((1, D_MODEL), layer - 1), layered((1, D_MODEL), layer - 1)]
        operands += [yg_prev, meta_prev, lw["g2"], lw["b2"]]
    else:
        in_specs = [pl.BlockSpec((SEQS_PER_STREAM, lc, D_MODEL), lambda q, i: (q, i + tile_off, 0))]
    in_specs += [
        pl.BlockSpec((1, HIST_ROWS, D_CONV), lambda q, i: (q + state_off, 0, 0)),
        pl.BlockSpec((1, SUBLANES, D_STATE), lambda q, i: (q + state_off, 0, 0)),
        pl.BlockSpec((1, SEQS_PER_STREAM, N_MEM, D_ATT), lambda q, i: (q + kv_off, 0, 0, 0)),
        pl.BlockSpec((1, SEQS_PER_STREAM, N_MEM, D_ATT), lambda q, i: (q + kv_off, 0, 0, 0)),
        const((tm, tm)), const((tm, tm)),
        layered((D_MODEL, D_IN)),
        layered((CONV_WIDTH * SUBLANES, D_CONV)), layered((1, D_CONV)), layered((1, D_CONV)),
        layered((1, D_CONV)),
        layered((SUBLANES, D_STATE)), layered((SUBLANES, D_STATE)),
        layered((SSM_BLOCKS, LANES, SCAN_LANES)), layered((SSM_BLOCKS, LANES, SCAN_LANES)),
        layered((SSM_BLOCKS, SCAN_LANES, LANES)), layered((SSM_BLOCKS, SCAN_LANES, LANES)),
        layered((1, D_SSM)), layered((D_SSM, D_SSM)), layered((1, D_SSM)),
        layered((D_MODEL, D_MODEL)), layered((1, D_MODEL)), layered((1, D_MODEL)),
        const((N_EXPERTS, D_MODEL)), const((N_EXPERTS, 1)), const((tm, tm)), const((N_EXPERTS, 1)),
    ]
    out_specs = [
        pl.BlockSpec((1, tm, D_MODEL), lambda q, i: (q, i, 0)),
        pl.BlockSpec((1, HIST_ROWS, D_CONV), lambda q, i: (q, 0, 0)),
        pl.BlockSpec((1, SUBLANES, D_STATE), lambda q, i: (q, 0, 0)),
        flat((tm, D_MODEL // 2)), flat((tm, META_COLS)),
        (pl.BlockSpec((META_COLS, tm), lambda q, i: (0, q * nt + i)) if tm % LANES == 0 else
         pl.BlockSpec((None, META_COLS, tm), lambda q, i: (q * nt + i, 0, 0))),
        const((N_EXPERTS, 1)),
    ]
    out_shape = [
        jax.ShapeDtypeStruct((nq, rows, D_MODEL), F32),
        jax.ShapeDtypeStruct((nq, HIST_ROWS, D_CONV), F32),
        jax.ShapeDtypeStruct((nq, SUBLANES, D_STATE), F32),
        jax.ShapeDtypeStruct((nq * rows, D_MODEL // 2), jnp.int32),
        jax.ShapeDtypeStruct((nq * rows, META_COLS), F32),
        jax.ShapeDtypeStruct((META_COLS, nq * rows) if tm % LANES == 0 else (nq * nt, META_COLS, tm), F32),
        jax.ShapeDtypeStruct((N_EXPERTS, 1), F32),
    ]
    scratch = [
        pltpu.VMEM((HIST_ROWS + tm + SUBLANES, D_CONV), F32),
        pltpu.VMEM((HIST_ROWS + tm + SUBLANES, D_CONV), F32),
        pltpu.VMEM((tm, D_CONV), F32),
        pltpu.VMEM((tm + SUBLANES, D_SSM), F32),
        pltpu.VMEM((tm + SUBLANES, D_SSM), F32),
        pltpu.VMEM((tm, D_STATE), F32),
        pltpu.VMEM((tm + SUBLANES, D_STATE), F32),
        pltpu.VMEM((tm, D_STATE), F32),
        pltpu.VMEM((tm + SUBLANES, D_STATE), F32),
        pltpu.VMEM((SUBLANES, D_STATE), F32),
        pltpu.VMEM((tm, D_MODEL), F32),
        pltpu.VMEM((N_EXPERTS, 1), F32),
    ]
    if not fuse_in:
        scratch.append(pltpu.VMEM((D_MODEL // LANES, tm, LANES), F32))
    return pl.pallas_call(
        kern,
        grid=(nq, nt),
        in_specs=in_specs,
        out_specs=out_specs,
        out_shape=out_shape,
        scratch_shapes=scratch,
        compiler_params=pltpu.CompilerParams(
            dimension_semantics=("arbitrary", "arbitrary"),
            vmem_limit_bytes=VMEM_LIMIT_BYTES),
        name="mixer",
    )(*operands, hist0, h0, kpad, vpad, perm, permt,
      lw["w_in"], lw["wdw"], lw["bdw"], lw["clg"], lw["clb"], lw["a1"], lw["a2"],
      lw["bre"], lw["bim"], lw["cre"], lw["cim"], lw["d"], lw["wglu"], lw["bglu"],
      lw["w_out"], lw["g1"], lw["b1"], wr, br, triu, cnt0)


def _route(logits_t):
    m = jnp.max(logits_t, axis=0, keepdims=True)
    e = jnp.exp(logits_t - m)
    aff = e / jnp.sum(e, axis=0, keepdims=True)
    rows = [aff[j:j + 1, :] for j in range(N_EXPERTS)]

    scores = []
    for gi in range(N_EXPERT_GROUPS):
        a, b, c, d = rows[EXPERTS_PER_GROUP * gi:EXPERTS_PER_GROUP * (gi + 1)]
        hi1, lo1 = jnp.maximum(a, b), jnp.minimum(a, b)
        hi2, lo2 = jnp.maximum(c, d), jnp.minimum(c, d)
        scores.append(jnp.maximum(hi1, hi2) + jnp.maximum(jnp.minimum(hi1, hi2), jnp.maximum(lo1, lo2)))
    best = scores[0]
    sel = jnp.zeros_like(best)
    for gi in range(1, N_EXPERT_GROUPS):
        better = scores[gi] > best
        sel = jnp.where(better, float(gi), sel)
        best = jnp.where(better, scores[gi], best)

    hot1, hot2 = [], []
    for gi in range(N_EXPERT_GROUPS):
        vals = rows[EXPERTS_PER_GROUP * gi:EXPERTS_PER_GROUP * (gi + 1)]
        chosen = sel == float(gi)
        for j in range(EXPERTS_PER_GROUP):
            ahead = jnp.zeros_like(best)
            for k in range(EXPERTS_PER_GROUP):
                if k < j:
                    ahead = ahead + jnp.where(vals[k] >= vals[j], 1.0, 0.0)
                elif k > j:
                    ahead = ahead + jnp.where(vals[k] > vals[j], 1.0, 0.0)
            hot1.append(jnp.where(chosen, jnp.where(ahead == 0.0, 1.0, 0.0), 0.0))
            hot2.append(jnp.where(chosen, jnp.where(ahead == 1.0, 1.0, 0.0), 0.0))
    return jnp.concatenate(hot1, axis=0), jnp.concatenate(hot2, axis=0), aff


def _pack_halves(y):
    half = y.shape[1] // 2
    lo = lax.bitcast_convert_type(y[:, :half].astype(BF16).astype(F32), jnp.int32)
    hi = lax.bitcast_convert_type(y[:, half:].astype(BF16).astype(F32), jnp.int32)
    return lax.shift_right_logical(lo, 16) | (hi & HI_HALF_MASK)


def _unpack_halves(w):
    lo = lax.bitcast_convert_type(lax.shift_left(w, 16), F32)
    hi = lax.bitcast_convert_type(w & HI_HALF_MASK, F32)
    return lo, hi


def _route_rows(x, wr_ref, br_ref, triu_ref, xp_ref, meta_ref, metat_ref, cnt_ref, running):
    tm = x.shape[0]
    xp_ref[...] = _pack_halves(x)
    logits_t = lax.dot_general(wr_ref[...], x.astype(BF16), (((1,), (1,)), ((), ())),
                               preferred_element_type=F32) + br_ref[...]
    hot1, hot2, aff = _route(logits_t)
    eid = lax.broadcasted_iota(jnp.int32, (N_EXPERTS, tm), 0).astype(F32)
    both = hot1 + hot2
    before = _dot(both.astype(BF16), triu_ref[...]) + running[...]

    def pick(hot, vals):
        return jnp.sum(hot * vals, axis=0, keepdims=True)

    v1 = pick(hot1, aff)
    v2 = pick(hot2, aff)
    denom = v1 + v2
    meta_t = jnp.concatenate(
        [pick(hot1, eid), pick(hot2, eid), v1 / denom, v2 / denom, pick(hot1, before), pick(hot2, before),
         jnp.zeros((LANES - 6, tm), F32)], axis=0)
    metat_ref[...] = meta_t[0:META_COLS, :]
    meta_ref[...] = meta_t.T[:, 0:META_COLS]
    running[...] = running[...] + jnp.sum(both, axis=1, keepdims=True)
    cnt_ref[...] = running[...]


def _expert_kernel(te_ref, nxt_ref, first_ref, nt_ref, xs_ref, wup_hbm, bup_ref, wdn_hbm, ys_ref,
                   wup_f32, wdn_f32, wup_bf, wdn_bf, sems, slot_ref, *, layer):
    i = pl.program_id(0)

    def weight_copies(e, slot):
        return (pltpu.make_async_copy(wup_hbm.at[layer, e], wup_f32.at[slot], sems.at[0, slot]),
                pltpu.make_async_copy(wdn_hbm.at[layer, e], wdn_f32.at[slot], sems.at[1, slot]))

    @pl.when(i == 0)
    def _():
        slot_ref[0] = 0
        for c in weight_copies(te_ref[0], 0):
            c.start()

    @pl.when(first_ref[i] == 1)
    def _():
        slot = slot_ref[0]
        for c in weight_copies(te_ref[i], slot):
            c.wait()
        wup_bf[...] = wup_f32[slot].astype(BF16)
        wdn_bf[...] = wdn_f32[slot].astype(BF16)

        @pl.when(nxt_ref[i] >= 0)
        def _():
            for c in weight_copies(nxt_ref[i], 1 - slot):
                c.start()

        slot_ref[0] = 1 - slot

    @pl.when(i < nt_ref[0])
    def _():
        half = D_MODEL // 2
        rt = xs_ref.shape[0]
        sub = min(EXPERT_SUB_ROWS, rt)
        for r in range(rt // sub):
            rows = slice(r * sub, (r + 1) * sub)
            lo, hi = _unpack_halves(xs_ref[rows, :])
            h = (_dot(lo.astype(BF16), wup_bf[0:half, :])
                 + _dot(hi.astype(BF16), wup_bf[half:D_MODEL, :]) + bup_ref[0])
            ys_ref[rows, :] = _pack_halves(_dot(_gelu_tanh(h).astype(BF16), wdn_bf[...]))


def _expert_call(xs, tile_expert, next_expert, first_tile, n_tiles, lw, *, rt):
    rows = xs.shape[0]

    def row_map(i, te, nxt, first, nt):
        return (jnp.minimum(i, nt[0] - 1), 0)

    return pl.pallas_call(
        functools.partial(_expert_kernel, layer=lw["layer"]),
        grid_spec=pltpu.PrefetchScalarGridSpec(
            num_scalar_prefetch=4,
            grid=(rows // rt,),
            in_specs=[
                pl.BlockSpec((rt, D_MODEL // 2), row_map),
                pl.BlockSpec(memory_space=pl.ANY),
                pl.BlockSpec((None, 1, 1, D_EXPERT),
                             lambda i, te, nxt, first, nt: (lw["layer"], te[i], 0, 0)),
                pl.BlockSpec(memory_space=pl.ANY),
            ],
            out_specs=pl.BlockSpec((rt, D_MODEL // 2), row_map),
            scratch_shapes=[pltpu.VMEM((2, D_MODEL, D_EXPERT), F32),
                            pltpu.VMEM((2, D_EXPERT, D_MODEL), F32),
                            pltpu.VMEM((D_MODEL, D_EXPERT), BF16),
                            pltpu.VMEM((D_EXPERT, D_MODEL), BF16),
                            pltpu.SemaphoreType.DMA((2, 2)),
                            pltpu.SMEM((1,), jnp.int32)],
        ),
        out_shape=jax.ShapeDtypeStruct((rows, D_MODEL // 2), jnp.int32),
        compiler_params=pltpu.CompilerParams(
            dimension_semantics=("arbitrary",), vmem_limit_bytes=VMEM_LIMIT_BYTES),
        name="experts",
    )(tile_expert, next_expert, first_tile, n_tiles, xs, lw["w_up"], lw["b_up"], lw["w_down"])


def _moe_combine(x1, yg_ref, meta, g, b, alpha):
    g1 = meta[:, 2:3]
    g2 = meta[:, 3:4]
    lo1, hi1 = _unpack_halves(yg_ref[0])
    lo2, hi2 = _unpack_halves(yg_ref[1])
    moe = jnp.concatenate([g1 * lo1 + g2 * lo2, g1 * hi1 + g2 * hi2], axis=1)
    return _layer_norm(alpha * x1 + moe, g, b)


def _combine_kernel(x_ref, yg_ref, meta_ref, g2_ref, b2_ref, *rest, alpha):
    o_ref, yslab = rest[-2:]
    y = _moe_combine(x_ref[...], yg_ref, meta_ref[...], g2_ref[...], b2_ref[...], alpha)
    nslab = D_MODEL // LANES
    steps = y.shape[0] // SEQS_PER_STREAM
    for j in range(nslab):
        yslab[j] = y[:, j * LANES:(j + 1) * LANES]
    for s in range(SEQS_PER_STREAM):
        for j in range(nslab):
            o_ref[s, :, j * LANES:(j + 1) * LANES] = yslab[j, pl.ds(s, steps, stride=SEQS_PER_STREAM), :]


def _combine_call(x1, yg, meta, lw, *, alpha, seq_len, step_off=0, prev_out=None):
    nq, srows, _ = x1.shape
    tmc = min(1024, srows)
    tps = srows // tmc
    steps = tmc // SEQS_PER_STREAM
    blk_off = step_off // steps
    in_specs = [
        pl.BlockSpec((tmc, D_MODEL), lambda i: (i, 0)),
        pl.BlockSpec((2, tmc, D_MODEL // 2), lambda i: (0, i, 0)),
        pl.BlockSpec((tmc, META_COLS), lambda i: (i, 0)),
        pl.BlockSpec((None, 1, D_MODEL), lambda i: (lw["layer"], 0, 0)),
        pl.BlockSpec((None, 1, D_MODEL), lambda i: (lw["layer"], 0, 0)),
    ]
    operands = [x1.reshape(nq * srows, D_MODEL), yg, meta, lw["g2"], lw["b2"]]
    aliases = {}
    if prev_out is not None:
        in_specs.append(pl.BlockSpec(memory_space=pl.ANY))
        operands.append(prev_out)
        aliases = {len(operands) - 1: 0}
    return pl.pallas_call(
        functools.partial(_combine_kernel, alpha=alpha),
        grid=(nq * tps,),
        in_specs=in_specs,
        out_specs=pl.BlockSpec((SEQS_PER_STREAM, steps, D_MODEL),
                               lambda i: (i // tps, i % tps + blk_off, 0)),
        out_shape=jax.ShapeDtypeStruct((nq * SEQS_PER_STREAM, seq_len, D_MODEL), F32),
        scratch_shapes=[pltpu.VMEM((D_MODEL // LANES, tmc, LANES), F32)],
        input_output_aliases=aliases,
        compiler_params=pltpu.CompilerParams(
            dimension_semantics=("arbitrary",), vmem_limit_bytes=VMEM_LIMIT_BYTES),
        name="combine",
    )(*operands)


def _sc_mesh():
    return plsc.VectorSubcoreMesh(core_axis_name="c", subcore_axis_name="s")


def _sc_chunk(rows):
    per_worker = rows // SC_WORKERS
    chunk = min(SC_MAX_CHUNK, per_worker)
    assert per_worker % chunk == 0 and chunk % 8 == 0
    return per_worker // chunk, chunk


def _sc_dispatch(xps, poss, n_slots):
    width = xps[0].shape[1]
    ngroups = len(xps)
    plans = [_sc_chunk(x.shape[0]) for x in xps]
    scratch = []
    for _, ch in plans:
        scratch += [pltpu.VMEM((ch,), jnp.int32), pltpu.VMEM((ch,), jnp.int32),
                    pltpu.VMEM((ch, width), jnp.int32)]

    @functools.partial(
        pl.kernel, mesh=_sc_mesh(),
        out_type=jax.ShapeDtypeStruct((n_slots, width), jnp.int32),
        scratch_types=scratch + [pltpu.SemaphoreType.DMA],
        name="sc_dispatch")
    def k(*refs):
        x_hbms, pos_hbms, o_hbm = refs[:ngroups], refs[ngroups:2 * ngroups], refs[2 * ngroups]
        bufs, sem = refs[2 * ngroups + 1:-1], refs[-1]
        wid = lax.axis_index("s") * SC_CORES + lax.axis_index("c")
        for g, (nch, ch) in enumerate(plans):
            idx0, idx1, buf = bufs[3 * g:3 * g + 3]

            @pl.loop(0, nch)
            def _(c, g=g, nch=nch, ch=ch, idx0=idx0, idx1=idx1, buf=buf):
                base = (wid * nch + c) * ch
                pltpu.sync_copy(x_hbms[g].at[pl.ds(base, ch)], buf)
                pltpu.sync_copy(pos_hbms[g].at[0, pl.ds(base, ch)], idx0)
                pltpu.sync_copy(pos_hbms[g].at[1, pl.ds(base, ch)], idx1)
                pltpu.async_copy(buf, o_hbm.at[idx0], sem).wait()
                pltpu.async_copy(buf, o_hbm.at[idx1], sem).wait()

    return k(*xps, *poss)


def _sc_gather(ys, poss):
    width = ys.shape[1]
    ngroups = len(poss)
    plans = [_sc_chunk(p.shape[1]) for p in poss]
    scratch = []
    for _, ch in plans:
        scratch += [pltpu.VMEM((ch,), jnp.int32), pltpu.VMEM((ch, width), jnp.int32)]

    @functools.partial(
        pl.kernel, mesh=_sc_mesh(),
        out_type=[jax.ShapeDtypeStruct((2, p.shape[1], width), jnp.int32) for p in poss],
        scratch_types=scratch + [pltpu.SemaphoreType.DMA],
        name="sc_gather")
    def k(*refs):
        y_hbm, pos_hbms = refs[0], refs[1:1 + ngroups]
        o_hbms = refs[1 + ngroups:1 + 2 * ngroups]
        bufs, sem = refs[1 + 2 * ngroups:-1], refs[-1]
        wid = lax.axis_index("s") * SC_CORES + lax.axis_index("c")
        for g, (nch, ch) in enumerate(plans):
            idx, buf = bufs[2 * g:2 * g + 2]

            @pl.loop(0, nch)
            def _(c, g=g, nch=nch, ch=ch, idx=idx, buf=buf):
                base = (wid * nch + c) * ch
                for kk in range(2):
                    pltpu.sync_copy(pos_hbms[g].at[kk, pl.ds(base, ch)], idx)
                    pltpu.async_copy(y_hbm.at[idx], buf, sem).wait()
                    pltpu.sync_copy(buf, o_hbms[g].at[kk, pl.ds(base, ch)])

    return k(ys, *poss)


def _moe_rows(xps, meta_ts, counts, lw):
    total = sum(x.shape[0] for x in xps)
    rt = 512 if total >= 8192 else 128
    n_slots = 2 * total + N_EXPERTS * rt
    nt_max = n_slots // rt

    cnt = counts[:, 0].astype(jnp.int32)
    padded = ((cnt + rt - 1) // rt) * rt
    ends = jnp.cumsum(padded)
    offs = ends - padded
    experts = jnp.arange(N_EXPERTS, dtype=jnp.int32)[:, None]
    poss = []
    for x, meta_t in zip(xps, meta_ts):
        if meta_t.ndim == 3:
            meta_t = meta_t.transpose(1, 0, 2).reshape(META_COLS, x.shape[0])
        eidx = meta_t[0:2].astype(jnp.int32)
        rank = meta_t[4:6].astype(jnp.int32)
        poss.append(jnp.sum(jnp.where(eidx[:, None, :] == experts, offs[:, None], 0), axis=1) + rank)
    n_tiles = ends[-1:] // rt
    tiles = jnp.minimum(jnp.arange(nt_max, dtype=jnp.int32), n_tiles[0] - 1)
    tile_expert = jnp.minimum(jnp.sum((ends // rt)[None, :] <= tiles[:, None], axis=1),
                              N_EXPERTS - 1).astype(jnp.int32)

    first_tile = jnp.concatenate([jnp.ones((1,), jnp.int32),
                                  (tile_expert[1:] != tile_expert[:-1]).astype(jnp.int32)])
    eid = jnp.arange(N_EXPERTS, dtype=jnp.int32)
    later = (eid[None, :] > eid[:, None]) & (cnt[None, :] > 0)
    next_of = jnp.min(jnp.where(later, eid[None, :], N_EXPERTS), axis=1)
    next_of = jnp.where(next_of == N_EXPERTS, -1, next_of)
    next_expert = jnp.sum(jnp.where(tile_expert[:, None] == eid[None, :], next_of[None, :], 0),
                          axis=1).astype(jnp.int32)

    xs = _sc_dispatch(xps, poss, n_slots)
    ys = _expert_call(xs, tile_expert, next_expert, first_tile, n_tiles.astype(jnp.int32), lw, rt=rt)
    return _sc_gather(ys, poss)


def _memkv_kernel(m_ref, wk_ref, wv_ref, k_ref, v_ref):
    mb = m_ref[...].astype(BF16)
    k_ref[0] = _dot(mb, wk_ref[0])
    v_ref[0] = _dot(mb, wv_ref[0])


def _memkv_call(mem, wk, wv):
    depth = wk.shape[0]
    rows = mem.shape[0]
    return pl.pallas_call(
        _memkv_kernel,
        grid=(depth,),
        in_specs=[
            pl.BlockSpec((rows, D_MODEL), lambda l: (0, 0)),
            pl.BlockSpec((1, D_MODEL, D_ATT), lambda l: (l, 0, 0)),
            pl.BlockSpec((1, D_MODEL, D_ATT), lambda l: (l, 0, 0)),
        ],
        out_specs=[
            pl.BlockSpec((1, rows, D_ATT), lambda l: (l, 0, 0)),
            pl.BlockSpec((1, rows, D_ATT), lambda l: (l, 0, 0)),
        ],
        out_shape=[jax.ShapeDtypeStruct((depth, rows, D_ATT), F32)] * 2,
        compiler_params=pltpu.CompilerParams(
            dimension_semantics=("arbitrary",), vmem_limit_bytes=VMEM_LIMIT_BYTES),
        name="memkv",
    )(mem, wk, wv)


def _perm_matrices(lc):
    tm = lc * SEQS_PER_STREAM
    p = np.zeros((tm, tm), np.float32)
    for s in range(SEQS_PER_STREAM):
        for t in range(lc):
            p[s * lc + t, t * SEQS_PER_STREAM + s] = 1.0
    return jnp.asarray(p, BF16), jnp.asarray(p.T, BF16)


def _to_streams(a):
    b, l, c = a.shape
    q = b // SEQS_PER_STREAM
    return a.reshape(q, SEQS_PER_STREAM, l, c).transpose(0, 2, 1, 3).reshape(q, l * SEQS_PER_STREAM, c)


def _from_streams(a, l):
    q, _, c = a.shape
    return a.reshape(q, l, SEQS_PER_STREAM, c).transpose(0, 2, 1, 3).reshape(q * SEQS_PER_STREAM, l, c)


def _pack_state(re, im):
    b = re.shape[0]
    q = b // SEQS_PER_STREAM
    re = re.reshape(q, SEQS_PER_STREAM, D_STATE)
    im = im.reshape(q, SEQS_PER_STREAM, D_STATE)
    return jnp.concatenate([im, re], axis=1)


def _unpack_state(h):
    q = h.shape[0]
    im = h[:, 0:SEQS_PER_STREAM].reshape(q * SEQS_PER_STREAM, N_SSM_GROUPS, SSM_STATE)
    re = h[:, SEQS_PER_STREAM:].reshape(q * SEQS_PER_STREAM, N_SSM_GROUPS, SSM_STATE)
    return re, im


def _pad_heads(mk, mv):
    b = mk.shape[0]
    q = b // SEQS_PER_STREAM
    shape = (q, SEQS_PER_STREAM, N_MEM, D_ATT)
    return mk.reshape(shape).transpose(0, 1, 3, 2).astype(BF16), mv.reshape(shape).astype(BF16)


def _model_params(w_in, w_dw, b_dw, conv_ln_g, conv_ln_b, ssm_a_re, ssm_a_im, ssm_b_re,
                  ssm_b_im, ssm_c_re, ssm_c_im, ssm_d, ssm_log_dt, ssm_w_glu, ssm_b_glu,
                  w_out, ln1_g, ln1_b, w_up, b_up, w_down, ln2_g, ln2_b):
    depth = w_in.shape[0]
    a = lax.complex(ssm_a_re, ssm_a_im)
    dt = jnp.exp(ssm_log_dt)[..., None]
    a_bar = jnp.exp(a * dt)
    b_bar = ((a_bar - 1.0) / a)[..., None] * lax.complex(ssm_b_re, ssm_b_im)
    eye = jnp.eye(N_SSM_GROUPS, dtype=F32)

    def b_block(m):
        full = jnp.einsum("lgpi,gh->lgihp", m, eye).reshape(depth, D_SSM, D_STATE).astype(BF16)
        return jnp.stack([full[:, k * LANES:(k + 1) * LANES, k * SCAN_LANES:(k + 1) * SCAN_LANES]
                          for k in range(SSM_BLOCKS)], axis=1)

    def c_block(m):
        full = jnp.einsum("lgip,gh->lgphi", m, eye).reshape(depth, D_STATE, D_SSM).astype(BF16)
        return jnp.stack([full[:, k * SCAN_LANES:(k + 1) * SCAN_LANES, k * LANES:(k + 1) * LANES]
                          for k in range(SSM_BLOCKS)], axis=1)

    ar = jnp.real(a_bar).reshape(depth, 1, D_STATE)
    ai = jnp.imag(a_bar).reshape(depth, 1, D_STATE)
    half = SUBLANES // 2
    return {
        "w_in": w_in.astype(BF16),
        "wdw": jnp.repeat(w_dw, SUBLANES, axis=1),
        "bdw": b_dw[:, None], "clg": conv_ln_g[:, None], "clb": conv_ln_b[:, None],
        "a1": jnp.broadcast_to(ar, (depth, SUBLANES, D_STATE)),
        "a2": jnp.concatenate([jnp.broadcast_to(-ai, (depth, half, D_STATE)),
                               jnp.broadcast_to(ai, (depth, half, D_STATE))], axis=1),
        "bre": b_block(jnp.real(b_bar)), "bim": b_block(jnp.imag(b_bar)),
        "cre": c_block(ssm_c_re), "cim": c_block(-ssm_c_im),
        "d": ssm_d[:, None], "wglu": ssm_w_glu.astype(BF16), "bglu": ssm_b_glu[:, None],
        "w_out": w_out.astype(BF16), "g1": ln1_g[:, None], "b1": ln1_b[:, None],
        "w_up": w_up, "b_up": b_up[:, :, None, :], "w_down": w_down,
        "g2": ln2_g[:, None], "b2": ln2_b[:, None],
    }


def kernel(x_prompt, x_sample, cache_conv, state_ssm_re, state_ssm_im, cache_mem_k, cache_mem_v, mem_prompt, w_in, w_dw, b_dw, conv_ln_g, conv_ln_b, ssm_a_re, ssm_a_im, ssm_b_re, ssm_b_im, ssm_c_re, ssm_c_im, ssm_d, ssm_log_dt, ssm_w_glu, ssm_b_glu, w_mem_k, w_mem_v, w_out, ln1_g, ln1_b, w_router, b_router, w_up, b_up, w_down, ln2_g, ln2_b):
    depth = w_in.shape[0]
    alpha = (2.0 * depth) ** 0.25
    bp, seq, _ = x_prompt.shape
    bs, dec_seq, _ = x_sample.shape
    assert bp == SEQS_PER_STREAM and bs % SEQS_PER_STREAM == 0
    lc_p = min(128, seq)
    lc_s = dec_seq
    assert seq % lc_p == 0 and lc_p % 16 == 0 and lc_s % 16 == 0

    mb, mm, _ = mem_prompt.shape
    mk_all, mv_all = _memkv_call(mem_prompt.reshape(mb * mm, D_MODEL),
                                 w_mem_k.astype(BF16), w_mem_v.astype(BF16))
    mk_all = mk_all.reshape(depth, mb, mm, N_MEM_HEADS, MEM_HEAD_DIM)
    mv_all = mv_all.reshape(depth, mb, mm, N_MEM_HEADS, MEM_HEAD_DIM)

    perm_p, permt_p = _perm_matrices(lc_p)
    perm_s, permt_s = _perm_matrices(lc_s)
    wr = w_router.T.astype(BF16)
    br = b_router[:, None]

    tiles_p = seq // lc_p
    tiles_a = max(tiles_p // 2, 1)
    tiles_b = tiles_p - tiles_a
    steps_a = tiles_a * lc_p
    xa = xb = x_prompt
    xs = x_sample
    zero_hist = jnp.zeros((1, HIST_ROWS, D_CONV), F32)
    zero_h = jnp.zeros((1, SUBLANES, D_STATE), F32)
    zero_cnt = jnp.zeros((N_EXPERTS, 1), F32)

    params = _model_params(w_in, w_dw, b_dw, conv_ln_g, conv_ln_b, ssm_a_re, ssm_a_im, ssm_b_re,
                           ssm_b_im, ssm_c_re, ssm_c_im, ssm_d, ssm_log_dt, ssm_w_glu, ssm_b_glu,
                           w_out, ln1_g, ln1_b, w_up, b_up, w_down, ln2_g, ln2_b)
    nq_p = bp // SEQS_PER_STREAM
    nq_s = bs // SEQS_PER_STREAM
    head_shape = (mm, N_MEM_HEADS, MEM_HEAD_DIM)
    kp_p, vp_p = _pad_heads(mk_all.reshape((depth * mb,) + head_shape),
                            mv_all.reshape((depth * mb,) + head_shape))
    kp_s, vp_s = _pad_heads(cache_mem_k.reshape((depth * bs,) + head_shape),
                            cache_mem_v.reshape((depth * bs,) + head_shape))
    hist_s = _to_streams(cache_conv.reshape(depth * bs, CONV_BUF, D_CONV))
    h_s = _pack_state(state_ssm_re.reshape(depth * bs, N_SSM_GROUPS, SSM_STATE),
                      state_ssm_im.reshape(depth * bs, N_SSM_GROUPS, SSM_STATE))

    hists_p, hsts_p, hists_s, hsts_s = [], [], [], []
    moe_a = moe_b = moe_s = None
    for l in range(depth):
        lw = dict(params, layer=l)
        xa, hist, hst, rows_a, meta_a, metat_a, cnt_a = _mixer_call(
            xa, moe_a, zero_hist, zero_h, kp_p, vp_p, perm_p, permt_p, lw, wr, br, zero_cnt,
            lc=lc_p, alpha=alpha, tile_off=0, n_tiles=tiles_a, kv_off=l * nq_p)
        (yg_a,) = _moe_rows([rows_a], [metat_a], cnt_a, lw)
        moe_a = (yg_a, meta_a)
        xb, hist, hst, rows_b, meta_b, metat_b, cnt_b = _mixer_call(
            xb, moe_b, hist, hst, kp_p, vp_p, perm_p, permt_p, lw, wr, br, zero_cnt,
            lc=lc_p, alpha=alpha, tile_off=tiles_a, n_tiles=tiles_b, kv_off=l * nq_p)
        hists_p.append(hist)
        hsts_p.append(hst)
        xs, hist, hst, rows_s, meta_s, metat_s, cnt_s = _mixer_call(
            xs, moe_s, hist_s, h_s, kp_s, vp_s, perm_s, permt_s, lw, wr, br, cnt_b,
            lc=lc_s, alpha=alpha, state_off=l * nq_s, kv_off=l * nq_s)
        hists_s.append(hist)
        hsts_s.append(hst)
        yg_b, yg_s = _moe_rows([rows_b, rows_s], [metat_b, metat_s], cnt_s, lw)
        moe_b = (yg_b, meta_b)
        moe_s = (yg_s, meta_s)

    y_prompt = _combine_call(xa, moe_a[0], moe_a[1], lw, alpha=alpha, seq_len=seq)
    y_prompt = _combine_call(xb, moe_b[0], moe_b[1], lw, alpha=alpha, seq_len=seq, step_off=steps_a,
                             prev_out=y_prompt)
    y_sample = _combine_call(xs, moe_s[0], moe_s[1], lw, alpha=alpha, seq_len=dec_seq)

    def states(hists, hsts, batch):
        conv = _from_streams(jnp.concatenate(hists, axis=0), CONV_BUF)
        re, im = _unpack_state(jnp.concatenate(hsts, axis=0))
        return (conv.reshape(depth, batch, CONV_BUF, D_CONV),
                re.reshape(depth, batch, N_SSM_GROUPS, SSM_STATE),
                im.reshape(depth, batch, N_SSM_GROUPS, SSM_STATE))

    conv_p, re_p, im_p = states(hists_p, hsts_p, bp)
    conv_s, re_s, im_s = states(hists_s, hsts_s, bs)
    return (y_prompt, y_sample, conv_p, re_p, im_p, mk_all, mv_all, conv_s, re_s, im_s)
```

```python
import functools
import math

import numpy as np
import jax
import jax.numpy as jnp
from jax import lax
from jax.experimental import pallas as pl
from jax.experimental.pallas import tpu as pltpu
from jax.experimental.pallas import tpu_sc as plsc

F32 = jnp.float32
BF16 = jnp.bfloat16

D_MODEL = 1024
CONV_WIDTH = 31
CONV_BUF = CONV_WIDTH - 1
D_CONV = 384
D_SSM = 384
SSM_GROUP = 16
N_SSM_GROUPS = D_SSM // SSM_GROUP
SSM_STATE = 64
D_STATE = N_SSM_GROUPS * SSM_STATE
N_MEM = 256
N_MEM_HEADS = 4
MEM_HEAD_DIM = 64
D_ATT = N_MEM_HEADS * MEM_HEAD_DIM
D_IN = 2 * D_CONV + D_SSM + D_ATT
N_EXPERTS = 16
N_EXPERT_GROUPS = 4
EXPERTS_PER_GROUP = N_EXPERTS // N_EXPERT_GROUPS
D_EXPERT = 512
LN_EPS = 1e-5

SEQS_PER_STREAM = 4
HIST_ROWS = CONV_BUF * SEQS_PER_STREAM
SUBLANES = 8
LANES = 128
SCAN_LANES = 512
SSM_BLOCKS = D_STATE // SCAN_LANES
CONV_ROWS = 32
VMEM_LIMIT_BYTES = 56 * 1024 * 1024
HI_HALF_MASK = np.int32(-65536)
META_COLS = 8
SC_CORES = 2
SC_WORKERS = SC_CORES * 16
SC_MAX_CHUNK = 128


def _sigmoid(x):
    return 1.0 / (1.0 + jnp.exp(-x))


def _gelu_tanh(x):
    c = math.sqrt(2.0 / math.pi)
    return 0.5 * x * (1.0 + jnp.tanh(c * (x + 0.044715 * (x * x * x))))


def _layer_norm(z, g, b):
    mu = jnp.mean(z, axis=-1, keepdims=True)
    zc = z - mu
    var = jnp.mean(zc * zc, axis=-1, keepdims=True)
    return zc * lax.rsqrt(var + LN_EPS) * g + b


def _dot(a, b):
    return jnp.dot(a, b, preferred_element_type=F32)


def _mixer_kernel(*refs, tm, lc, alpha, fuse_in):
    refs = list(refs)
    if fuse_in:
        xprev_ref, yg_ref, metain_ref, g2p_ref, b2p_ref = refs[:5]
        refs = refs[5:]
    else:
        xprev_ref = refs.pop(0)
        xslab = refs.pop()
    (hist0_ref, h0_ref, k_ref, v_ref, perm_ref, permt_ref,
     w_in_ref, wdw_ref, bdw_ref, clg_ref, clb_ref, a1_ref, a2_ref,
     bre_ref, bim_ref, cre_ref, cim_ref, d_ref, wglu_ref, bglu_ref,
     wout_ref, g1_ref, b1_ref, wr_ref, br_ref, tri_ref, cnt0_ref,
     x1_ref, hist_out_ref, h_out_ref, xp_ref, meta_ref, metat_ref, cnt_ref,
     xpad0, xpad4, cy, ush, yim, bu_re, bu_im, hre, him, hcar, xin, running) = refs
    i = pl.program_id(1)

    @pl.when(i == 0)
    def _():
        xpad0[0:HIST_ROWS, :] = hist0_ref[0]
        hcar[...] = h0_ref[0]
        ush[...] = jnp.zeros_like(ush)

    @pl.when((i == 0) & (pl.program_id(0) == 0))
    def _():
        running[...] = cnt0_ref[...]

    if fuse_in:
        xin[...] = _moe_combine(xprev_ref[0], yg_ref, metain_ref[...], g2p_ref[...], b2p_ref[...], alpha)
    else:
        nslab = D_MODEL // LANES
        for s in range(SEQS_PER_STREAM):
            for j in range(nslab):
                xslab[j, pl.ds(s, lc, stride=SEQS_PER_STREAM), :] = xprev_ref[s, :, j * LANES:(j + 1) * LANES]
        xin[...] = jnp.concatenate([xslab[j] for j in range(nslab)], axis=1)
    x = xin[...]
    proj = _dot(x.astype(BF16), w_in_ref[...])

    g = proj[:, 0:D_CONV] * _sigmoid(proj[:, D_CONV:2 * D_CONV])
    xpad0[HIST_ROWS:HIST_ROWS + tm, :] = g
    xpad4[0:HIST_ROWS + tm - 4, :] = xpad0[4:HIST_ROWS + tm, :]

    nsub = CONV_ROWS // SUBLANES

    def conv_rows(rb, carry):
        r0 = pl.multiple_of(rb * CONV_ROWS, CONV_ROWS)
        accs = [jnp.broadcast_to(bdw_ref[...], (SUBLANES, D_CONV)) for _ in range(nsub)]
        for k in range(CONV_WIDTH):
            wk = wdw_ref[SUBLANES * k:SUBLANES * (k + 1), :]
            for sb in range(nsub):
                off = r0 + SEQS_PER_STREAM * k + SUBLANES * sb
                if k % 2 == 0:
                    xs = xpad0[pl.ds(pl.multiple_of(off, SUBLANES), SUBLANES), :]
                else:
                    xs = xpad4[pl.ds(pl.multiple_of(off - 4, SUBLANES), SUBLANES), :]
                accs[sb] = accs[sb] + xs * wk
        for sb in range(nsub):
            cy[pl.ds(pl.multiple_of(r0 + SUBLANES * sb, SUBLANES), SUBLANES), :] = accs[sb]
        return carry

    lax.fori_loop(0, tm // CONV_ROWS, conv_rows, 0, unroll=True)
    conv_n = _layer_norm(cy[...], clg_ref[...], clb_ref[...])
    cy[...] = conv_n * _sigmoid(conv_n)

    new_hist = xpad0[tm:tm + HIST_ROWS, :]
    xpad0[0:HIST_ROWS, :] = new_hist
    hist_out_ref[0] = new_hist

    u = proj[:, 2 * D_CONV:2 * D_CONV + D_SSM]
    ush[4:tm + 4, :] = u
    ub = u.astype(BF16)
    ub_sh = ush[...].astype(BF16)
    for m in range(SSM_BLOCKS):
        ch = slice(m * LANES, (m + 1) * LANES)
        st = slice(m * SCAN_LANES, (m + 1) * SCAN_LANES)
        bu_re[:, st] = _dot(ub[:, ch], bre_ref[m])
        bu_im[:, st] = _dot(ub_sh[:, ch], bim_ref[m])

    lo = lax.broadcasted_iota(jnp.int32, (SUBLANES, SCAN_LANES), 0) < 4
    for c in range(D_STATE // SCAN_LANES):
        cs = slice(c * SCAN_LANES, (c + 1) * SCAN_LANES)
        a1 = a1_ref[:, cs]
        a2 = a2_ref[:, cs]

        def scan_pair(j, carry, cs=cs, a1=a1, a2=a2):
            h_prev, im_cur = carry
            r = pl.multiple_of(j * SUBLANES, SUBLANES)
            re_cur = bu_re[pl.ds(r, SUBLANES), cs]
            im_next = bu_im[pl.ds(r + SUBLANES, SUBLANES), cs]
            p_even = jnp.where(lo, re_cur, im_cur)
            p_odd = jnp.where(lo, im_next, re_cur)
            h_even = a1 * pltpu.roll(h_prev, 4, 0) + a2 * h_prev + p_even
            h_odd = a1 * pltpu.roll(h_even, 4, 0) - a2 * h_even + p_odd
            hre[pl.ds(r, SUBLANES), cs] = jnp.where(lo, h_even, h_odd)
            him[pl.ds(r, SUBLANES), cs] = jnp.where(lo, h_prev, h_even)
            return h_odd, im_next

        h_last, _ = lax.fori_loop(0, lc // 2, scan_pair,
                                  (hcar[:, cs], bu_im[0:SUBLANES, cs]), unroll=True)
        him[tm:tm + SUBLANES, cs] = jnp.where(lo, h_last, 0.0)
        hcar[:, cs] = h_last
    h_out_ref[0] = hcar[...]

    y_re_blocks = []
    for m in range(SSM_BLOCKS):
        ch = slice(m * LANES, (m + 1) * LANES)
        st = slice(m * SCAN_LANES, (m + 1) * SCAN_LANES)
        y_re_blocks.append(_dot(hre[:, st].astype(BF16), cre_ref[m]))
        yim[:, ch] = _dot(him[:, st].astype(BF16), cim_ref[m])
    y_re = jnp.concatenate(y_re_blocks, axis=1)
    y = y_re + yim[4:tm + 4, :] + d_ref[...] * u
    y = _gelu_tanh(y)
    ssm_y = y * _sigmoid(_dot(y.astype(BF16), wglu_ref[...]) + bglu_ref[...])

    q = proj[:, 2 * D_CONV + D_SSM:D_IN].astype(BF16)
    q_seq = _dot(perm_ref[...], q)
    head_of_col = lax.shift_right_logical(lax.broadcasted_iota(jnp.int32, (1, D_ATT), 1), 6)
    head_masks = [jnp.where(head_of_col == h, 1.0, 0.0) for h in range(N_MEM_HEADS)]
    outs = []
    for s in range(SEQS_PER_STREAM):
        qs = q_seq[s * lc:(s + 1) * lc, :]
        ks = k_ref[0, s]
        vs = v_ref[0, s]
        acc = jnp.zeros((lc, D_ATT), F32)
        for h in range(N_MEM_HEADS):
            qh = (qs * head_masks[h]).astype(BF16)
            sc = _dot(qh, ks) * (MEM_HEAD_DIM ** -0.5)
            sc = sc - jnp.max(sc, axis=-1, keepdims=True)
            e = jnp.exp(sc)
            p = e * (1.0 / jnp.sum(e, axis=-1, keepdims=True))
            acc = acc + _dot(p.astype(BF16), vs) * head_masks[h]
        outs.append(acc)
    att_seq = jnp.concatenate(outs, axis=0).astype(BF16)
    att = _dot(permt_ref[...], att_seq).astype(BF16)

    mix = _dot(jnp.concatenate([cy[...].astype(BF16), ssm_y.astype(BF16), att], axis=1), wout_ref[...])
    x1 = _layer_norm(alpha * xin[...] + mix, g1_ref[...], b1_ref[...])
    x1_ref[0] = x1
    _route_rows(x1, wr_ref, br_ref, tri_ref, xp_ref, meta_ref, metat_ref, cnt_ref, running)


def _mixer_call(x, prev_moe, hist0, h0, kpad, vpad, perm, permt, lw, wr, br, cnt0, *, lc, alpha,
                tile_off=0, n_tiles=None, state_off=0, kv_off=0):
    fuse_in = prev_moe is not None
    tm = lc * SEQS_PER_STREAM
    if fuse_in:
        nq, rows, _ = x.shape
    else:
        nq = x.shape[0] // SEQS_PER_STREAM
        rows = (x.shape[1] // lc if n_tiles is None else n_tiles) * tm
    nt = rows // tm
    kern = functools.partial(_mixer_kernel, tm=tm, lc=lc, alpha=alpha, fuse_in=fuse_in)
    triu = jnp.asarray(np.triu(np.ones((tm, tm), np.float32), 1), BF16)

    def const(shape):
        return pl.BlockSpec(shape, lambda q, i: (0,) * len(shape))

    def flat(shape):
        return pl.BlockSpec(shape, lambda q, i: (0,) * (len(shape) - 2) + (q * nt + i, 0))

    layer = lw["layer"]

    def layered(shape, l=layer):
        return pl.BlockSpec((None,) + shape, lambda q, i: (l,) + (0,) * len(shape))

    operands = [x]
    if fuse_in:
        yg_prev, meta_prev = prev_moe
        in_specs = [pl.BlockSpec((1, tm, D_MODEL), lambda q, i: (q, i, 0)),
                    flat((2, tm, D_MODEL // 2)), flat((tm, META_COLS)),
                    layered((1, D_MODEL), layer - 1), layered((1, D_MODEL), layer - 1)]
        operands += [yg_prev, meta_prev, lw["g2"], lw["b2"]]
    else:
        in_specs = [pl.BlockSpec((SEQS_PER_STREAM, lc, D_MODEL), lambda q, i: (q, i + tile_off, 0))]
    in_specs += [
        pl.BlockSpec((1, HIST_ROWS, D_CONV), lambda q, i: (q + state_off, 0, 0)),
        pl.BlockSpec((1, SUBLANES, D_STATE), lambda q, i: (q + state_off, 0, 0)),
        pl.BlockSpec((1, SEQS_PER_STREAM, N_MEM, D_ATT), lambda q, i: (q + kv_off, 0, 0, 0)),
        pl.BlockSpec((1, SEQS_PER_STREAM, N_MEM, D_ATT), lambda q, i: (q + kv_off, 0, 0, 0)),
        const((tm, tm)), const((tm, tm)),
        layered((D_MODEL, D_IN)),
        layered((CONV_WIDTH * SUBLANES, D_CONV)), layered((1, D_CONV)), layered((1, D_CONV)),
        layered((1, D_CONV)),
        layered((SUBLANES, D_STATE)), layered((SUBLANES, D_STATE)),
        layered((SSM_BLOCKS, LANES, SCAN_LANES)), layered((SSM_BLOCKS, LANES, SCAN_LANES)),
        layered((SSM_BLOCKS, SCAN_LANES, LANES)), layered((SSM_BLOCKS, SCAN_LANES, LANES)),
        layered((1, D_SSM)), layered((D_SSM, D_SSM)), layered((1, D_SSM)),
        layered((D_MODEL, D_MODEL)), layered((1, D_MODEL)), layered((1, D_MODEL)),
        const((N_EXPERTS, D_MODEL)), const((N_EXPERTS, 1)), const((tm, tm)), const((N_EXPERTS, 1)),
    ]
    out_specs = [
        pl.BlockSpec((1, tm, D_MODEL), lambda q, i: (q, i, 0)),
        pl.BlockSpec((1, HIST_ROWS, D_CONV), lambda q, i: (q, 0, 0)),
        pl.BlockSpec((1, SUBLANES, D_STATE), lambda q, i: (q, 0, 0)),
        flat((tm, D_MODEL // 2)), flat((tm, META_COLS)),
        (pl.BlockSpec((META_COLS, tm), lambda q, i: (0, q * nt + i)) if tm % LANES == 0 else
         pl.BlockSpec((None, META_COLS, tm), lambda q, i: (q * nt + i, 0, 0))),
        const((N_EXPERTS, 1)),
    ]
    out_shape = [
        jax.ShapeDtypeStruct((nq, rows, D_MODEL), F32),
        jax.ShapeDtypeStruct((nq, HIST_ROWS, D_CONV), F32),
        jax.ShapeDtypeStruct((nq, SUBLANES, D_STATE), F32),
        jax.ShapeDtypeStruct((nq * rows, D_MODEL // 2), jnp.int32),
        jax.ShapeDtypeStruct((nq * rows, META_COLS), F32),
        jax.ShapeDtypeStruct((META_COLS, nq * rows) if tm % LANES == 0 else (nq * nt, META_COLS, tm), F32),
        jax.ShapeDtypeStruct((N_EXPERTS, 1), F32),
    ]
    scratch = [
        pltpu.VMEM((HIST_ROWS + tm + SUBLANES, D_CONV), F32),
        pltpu.VMEM((HIST_ROWS + tm + SUBLANES, D_CONV), F32),
        pltpu.VMEM((tm, D_CONV), F32),
        pltpu.VMEM((tm + SUBLANES, D_SSM), F32),
        pltpu.VMEM((tm + SUBLANES, D_SSM), F32),
        pltpu.VMEM((tm, D_STATE), F32),
        pltpu.VMEM((tm + SUBLANES, D_STATE), F32),
        pltpu.VMEM((tm, D_STATE), F32),
        pltpu.VMEM((tm + SUBLANES, D_STATE), F32),
        pltpu.VMEM((SUBLANES, D_STATE), F32),
        pltpu.VMEM((tm, D_MODEL), F32),
        pltpu.VMEM((N_EXPERTS, 1), F32),
    ]
    if not fuse_in:
        scratch.append(pltpu.VMEM((D_MODEL // LANES, tm, LANES), F32))
    return pl.pallas_call(
        kern,
        grid=(nq, nt),
        in_specs=in_specs,
        out_specs=out_specs,
        out_shape=out_shape,
        scratch_shapes=scratch,
        compiler_params=pltpu.CompilerParams(
            dimension_semantics=("arbitrary", "arbitrary"),
            vmem_limit_bytes=VMEM_LIMIT_BYTES),
        name="mixer",
    )(*operands, hist0, h0, kpad, vpad, perm, permt,
      lw["w_in"], lw["wdw"], lw["bdw"], lw["clg"], lw["clb"], lw["a1"], lw["a2"],
      lw["bre"], lw["bim"], lw["cre"], lw["cim"], lw["d"], lw["wglu"], lw["bglu"],
      lw["w_out"], lw["g1"], lw["b1"], wr, br, triu, cnt0)


def _route(logits_t):
    m = jnp.max(logits_t, axis=0, keepdims=True)
    e = jnp.exp(logits_t - m)
    aff = e / jnp.sum(e, axis=0, keepdims=True)
    rows = [aff[j:j + 1, :] for j in range(N_EXPERTS)]

    scores = []
    for gi in range(N_EXPERT_GROUPS):
        a, b, c, d = rows[EXPERTS_PER_GROUP * gi:EXPERTS_PER_GROUP * (gi + 1)]
        hi1, lo1 = jnp.maximum(a, b), jnp.minimum(a, b)
        hi2, lo2 = jnp.maximum(c, d), jnp.minimum(c, d)
        scores.append(jnp.maximum(hi1, hi2) + jnp.maximum(jnp.minimum(hi1, hi2), jnp.maximum(lo1, lo2)))
    best = scores[0]
    sel = jnp.zeros_like(best)
    for gi in range(1, N_EXPERT_GROUPS):
        better = scores[gi] > best
        sel = jnp.where(better, float(gi), sel)
        best = jnp.where(better, scores[gi], best)

    hot1, hot2 = [], []
    for gi in range(N_EXPERT_GROUPS):
        vals = rows[EXPERTS_PER_GROUP * gi:EXPERTS_PER_GROUP * (gi + 1)]
        chosen = sel == float(gi)
        for j in range(EXPERTS_PER_GROUP):
            ahead = jnp.zeros_like(best)
            for k in range(EXPERTS_PER_GROUP):
                if k < j:
                    ahead = ahead + jnp.where(vals[k] >= vals[j], 1.0, 0.0)
                elif k > j:
                    ahead = ahead + jnp.where(vals[k] > vals[j], 1.0, 0.0)
            hot1.append(jnp.where(chosen, jnp.where(ahead == 0.0, 1.0, 0.0), 0.0))
            hot2.append(jnp.where(chosen, jnp.where(ahead == 1.0, 1.0, 0.0), 0.0))
    return jnp.concatenate(hot1, axis=0), jnp.concatenate(hot2, axis=0), aff


def _pack_halves(y):
    half = y.shape[1] // 2
    lo = lax.bitcast_convert_type(y[:, :half].astype(BF16).astype(F32), jnp.int32)
    hi = lax.bitcast_convert_type(y[:, half:].astype(BF16).astype(F32), jnp.int32)
    return lax.shift_right_logical(lo, 16) | (hi & HI_HALF_MASK)


def _unpack_halves(w):
    lo = lax.bitcast_convert_type(lax.shift_left(w, 16), F32)
    hi = lax.bitcast_convert_type(w & HI_HALF_MASK, F32)
    return lo, hi


def _route_rows(x, wr_ref, br_ref, triu_ref, xp_ref, meta_ref, metat_ref, cnt_ref, running):
    tm = x.shape[0]
    xp_ref[...] = _pack_halves(x)
    logits_t = lax.dot_general(wr_ref[...], x.astype(BF16), (((1,), (1,)), ((), ())),
                               preferred_element_type=F32) + br_ref[...]
    hot1, hot2, aff = _route(logits_t)
    eid = lax.broadcasted_iota(jnp.int32, (N_EXPERTS, tm), 0).astype(F32)
    both = hot1 + hot2
    before = _dot(both.astype(BF16), triu_ref[...]) + running[...]

    def pick(hot, vals):
        return jnp.sum(hot * vals, axis=0, keepdims=True)

    v1 = pick(hot1, aff)
    v2 = pick(hot2, aff)
    denom = v1 + v2
    meta_t = jnp.concatenate(
        [pick(hot1, eid), pick(hot2, eid), v1 / denom, v2 / denom, pick(hot1, before), pick(hot2, before),
         jnp.zeros((LANES - 6, tm), F32)], axis=0)
    metat_ref[...] = meta_t[0:META_COLS, :]
    meta_ref[...] = meta_t.T[:, 0:META_COLS]
    running[...] = running[...] + jnp.sum(both, axis=1, keepdims=True)
    cnt_ref[...] = running[...]


def _expert_kernel(te_ref, nxt_ref, first_ref, nt_ref, xs_ref, wup_hbm, bup_ref, wdn_hbm, ys_ref,
                   wup_f32, wdn_f32, wup_bf, wdn_bf, sems, slot_ref, *, layer):
    i = pl.program_id(0)

    def weight_copies(e, slot):
        return (pltpu.make_async_copy(wup_hbm.at[layer, e], wup_f32.at[slot], sems.at[0, slot]),
                pltpu.make_async_copy(wdn_hbm.at[layer, e], wdn_f32.at[slot], sems.at[1, slot]))

    @pl.when(i == 0)
    def _():
        slot_ref[0] = 0
        for c in weight_copies(te_ref[0], 0):
            c.start()

    @pl.when(first_ref[i] == 1)
    def _():
        slot = slot_ref[0]
        for c in weight_copies(te_ref[i], slot):
            c.wait()
        wup_bf[...] = wup_f32[slot].astype(BF16)
        wdn_bf[...] = wdn_f32[slot].astype(BF16)

        @pl.when(nxt_ref[i] >= 0)
        def _():
            for c in weight_copies(nxt_ref[i], 1 - slot):
                c.start()

        slot_ref[0] = 1 - slot

    @pl.when(i < nt_ref[0])
    def _():
        half = D_MODEL // 2
        lo, hi = _unpack_halves(xs_ref[...])
        h = (_dot(lo.astype(BF16), wup_bf[0:half, :])
             + _dot(hi.astype(BF16), wup_bf[half:D_MODEL, :]) + bup_ref[0])
        ys_ref[...] = _pack_halves(_dot(_gelu_tanh(h).astype(BF16), wdn_bf[...]))


def _expert_call(xs, tile_expert, next_expert, first_tile, n_tiles, lw, *, rt):
    rows = xs.shape[0]

    def row_map(i, te, nxt, first, nt):
        return (jnp.minimum(i, nt[0] - 1), 0)

    return pl.pallas_call(
        functools.partial(_expert_kernel, layer=lw["layer"]),
        grid_spec=pltpu.PrefetchScalarGridSpec(
            num_scalar_prefetch=4,
            grid=(rows // rt,),
            in_specs=[
                pl.BlockSpec((rt, D_MODEL // 2), row_map),
                pl.BlockSpec(memory_space=pl.ANY),
                pl.BlockSpec((None, 1, 1, D_EXPERT),
                             lambda i, te, nxt, first, nt: (lw["layer"], te[i], 0, 0)),
                pl.BlockSpec(memory_space=pl.ANY),
            ],
            out_specs=pl.BlockSpec((rt, D_MODEL // 2), row_map),
            scratch_shapes=[pltpu.VMEM((2, D_MODEL, D_EXPERT), F32),
                            pltpu.VMEM((2, D_EXPERT, D_MODEL), F32),
                            pltpu.VMEM((D_MODEL, D_EXPERT), BF16),
                            pltpu.VMEM((D_EXPERT, D_MODEL), BF16),
                            pltpu.SemaphoreType.DMA((2, 2)),
                            pltpu.SMEM((1,), jnp.int32)],
        ),
        out_shape=jax.ShapeDtypeStruct((rows, D_MODEL // 2), jnp.int32),
        compiler_params=pltpu.CompilerParams(
            dimension_semantics=("arbitrary",), vmem_limit_bytes=VMEM_LIMIT_BYTES),
        name="experts",
    )(tile_expert, next_expert, first_tile, n_tiles, xs, lw["w_up"], lw["b_up"], lw["w_down"])


def _moe_combine(x1, yg_ref, meta, g, b, alpha):
    g1 = meta[:, 2:3]
    g2 = meta[:, 3:4]
    lo1, hi1 = _unpack_halves(yg_ref[0])
    lo2, hi2 = _unpack_halves(yg_ref[1])
    moe = jnp.concatenate([g1 * lo1 + g2 * lo2, g1 * hi1 + g2 * hi2], axis=1)
    return _layer_norm(alpha * x1 + moe, g, b)


def _combine_kernel(x_ref, yg_ref, meta_ref, g2_ref, b2_ref, *rest, alpha):
    o_ref, yslab = rest[-2:]
    y = _moe_combine(x_ref[...], yg_ref, meta_ref[...], g2_ref[...], b2_ref[...], alpha)
    nslab = D_MODEL // LANES
    steps = y.shape[0] // SEQS_PER_STREAM
    for j in range(nslab):
        yslab[j] = y[:, j * LANES:(j + 1) * LANES]
    for s in range(SEQS_PER_STREAM):
        for j in range(nslab):
            o_ref[s, :, j * LANES:(j + 1) * LANES] = yslab[j, pl.ds(s, steps, stride=SEQS_PER_STREAM), :]


def _combine_call(x1, yg, meta, lw, *, alpha, seq_len, step_off=0, prev_out=None):
    nq, srows, _ = x1.shape
    tmc = min(1024, srows)
    tps = srows // tmc
    steps = tmc // SEQS_PER_STREAM
    blk_off = step_off // steps
    in_specs = [
        pl.BlockSpec((tmc, D_MODEL), lambda i: (i, 0)),
        pl.BlockSpec((2, tmc, D_MODEL // 2), lambda i: (0, i, 0)),
        pl.BlockSpec((tmc, META_COLS), lambda i: (i, 0)),
        pl.BlockSpec((None, 1, D_MODEL), lambda i: (lw["layer"], 0, 0)),
        pl.BlockSpec((None, 1, D_MODEL), lambda i: (lw["layer"], 0, 0)),
    ]
    operands = [x1.reshape(nq * srows, D_MODEL), yg, meta, lw["g2"], lw["b2"]]
    aliases = {}
    if prev_out is not None:
        in_specs.append(pl.BlockSpec(memory_space=pl.ANY))
        operands.append(prev_out)
        aliases = {len(operands) - 1: 0}
    return pl.pallas_call(
        functools.partial(_combine_kernel, alpha=alpha),
        grid=(nq * tps,),
        in_specs=in_specs,
        out_specs=pl.BlockSpec((SEQS_PER_STREAM, steps, D_MODEL),
                               lambda i: (i // tps, i % tps + blk_off, 0)),
        out_shape=jax.ShapeDtypeStruct((nq * SEQS_PER_STREAM, seq_len, D_MODEL), F32),
        scratch_shapes=[pltpu.VMEM((D_MODEL // LANES, tmc, LANES), F32)],
        input_output_aliases=aliases,
        compiler_params=pltpu.CompilerParams(
            dimension_semantics=("arbitrary",), vmem_limit_bytes=VMEM_LIMIT_BYTES),
        name="combine",
    )(*operands)


def _sc_mesh():
    return plsc.VectorSubcoreMesh(core_axis_name="c", subcore_axis_name="s")


def _sc_chunk(rows):
    per_worker = rows // SC_WORKERS
    chunk = min(SC_MAX_CHUNK, per_worker)
    assert per_worker % chunk == 0 and chunk % 8 == 0
    return per_worker // chunk, chunk


def _sc_dispatch(xps, poss, n_slots):
    width = xps[0].shape[1]
    ngroups = len(xps)
    plans = [_sc_chunk(x.shape[0]) for x in xps]
    scratch = []
    for _, ch in plans:
        scratch += [pltpu.VMEM((ch,), jnp.int32), pltpu.VMEM((ch,), jnp.int32),
                    pltpu.VMEM((ch, width), jnp.int32)]

    @functools.partial(
        pl.kernel, mesh=_sc_mesh(),
        out_type=jax.ShapeDtypeStruct((n_slots, width), jnp.int32),
        scratch_types=scratch + [pltpu.SemaphoreType.DMA],
        name="sc_dispatch")
    def k(*refs):
        x_hbms, pos_hbms, o_hbm = refs[:ngroups], refs[ngroups:2 * ngroups], refs[2 * ngroups]
        bufs, sem = refs[2 * ngroups + 1:-1], refs[-1]
        wid = lax.axis_index("s") * SC_CORES + lax.axis_index("c")
        for g, (nch, ch) in enumerate(plans):
            idx0, idx1, buf = bufs[3 * g:3 * g + 3]

            @pl.loop(0, nch)
            def _(c, g=g, nch=nch, ch=ch, idx0=idx0, idx1=idx1, buf=buf):
                base = (wid * nch + c) * ch
                pltpu.sync_copy(x_hbms[g].at[pl.ds(base, ch)], buf)
                pltpu.sync_copy(pos_hbms[g].at[0, pl.ds(base, ch)], idx0)
                pltpu.sync_copy(pos_hbms[g].at[1, pl.ds(base, ch)], idx1)
                pltpu.async_copy(buf, o_hbm.at[idx0], sem).wait()
                pltpu.async_copy(buf, o_hbm.at[idx1], sem).wait()

    return k(*xps, *poss)


def _sc_gather(ys, poss):
    width = ys.shape[1]
    ngroups = len(poss)
    plans = [_sc_chunk(p.shape[1]) for p in poss]
    scratch = []
    for _, ch in plans:
        scratch += [pltpu.VMEM((ch,), jnp.int32), pltpu.VMEM((ch, width), jnp.int32)]

    @functools.partial(
        pl.kernel, mesh=_sc_mesh(),
        out_type=[jax.ShapeDtypeStruct((2, p.shape[1], width), jnp.int32) for p in poss],
        scratch_types=scratch + [pltpu.SemaphoreType.DMA],
        name="sc_gather")
    def k(*refs):
        y_hbm, pos_hbms = refs[0], refs[1:1 + ngroups]
        o_hbms = refs[1 + ngroups:1 + 2 * ngroups]
        bufs, sem = refs[1 + 2 * ngroups:-1], refs[-1]
        wid = lax.axis_index("s") * SC_CORES + lax.axis_index("c")
        for g, (nch, ch) in enumerate(plans):
            idx, buf = bufs[2 * g:2 * g + 2]

            @pl.loop(0, nch)
            def _(c, g=g, nch=nch, ch=ch, idx=idx, buf=buf):
                base = (wid * nch + c) * ch
                for kk in range(2):
                    pltpu.sync_copy(pos_hbms[g].at[kk, pl.ds(base, ch)], idx)
                    pltpu.async_copy(y_hbm.at[idx], buf, sem).wait()
                    pltpu.sync_copy(buf, o_hbms[g].at[kk, pl.ds(base, ch)])

    return k(ys, *poss)


def _moe_rows(xps, meta_ts, counts, lw):
    total = sum(x.shape[0] for x in xps)
    rt = 512 if total >= 8192 else 128
    n_slots = 2 * total + N_EXPERTS * rt
    nt_max = n_slots // rt

    cnt = counts[:, 0].astype(jnp.int32)
    padded = ((cnt + rt - 1) // rt) * rt
    ends = jnp.cumsum(padded)
    offs = ends - padded
    experts = jnp.arange(N_EXPERTS, dtype=jnp.int32)[:, None]
    poss = []
    for x, meta_t in zip(xps, meta_ts):
        if meta_t.ndim == 3:
            meta_t = meta_t.transpose(1, 0, 2).reshape(META_COLS, x.shape[0])
        eidx = meta_t[0:2].astype(jnp.int32)
        rank = meta_t[4:6].astype(jnp.int32)
        poss.append(jnp.sum(jnp.where(eidx[:, None, :] == experts, offs[:, None], 0), axis=1) + rank)
    n_tiles = ends[-1:] // rt
    tiles = jnp.minimum(jnp.arange(nt_max, dtype=jnp.int32), n_tiles[0] - 1)
    tile_expert = jnp.minimum(jnp.sum((ends // rt)[None, :] <= tiles[:, None], axis=1),
                              N_EXPERTS - 1).astype(jnp.int32)

    first_tile = jnp.concatenate([jnp.ones((1,), jnp.int32),
                                  (tile_expert[1:] != tile_expert[:-1]).astype(jnp.int32)])
    eid = jnp.arange(N_EXPERTS, dtype=jnp.int32)
    later = (eid[None, :] > eid[:, None]) & (cnt[None, :] > 0)
    next_of = jnp.min(jnp.where(later, eid[None, :], N_EXPERTS), axis=1)
    next_of = jnp.where(next_of == N_EXPERTS, -1, next_of)
    next_expert = jnp.sum(jnp.where(tile_expert[:, None] == eid[None, :], next_of[None, :], 0),
                          axis=1).astype(jnp.int32)

    xs = _sc_dispatch(xps, poss, n_slots)
    ys = _expert_call(xs, tile_expert, next_expert, first_tile, n_tiles.astype(jnp.int32), lw, rt=rt)
    return _sc_gather(ys, poss)


def _memkv_kernel(m_ref, wk_ref, wv_ref, k_ref, v_ref):
    mb = m_ref[...].astype(BF16)
    k_ref[0] = _dot(mb, wk_ref[0])
    v_ref[0] = _dot(mb, wv_ref[0])


def _memkv_call(mem, wk, wv):
    depth = wk.shape[0]
    rows = mem.shape[0]
    return pl.pallas_call(
        _memkv_kernel,
        grid=(depth,),
        in_specs=[
            pl.BlockSpec((rows, D_MODEL), lambda l: (0, 0)),
            pl.BlockSpec((1, D_MODEL, D_ATT), lambda l: (l, 0, 0)),
            pl.BlockSpec((1, D_MODEL, D_ATT), lambda l: (l, 0, 0)),
        ],
        out_specs=[
            pl.BlockSpec((1, rows, D_ATT), lambda l: (l, 0, 0)),
            pl.BlockSpec((1, rows, D_ATT), lambda l: (l, 0, 0)),
        ],
        out_shape=[jax.ShapeDtypeStruct((depth, rows, D_ATT), F32)] * 2,
        compiler_params=pltpu.CompilerParams(
            dimension_semantics=("arbitrary",), vmem_limit_bytes=VMEM_LIMIT_BYTES),
        name="memkv",
    )(mem, wk, wv)


def _perm_matrices(lc):
    tm = lc * SEQS_PER_STREAM
    p = np.zeros((tm, tm), np.float32)
    for s in range(SEQS_PER_STREAM):
        for t in range(lc):
            p[s * lc + t, t * SEQS_PER_STREAM + s] = 1.0
    return jnp.asarray(p, BF16), jnp.asarray(p.T, BF16)


def _to_streams(a):
    b, l, c = a.shape
    q = b // SEQS_PER_STREAM
    return a.reshape(q, SEQS_PER_STREAM, l, c).transpose(0, 2, 1, 3).reshape(q, l * SEQS_PER_STREAM, c)


def _from_streams(a, l):
    q, _, c = a.shape
    return a.reshape(q, l, SEQS_PER_STREAM, c).transpose(0, 2, 1, 3).reshape(q * SEQS_PER_STREAM, l, c)


def _pack_state(re, im):
    b = re.shape[0]
    q = b // SEQS_PER_STREAM
    re = re.reshape(q, SEQS_PER_STREAM, D_STATE)
    im = im.reshape(q, SEQS_PER_STREAM, D_STATE)
    return jnp.concatenate([im, re], axis=1)


def _unpack_state(h):
    q = h.shape[0]
    im = h[:, 0:SEQS_PER_STREAM].reshape(q * SEQS_PER_STREAM, N_SSM_GROUPS, SSM_STATE)
    re = h[:, SEQS_PER_STREAM:].reshape(q * SEQS_PER_STREAM, N_SSM_GROUPS, SSM_STATE)
    return re, im


def _pad_heads(mk, mv):
    b = mk.shape[0]
    q = b // SEQS_PER_STREAM
    shape = (q, SEQS_PER_STREAM, N_MEM, D_ATT)
    return mk.reshape(shape).transpose(0, 1, 3, 2).astype(BF16), mv.reshape(shape).astype(BF16)


def _model_params(w_in, w_dw, b_dw, conv_ln_g, conv_ln_b, ssm_a_re, ssm_a_im, ssm_b_re,
                  ssm_b_im, ssm_c_re, ssm_c_im, ssm_d, ssm_log_dt, ssm_w_glu, ssm_b_glu,
                  w_out, ln1_g, ln1_b, w_up, b_up, w_down, ln2_g, ln2_b):
    depth = w_in.shape[0]
    a = lax.complex(ssm_a_re, ssm_a_im)
    dt = jnp.exp(ssm_log_dt)[..., None]
    a_bar = jnp.exp(a * dt)
    b_bar = ((a_bar - 1.0) / a)[..., None] * lax.complex(ssm_b_re, ssm_b_im)
    eye = jnp.eye(N_SSM_GROUPS, dtype=F32)

    def b_block(m):
        full = jnp.einsum("lgpi,gh->lgihp", m, eye).reshape(depth, D_SSM, D_STATE).astype(BF16)
        return jnp.stack([full[:, k * LANES:(k + 1) * LANES, k * SCAN_LANES:(k + 1) * SCAN_LANES]
                          for k in range(SSM_BLOCKS)], axis=1)

    def c_block(m):
        full = jnp.einsum("lgip,gh->lgphi", m, eye).reshape(depth, D_STATE, D_SSM).astype(BF16)
        return jnp.stack([full[:, k * SCAN_LANES:(k + 1) * SCAN_LANES, k * LANES:(k + 1) * LANES]
                          for k in range(SSM_BLOCKS)], axis=1)

    ar = jnp.real(a_bar).reshape(depth, 1, D_STATE)
    ai = jnp.imag(a_bar).reshape(depth, 1, D_STATE)
    half = SUBLANES // 2
    return {
        "w_in": w_in.astype(BF16),
        "wdw": jnp.repeat(w_dw, SUBLANES, axis=1),
        "bdw": b_dw[:, None], "clg": conv_ln_g[:, None], "clb": conv_ln_b[:, None],
        "a1": jnp.broadcast_to(ar, (depth, SUBLANES, D_STATE)),
        "a2": jnp.concatenate([jnp.broadcast_to(-ai, (depth, half, D_STATE)),
                               jnp.broadcast_to(ai, (depth, half, D_STATE))], axis=1),
        "bre": b_block(jnp.real(b_bar)), "bim": b_block(jnp.imag(b_bar)),
        "cre": c_block(ssm_c_re), "cim": c_block(-ssm_c_im),
        "d": ssm_d[:, None], "wglu": ssm_w_glu.astype(BF16), "bglu": ssm_b_glu[:, None],
        "w_out": w_out.astype(BF16), "g1": ln1_g[:, None], "b1": ln1_b[:, None],
        "w_up": w_up, "b_up": b_up[:, :, None, :], "w_down": w_down,
        "g2": ln2_g[:, None], "b2": ln2_b[:, None],
    }


def kernel(x_prompt, x_sample, cache_conv, state_ssm_re, state_ssm_im, cache_mem_k, cache_mem_v, mem_prompt, w_in, w_dw, b_dw, conv_ln_g, conv_ln_b, ssm_a_re, ssm_a_im, ssm_b_re, ssm_b_im, ssm_c_re, ssm_c_im, ssm_d, ssm_log_dt, ssm_w_glu, ssm_b_glu, w_mem_k, w_mem_v, w_out, ln1_g, ln1_b, w_router, b_router, w_up, b_up, w_down, ln2_g, ln2_b):
    depth = w_in.shape[0]
    alpha = (2.0 * depth) ** 0.25
    bp, seq, _ = x_prompt.shape
    bs, dec_seq, _ = x_sample.shape
    assert bp == SEQS_PER_STREAM and bs % SEQS_PER_STREAM == 0
    lc_p = min(128, seq)
    lc_s = dec_seq
    assert seq % lc_p == 0 and lc_p % 16 == 0 and lc_s % 16 == 0

    mb, mm, _ = mem_prompt.shape
    mk_all, mv_all = _memkv_call(mem_prompt.reshape(mb * mm, D_MODEL),
                                 w_mem_k.astype(BF16), w_mem_v.astype(BF16))
    mk_all = mk_all.reshape(depth, mb, mm, N_MEM_HEADS, MEM_HEAD_DIM)
    mv_all = mv_all.reshape(depth, mb, mm, N_MEM_HEADS, MEM_HEAD_DIM)

    perm_p, permt_p = _perm_matrices(lc_p)
    perm_s, permt_s = _perm_matrices(lc_s)
    wr = w_router.T.astype(BF16)
    br = b_router[:, None]

    tiles_p = seq // lc_p
    tiles_a = max(tiles_p // 2, 1)
    tiles_b = tiles_p - tiles_a
    steps_a = tiles_a * lc_p
    xa = xb = x_prompt
    xs = x_sample
    zero_hist = jnp.zeros((1, HIST_ROWS, D_CONV), F32)
    zero_h = jnp.zeros((1, SUBLANES, D_STATE), F32)
    zero_cnt = jnp.zeros((N_EXPERTS, 1), F32)

    params = _model_params(w_in, w_dw, b_dw, conv_ln_g, conv_ln_b, ssm_a_re, ssm_a_im, ssm_b_re,
                           ssm_b_im, ssm_c_re, ssm_c_im, ssm_d, ssm_log_dt, ssm_w_glu, ssm_b_glu,
                           w_out, ln1_g, ln1_b, w_up, b_up, w_down, ln2_g, ln2_b)
    nq_p = bp // SEQS_PER_STREAM
    nq_s = bs // SEQS_PER_STREAM
    head_shape = (mm, N_MEM_HEADS, MEM_HEAD_DIM)
    kp_p, vp_p = _pad_heads(mk_all.reshape((depth * mb,) + head_shape),
                            mv_all.reshape((depth * mb,) + head_shape))
    kp_s, vp_s = _pad_heads(cache_mem_k.reshape((depth * bs,) + head_shape),
                            cache_mem_v.reshape((depth * bs,) + head_shape))
    hist_s = _to_streams(cache_conv.reshape(depth * bs, CONV_BUF, D_CONV))
    h_s = _pack_state(state_ssm_re.reshape(depth * bs, N_SSM_GROUPS, SSM_STATE),
                      state_ssm_im.reshape(depth * bs, N_SSM_GROUPS, SSM_STATE))

    hists_p, hsts_p, hists_s, hsts_s = [], [], [], []
    moe_a = moe_b = moe_s = None
    for l in range(depth):
        lw = dict(params, layer=l)
        xa, hist, hst, rows_a, meta_a, metat_a, cnt_a = _mixer_call(
            xa, moe_a, zero_hist, zero_h, kp_p, vp_p, perm_p, permt_p, lw, wr, br, zero_cnt,
            lc=lc_p, alpha=alpha, tile_off=0, n_tiles=tiles_a, kv_off=l * nq_p)
        (yg_a,) = _moe_rows([rows_a], [metat_a], cnt_a, lw)
        moe_a = (yg_a, meta_a)
        xb, hist, hst, rows_b, meta_b, metat_b, cnt_b = _mixer_call(
            xb, moe_b, hist, hst, kp_p, vp_p, perm_p, permt_p, lw, wr, br, zero_cnt,
            lc=lc_p, alpha=alpha, tile_off=tiles_a, n_tiles=tiles_b, kv_off=l * nq_p)
        hists_p.append(hist)
        hsts_p.append(hst)
        xs, hist, hst, rows_s, meta_s, metat_s, cnt_s = _mixer_call(
            xs, moe_s, hist_s, h_s, kp_s, vp_s, perm_s, permt_s, lw, wr, br, cnt_b,
            lc=lc_s, alpha=alpha, state_off=l * nq_s, kv_off=l * nq_s)
        hists_s.append(hist)
        hsts_s.append(hst)
        yg_b, yg_s = _moe_rows([rows_b, rows_s], [metat_b, metat_s], cnt_s, lw)
        moe_b = (yg_b, meta_b)
        moe_s = (yg_s, meta_s)

    y_prompt = _combine_call(xa, moe_a[0], moe_a[1], lw, alpha=alpha, seq_len=seq)
    y_prompt = _combine_call(xb, moe_b[0], moe_b[1], lw, alpha=alpha, seq_len=seq, step_off=steps_a,
                             prev_out=y_prompt)
    y_sample = _combine_call(xs, moe_s[0], moe_s[1], lw, alpha=alpha, seq_len=dec_seq)

    def states(hists, hsts, batch):
        conv = _from_streams(jnp.concatenate(hists, axis=0), CONV_BUF)
        re, im = _unpack_state(jnp.concatenate(hsts, axis=0))
        return (conv.reshape(depth, batch, CONV_BUF, D_CONV),
                re.reshape(depth, batch, N_SSM_GROUPS, SSM_STATE),
                im.reshape(depth, batch, N_SSM_GROUPS, SSM_STATE))

    conv_p, re_p, im_p = states(hists_p, hsts_p, bp)
    conv_s, re_s, im_s = states(hists_s, hsts_s, bs)
    return (y_prompt, y_sample, conv_p, re_p, im_p, mk_all, mv_all, conv_s, re_s, im_s)
```

```python
import functools
import math

import numpy as np
import jax
import jax.numpy as jnp
from jax import lax
from jax.experimental import pallas as pl
from jax.experimental.pallas import tpu as pltpu
from jax.experimental.pallas import tpu_sc as plsc

F32 = jnp.float32
BF16 = jnp.bfloat16

D_MODEL = 1024
CONV_WIDTH = 31
CONV_BUF = CONV_WIDTH - 1
D_CONV = 384
D_SSM = 384
SSM_GROUP = 16
N_SSM_GROUPS = D_SSM // SSM_GROUP
SSM_STATE = 64
D_STATE = N_SSM_GROUPS * SSM_STATE
N_MEM = 256
N_MEM_HEADS = 4
MEM_HEAD_DIM = 64
D_ATT = N_MEM_HEADS * MEM_HEAD_DIM
D_IN = 2 * D_CONV + D_SSM + D_ATT
N_EXPERTS = 16
N_EXPERT_GROUPS = 4
EXPERTS_PER_GROUP = N_EXPERTS // N_EXPERT_GROUPS
D_EXPERT = 512
LN_EPS = 1e-5

SEQS_PER_STREAM = 4
HIST_ROWS = CONV_BUF * SEQS_PER_STREAM
SUBLANES = 8
LANES = 128
SCAN_LANES = 512
SSM_BLOCKS = D_STATE // SCAN_LANES
CONV_ROWS = 32
VMEM_LIMIT_BYTES = 56 * 1024 * 1024
HI_HALF_MASK = np.int32(-65536)
META_COLS = 8
SC_CORES = 2
SC_WORKERS = SC_CORES * 16
SC_MAX_CHUNK = 128


def _sigmoid(x):
    return 1.0 / (1.0 + jnp.exp(-x))


def _gelu_tanh(x):
    c = math.sqrt(2.0 / math.pi)
    return 0.5 * x * (1.0 + jnp.tanh(c * (x + 0.044715 * (x * x * x))))


def _layer_norm(z, g, b):
    mu = jnp.mean(z, axis=-1, keepdims=True)
    zc = z - mu
    var = jnp.mean(zc * zc, axis=-1, keepdims=True)
    return zc * lax.rsqrt(var + LN_EPS) * g + b


def _dot(a, b):
    return jnp.dot(a, b, preferred_element_type=F32)


def _mixer_kernel(*refs, tm, lc, alpha, fuse_in):
    refs = list(refs)
    if fuse_in:
        xprev_ref, yg_ref, metain_ref, g2p_ref, b2p_ref = refs[:5]
        refs = refs[5:]
    else:
        xprev_ref = refs.pop(0)
        xslab = refs.pop()
    (hist0_ref, h0_ref, k_ref, v_ref, perm_ref, permt_ref,
     w_in_ref, wdw_ref, bdw_ref, clg_ref, clb_ref, a1_ref, a2_ref,
     bre_ref, bim_ref, cre_ref, cim_ref, d_ref, wglu_ref, bglu_ref,
     wout_ref, g1_ref, b1_ref, wr_ref, br_ref, tri_ref, cnt0_ref,
     x1_ref, hist_out_ref, h_out_ref, xp_ref, meta_ref, metat_ref, cnt_ref,
     xpad0, xpad4, cy, ush, yim, bu_re, bu_im, hre, him, hcar, xin, running) = refs
    i = pl.program_id(1)

    @pl.when(i == 0)
    def _():
        xpad0[0:HIST_ROWS, :] = hist0_ref[0]
        hcar[...] = h0_ref[0]
        ush[...] = jnp.zeros_like(ush)

    @pl.when((i == 0) & (pl.program_id(0) == 0))
    def _():
        running[...] = cnt0_ref[...]

    if fuse_in:
        xin[...] = _moe_combine(xprev_ref[0], yg_ref, metain_ref[...], g2p_ref[...], b2p_ref[...], alpha)
    else:
        nslab = D_MODEL // LANES
        for s in range(SEQS_PER_STREAM):
            for j in range(nslab):
                xslab[j, pl.ds(s, lc, stride=SEQS_PER_STREAM), :] = xprev_ref[s, :, j * LANES:(j + 1) * LANES]
        xin[...] = jnp.concatenate([xslab[j] for j in range(nslab)], axis=1)
    x = xin[...]
    proj = _dot(x.astype(BF16), w_in_ref[...])

    g = proj[:, 0:D_CONV] * _sigmoid(proj[:, D_CONV:2 * D_CONV])
    xpad0[HIST_ROWS:HIST_ROWS + tm, :] = g
    xpad4[0:HIST_ROWS + tm - 4, :] = xpad0[4:HIST_ROWS + tm, :]

    nsub = CONV_ROWS // SUBLANES

    def conv_rows(rb, carry):
        r0 = pl.multiple_of(rb * CONV_ROWS, CONV_ROWS)
        accs = [jnp.broadcast_to(bdw_ref[...], (SUBLANES, D_CONV)) for _ in range(nsub)]
        for k in range(CONV_WIDTH):
            wk = wdw_ref[SUBLANES * k:SUBLANES * (k + 1), :]
            for sb in range(nsub):
                off = r0 + SEQS_PER_STREAM * k + SUBLANES * sb
                if k % 2 == 0:
                    xs = xpad0[pl.ds(pl.multiple_of(off, SUBLANES), SUBLANES), :]
                else:
                    xs = xpad4[pl.ds(pl.multiple_of(off - 4, SUBLANES), SUBLANES), :]
                accs[sb] = accs[sb] + xs * wk
        for sb in range(nsub):
            cy[pl.ds(pl.multiple_of(r0 + SUBLANES * sb, SUBLANES), SUBLANES), :] = accs[sb]
        return carry

    lax.fori_loop(0, tm // CONV_ROWS, conv_rows, 0, unroll=True)
    conv_n = _layer_norm(cy[...], clg_ref[...], clb_ref[...])
    cy[...] = conv_n * _sigmoid(conv_n)

    new_hist = xpad0[tm:tm + HIST_ROWS, :]
    xpad0[0:HIST_ROWS, :] = new_hist
    hist_out_ref[0] = new_hist

    u = proj[:, 2 * D_CONV:2 * D_CONV + D_SSM]
    ush[4:tm + 4, :] = u
    ub = u.astype(BF16)
    ub_sh = ush[...].astype(BF16)
    for m in range(SSM_BLOCKS):
        ch = slice(m * LANES, (m + 1) * LANES)
        st = slice(m * SCAN_LANES, (m + 1) * SCAN_LANES)
        bu_re[:, st] = _dot(ub[:, ch], bre_ref[m])
        bu_im[:, st] = _dot(ub_sh[:, ch], bim_ref[m])

    lo = lax.broadcasted_iota(jnp.int32, (SUBLANES, SCAN_LANES), 0) < 4
    for c in range(D_STATE // SCAN_LANES):
        cs = slice(c * SCAN_LANES, (c + 1) * SCAN_LANES)
        a1 = a1_ref[:, cs]
        a2 = a2_ref[:, cs]

        def scan_pair(j, carry, cs=cs, a1=a1, a2=a2):
            h_prev, im_cur = carry
            r = pl.multiple_of(j * SUBLANES, SUBLANES)
            re_cur = bu_re[pl.ds(r, SUBLANES), cs]
            im_next = bu_im[pl.ds(r + SUBLANES, SUBLANES), cs]
            p_even = jnp.where(lo, re_cur, im_cur)
            p_odd = jnp.where(lo, im_next, re_cur)
            h_even = a1 * pltpu.roll(h_prev, 4, 0) + a2 * h_prev + p_even
            h_odd = a1 * pltpu.roll(h_even, 4, 0) - a2 * h_even + p_odd
            hre[pl.ds(r, SUBLANES), cs] = jnp.where(lo, h_even, h_odd)
            him[pl.ds(r, SUBLANES), cs] = jnp.where(lo, h_prev, h_even)
            return h_odd, im_next

        h_last, _ = lax.fori_loop(0, lc // 2, scan_pair,
                                  (hcar[:, cs], bu_im[0:SUBLANES, cs]), unroll=True)
        him[tm:tm + SUBLANES, cs] = jnp.where(lo, h_last, 0.0)
        hcar[:, cs] = h_last
    h_out_ref[0] = hcar[...]

    y_re_blocks = []
    for m in range(SSM_BLOCKS):
        ch = slice(m * LANES, (m + 1) * LANES)
        st = slice(m * SCAN_LANES, (m + 1) * SCAN_LANES)
        y_re_blocks.append(_dot(hre[:, st].astype(BF16), cre_ref[m]))
        yim[:, ch] = _dot(him[:, st].astype(BF16), cim_ref[m])
    y_re = jnp.concatenate(y_re_blocks, axis=1)
    y = y_re + yim[4:tm + 4, :] + d_ref[...] * u
    y = _gelu_tanh(y)
    ssm_y = y * _sigmoid(_dot(y.astype(BF16), wglu_ref[...]) + bglu_ref[...])

    q = proj[:, 2 * D_CONV + D_SSM:D_IN].astype(BF16)
    q_seq = _dot(perm_ref[...], q)
    head_of_col = lax.shift_right_logical(lax.broadcasted_iota(jnp.int32, (1, D_ATT), 1), 6)
    head_masks = [jnp.where(head_of_col == h, 1.0, 0.0) for h in range(N_MEM_HEADS)]
    outs = []
    for s in range(SEQS_PER_STREAM):
        qs = q_seq[s * lc:(s + 1) * lc, :]
        ks = k_ref[0, s]
        vs = v_ref[0, s]
        acc = jnp.zeros((lc, D_ATT), F32)
        for h in range(N_MEM_HEADS):
            qh = (qs * head_masks[h]).astype(BF16)
            sc = _dot(qh, ks) * (MEM_HEAD_DIM ** -0.5)
            sc = sc - jnp.max(sc, axis=-1, keepdims=True)
            e = jnp.exp(sc)
            p = e * (1.0 / jnp.sum(e, axis=-1, keepdims=True))
            acc = acc + _dot(p.astype(BF16), vs) * head_masks[h]
        outs.append(acc)
    att_seq = jnp.concatenate(outs, axis=0).astype(BF16)
    att = _dot(permt_ref[...], att_seq).astype(BF16)

    mix = _dot(jnp.concatenate([cy[...].astype(BF16), ssm_y.astype(BF16), att], axis=1), wout_ref[...])
    x1 = _layer_norm(alpha * xin[...] + mix, g1_ref[...], b1_ref[...])
    x1_ref[0] = x1
    _route_rows(x1, wr_ref, br_ref, tri_ref, xp_ref, meta_ref, metat_ref, cnt_ref, running)


def _mixer_call(x, prev_moe, hist0, h0, kpad, vpad, perm, permt, lw, wr, br, cnt0, *, lc, alpha,
                tile_off=0, n_tiles=None, state_off=0, kv_off=0):
    fuse_in = prev_moe is not None
    tm = lc * SEQS_PER_STREAM
    if fuse_in:
        nq, rows, _ = x.shape
    else:
        nq = x.shape[0] // SEQS_PER_STREAM
        rows = (x.shape[1] // lc if n_tiles is None else n_tiles) * tm
    nt = rows // tm
    kern = functools.partial(_mixer_kernel, tm=tm, lc=lc, alpha=alpha, fuse_in=fuse_in)
    triu = jnp.asarray(np.triu(np.ones((tm, tm), np.float32), 1), BF16)

    def const(shape):
        return pl.BlockSpec(shape, lambda q, i: (0,) * len(shape))

    def flat(shape):
        return pl.BlockSpec(shape, lambda q, i: (0,) * (len(shape) - 2) + (q * nt + i, 0))

    layer = lw["layer"]

    def layered(shape, l=layer):
        return pl.BlockSpec((None,) + shape, lambda q, i: (l,) + (0,) * len(shape))

    operands = [x]
    if fuse_in:
        yg_prev, meta_prev = prev_moe
        in_specs = [pl.BlockSpec((1, tm, D_MODEL), lambda q, i: (q, i, 0)),
                    flat((2, tm, D_MODEL // 2)), flat((tm, META_COLS)),
                    layered((1, D_MODEL), layer - 1), layered((1, D_MODEL), layer - 1)]
        operands += [yg_prev, meta_prev, lw["g2"], lw["b2"]]
    else:
        in_specs = [pl.BlockSpec((SEQS_PER_STREAM, lc, D_MODEL), lambda q, i: (q, i + tile_off, 0))]
    in_specs += [
        pl.BlockSpec((1, HIST_ROWS, D_CONV), lambda q, i: (q + state_off, 0, 0)),
        pl.BlockSpec((1, SUBLANES, D_STATE), lambda q, i: (q + state_off, 0, 0)),
        pl.BlockSpec((1, SEQS_PER_STREAM, N_MEM, D_ATT), lambda q, i: (q + kv_off, 0, 0, 0)),
        pl.BlockSpec((1, SEQS_PER_STREAM, N_MEM, D_ATT), lambda q, i: (q + kv_off, 0, 0, 0)),
        const((tm, tm)), const((tm, tm)),
        layered((D_MODEL, D_IN)),
        layered((CONV_WIDTH * SUBLANES, D_CONV)), layered((1, D_CONV)), layered((1, D_CONV)),
        layered((1, D_CONV)),
        layered((SUBLANES, D_STATE)), layered((SUBLANES, D_STATE)),
        layered((SSM_BLOCKS, LANES, SCAN_LANES)), layered((SSM_BLOCKS, LANES, SCAN_LANES)),
        layered((SSM_BLOCKS, SCAN_LANES, LANES)), layered((SSM_BLOCKS, SCAN_LANES, LANES)),
        layered((1, D_SSM)), layered((D_SSM, D_SSM)), layered((1, D_SSM)),
        layered((D_MODEL, D_MODEL)), layered((1, D_MODEL)), layered((1, D_MODEL)),
        const((N_EXPERTS, D_MODEL)), const((N_EXPERTS, 1)), const((tm, tm)), const((N_EXPERTS, 1)),
    ]
    out_specs = [
        pl.BlockSpec((1, tm, D_MODEL), lambda q, i: (q, i, 0)),
        pl.BlockSpec((1, HIST_ROWS, D_CONV), lambda q, i: (q, 0, 0)),
        pl.BlockSpec((1, SUBLANES, D_STATE), lambda q, i: (q, 0, 0)),
        flat((tm, D_MODEL // 2)), flat((tm, META_COLS)),
        (pl.BlockSpec((META_COLS, tm), lambda q, i: (0, q * nt + i)) if tm % LANES == 0 else
         pl.BlockSpec((None, META_COLS, tm), lambda q, i: (q * nt + i, 0, 0))),
        const((N_EXPERTS, 1)),
    ]
    out_shape = [
        jax.ShapeDtypeStruct((nq, rows, D_MODEL), F32),
        jax.ShapeDtypeStruct((nq, HIST_ROWS, D_CONV), F32),
        jax.ShapeDtypeStruct((nq, SUBLANES, D_STATE), F32),
        jax.ShapeDtypeStruct((nq * rows, D_MODEL // 2), jnp.int32),
        jax.ShapeDtypeStruct((nq * rows, META_COLS), F32),
        jax.ShapeDtypeStruct((META_COLS, nq * rows) if tm % LANES == 0 else (nq * nt, META_COLS, tm), F32),
        jax.ShapeDtypeStruct((N_EXPERTS, 1), F32),
    ]
    scratch = [
        pltpu.VMEM((HIST_ROWS + tm + SUBLANES, D_CONV), F32),
        pltpu.VMEM((HIST_ROWS + tm + SUBLANES, D_CONV), F32),
        pltpu.VMEM((tm, D_CONV), F32),
        pltpu.VMEM((tm + SUBLANES, D_SSM), F32),
        pltpu.VMEM((tm + SUBLANES, D_SSM), F32),
        pltpu.VMEM((tm, D_STATE), F32),
        pltpu.VMEM((tm + SUBLANES, D_STATE), F32),
        pltpu.VMEM((tm, D_STATE), F32),
        pltpu.VMEM((tm + SUBLANES, D_STATE), F32),
        pltpu.VMEM((SUBLANES, D_STATE), F32),
        pltpu.VMEM((tm, D_MODEL), F32),
        pltpu.VMEM((N_EXPERTS, 1), F32),
    ]
    if not fuse_in:
        scratch.append(pltpu.VMEM((D_MODEL // LANES, tm, LANES), F32))
    return pl.pallas_call(
        kern,
        grid=(nq, nt),
        in_specs=in_specs,
        out_specs=out_specs,
        out_shape=out_shape,
        scratch_shapes=scratch,
        compiler_params=pltpu.CompilerParams(
            dimension_semantics=("arbitrary", "arbitrary"),
            vmem_limit_bytes=VMEM_LIMIT_BYTES),
        name="mixer",
    )(*operands, hist0, h0, kpad, vpad, perm, permt,
      lw["w_in"], lw["wdw"], lw["bdw"], lw["clg"], lw["clb"], lw["a1"], lw["a2"],
      lw["bre"], lw["bim"], lw["cre"], lw["cim"], lw["d"], lw["wglu"], lw["bglu"],
      lw["w_out"], lw["g1"], lw["b1"], wr, br, triu, cnt0)


def _route(logits_t):
    m = jnp.max(logits_t, axis=0, keepdims=True)
    e = jnp.exp(logits_t - m)
    aff = e / jnp.sum(e, axis=0, keepdims=True)
    rows = [aff[j:j + 1, :] for j in range(N_EXPERTS)]

    scores = []
    for gi in range(N_EXPERT_GROUPS):
        a, b, c, d = rows[EXPERTS_PER_GROUP * gi:EXPERTS_PER_GROUP * (gi + 1)]
        hi1, lo1 = jnp.maximum(a, b), jnp.minimum(a, b)
        hi2, lo2 = jnp.maximum(c, d), jnp.minimum(c, d)
        scores.append(jnp.maximum(hi1, hi2) + jnp.maximum(jnp.minimum(hi1, hi2), jnp.maximum(lo1, lo2)))
    best = scores[0]
    sel = jnp.zeros_like(best)
    for gi in range(1, N_EXPERT_GROUPS):
        better = scores[gi] > best
        sel = jnp.where(better, float(gi), sel)
        best = jnp.where(better, scores[gi], best)

    hot1, hot2 = [], []
    for gi in range(N_EXPERT_GROUPS):
        vals = rows[EXPERTS_PER_GROUP * gi:EXPERTS_PER_GROUP * (gi + 1)]
        chosen = sel == float(gi)
        for j in range(EXPERTS_PER_GROUP):
            ahead = jnp.zeros_like(best)
            for k in range(EXPERTS_PER_GROUP):
                if k < j:
                    ahead = ahead + jnp.where(vals[k] >= vals[j], 1.0, 0.0)
                elif k > j:
                    ahead = ahead + jnp.where(vals[k] > vals[j], 1.0, 0.0)
            hot1.append(jnp.where(chosen, jnp.where(ahead == 0.0, 1.0, 0.0), 0.0))
            hot2.append(jnp.where(chosen, jnp.where(ahead == 1.0, 1.0, 0.0), 0.0))
    return jnp.concatenate(hot1, axis=0), jnp.concatenate(hot2, axis=0), aff


def _pack_halves(y):
    half = y.shape[1] // 2
    lo = lax.bitcast_convert_type(y[:, :half].astype(BF16).astype(F32), jnp.int32)
    hi = lax.bitcast_convert_type(y[:, half:].astype(BF16).astype(F32), jnp.int32)
    return lax.shift_right_logical(lo, 16) | (hi & HI_HALF_MASK)


def _unpack_halves(w):
    lo = lax.bitcast_convert_type(lax.shift_left(w, 16), F32)
    hi = lax.bitcast_convert_type(w & HI_HALF_MASK, F32)
    return lo, hi


def _route_rows(x, wr_ref, br_ref, triu_ref, xp_ref, meta_ref, metat_ref, cnt_ref, running):
    tm = x.shape[0]
    xp_ref[...] = _pack_halves(x)
    logits_t = lax.dot_general(wr_ref[...], x.astype(BF16), (((1,), (1,)), ((), ())),
                               preferred_element_type=F32) + br_ref[...]
    hot1, hot2, aff = _route(logits_t)
    eid = lax.broadcasted_iota(jnp.int32, (N_EXPERTS, tm), 0).astype(F32)
    both = hot1 + hot2
    before = _dot(both.astype(BF16), triu_ref[...]) + running[...]

    def pick(hot, vals):
        return jnp.sum(hot * vals, axis=0, keepdims=True)

    v1 = pick(hot1, aff)
    v2 = pick(hot2, aff)
    denom = v1 + v2
    meta_t = jnp.concatenate(
        [pick(hot1, eid), pick(hot2, eid), v1 / denom, v2 / denom, pick(hot1, before), pick(hot2, before),
         jnp.zeros((LANES - 6, tm), F32)], axis=0)
    metat_ref[...] = meta_t[0:META_COLS, :]
    meta_ref[...] = meta_t.T[:, 0:META_COLS]
    running[...] = running[...] + jnp.sum(both, axis=1, keepdims=True)
    cnt_ref[...] = running[...]


def _expert_kernel(te_ref, nxt_ref, first_ref, nt_ref, xs_ref, wup_hbm, bup_ref, wdn_hbm, ys_ref,
                   wup_f32, wdn_f32, wup_bf, wdn_bf, sems, slot_ref, *, layer):
    i = pl.program_id(0)

    def weight_copies(e, slot):
        return (pltpu.make_async_copy(wup_hbm.at[layer, e], wup_f32.at[slot], sems.at[0, slot]),
                pltpu.make_async_copy(wdn_hbm.at[layer, e], wdn_f32.at[slot], sems.at[1, slot]))

    @pl.when(i == 0)
    def _():
        slot_ref[0] = 0
        for c in weight_copies(te_ref[0], 0):
            c.start()

    @pl.when(first_ref[i] == 1)
    def _():
        slot = slot_ref[0]
        for c in weight_copies(te_ref[i], slot):
            c.wait()
        wup_bf[...] = wup_f32[slot].astype(BF16)
        wdn_bf[...] = wdn_f32[slot].astype(BF16)

        @pl.when(nxt_ref[i] >= 0)
        def _():
            for c in weight_copies(nxt_ref[i], 1 - slot):
                c.start()

        slot_ref[0] = 1 - slot

    @pl.when(i < nt_ref[0])
    def _():
        half = D_MODEL // 2
        lo, hi = _unpack_halves(xs_ref[...])
        h = (_dot(lo.astype(BF16), wup_bf[0:half, :])
             + _dot(hi.astype(BF16), wup_bf[half:D_MODEL, :]) + bup_ref[0])
        ys_ref[...] = _pack_halves(_dot(_gelu_tanh(h).astype(BF16), wdn_bf[...]))


def _expert_call(xs, tile_expert, next_expert, first_tile, n_tiles, lw, *, rt):
    rows = xs.shape[0]

    def row_map(i, te, nxt, first, nt):
        return (jnp.minimum(i, nt[0] - 1), 0)

    return pl.pallas_call(
        functools.partial(_expert_kernel, layer=lw["layer"]),
        grid_spec=pltpu.PrefetchScalarGridSpec(
            num_scalar_prefetch=4,
            grid=(rows // rt,),
            in_specs=[
                pl.BlockSpec((rt, D_MODEL // 2), row_map),
                pl.BlockSpec(memory_space=pl.ANY),
                pl.BlockSpec((None, 1, 1, D_EXPERT),
                             lambda i, te, nxt, first, nt: (lw["layer"], te[i], 0, 0)),
                pl.BlockSpec(memory_space=pl.ANY),
            ],
            out_specs=pl.BlockSpec((rt, D_MODEL // 2), row_map),
            scratch_shapes=[pltpu.VMEM((2, D_MODEL, D_EXPERT), F32),
                            pltpu.VMEM((2, D_EXPERT, D_MODEL), F32),
                            pltpu.VMEM((D_MODEL, D_EXPERT), BF16),
                            pltpu.VMEM((D_EXPERT, D_MODEL), BF16),
                            pltpu.SemaphoreType.DMA((2, 2)),
                            pltpu.SMEM((1,), jnp.int32)],
        ),
        out_shape=jax.ShapeDtypeStruct((rows, D_MODEL // 2), jnp.int32),
        compiler_params=pltpu.CompilerParams(
            dimension_semantics=("arbitrary",), vmem_limit_bytes=VMEM_LIMIT_BYTES),
        name="experts",
    )(tile_expert, next_expert, first_tile, n_tiles, xs, lw["w_up"], lw["b_up"], lw["w_down"])


def _moe_combine(x1, yg_ref, meta, g, b, alpha):
    g1 = meta[:, 2:3]
    g2 = meta[:, 3:4]
    lo1, hi1 = _unpack_halves(yg_ref[0])
    lo2, hi2 = _unpack_halves(yg_ref[1])
    moe = jnp.concatenate([g1 * lo1 + g2 * lo2, g1 * hi1 + g2 * hi2], axis=1)
    return _layer_norm(alpha * x1 + moe, g, b)


def _combine_kernel(x_ref, yg_ref, meta_ref, g2_ref, b2_ref, *rest, alpha):
    o_ref, yslab = rest[-2:]
    y = _moe_combine(x_ref[...], yg_ref, meta_ref[...], g2_ref[...], b2_ref[...], alpha)
    nslab = D_MODEL // LANES
    steps = y.shape[0] // SEQS_PER_STREAM
    for j in range(nslab):
        yslab[j] = y[:, j * LANES:(j + 1) * LANES]
    for s in range(SEQS_PER_STREAM):
        for j in range(nslab):
            o_ref[s, :, j * LANES:(j + 1) * LANES] = yslab[j, pl.ds(s, steps, stride=SEQS_PER_STREAM), :]


def _combine_call(x1, yg, meta, lw, *, alpha, seq_len, step_off=0, prev_out=None):
    nq, srows, _ = x1.shape
    tmc = min(1024, srows)
    tps = srows // tmc
    steps = tmc // SEQS_PER_STREAM
    blk_off = step_off // steps
    in_specs = [
        pl.BlockSpec((tmc, D_MODEL), lambda i: (i, 0)),
        pl.BlockSpec((2, tmc, D_MODEL // 2), lambda i: (0, i, 0)),
        pl.BlockSpec((tmc, META_COLS), lambda i: (i, 0)),
        pl.BlockSpec((None, 1, D_MODEL), lambda i: (lw["layer"], 0, 0)),
        pl.BlockSpec((None, 1, D_MODEL), lambda i: (lw["layer"], 0, 0)),
    ]
    operands = [x1.reshape(nq * srows, D_MODEL), yg, meta, lw["g2"], lw["b2"]]
    aliases = {}
    if prev_out is not None:
        in_specs.append(pl.BlockSpec(memory_space=pl.ANY))
        operands.append(prev_out)
        aliases = {len(operands) - 1: 0}
    return pl.pallas_call(
        functools.partial(_combine_kernel, alpha=alpha),
        grid=(nq * tps,),
        in_specs=in_specs,
        out_specs=pl.BlockSpec((SEQS_PER_STREAM, steps, D_MODEL),
                               lambda i: (i // tps, i % tps + blk_off, 0)),
        out_shape=jax.ShapeDtypeStruct((nq * SEQS_PER_STREAM, seq_len, D_MODEL), F32),
        scratch_shapes=[pltpu.VMEM((D_MODEL // LANES, tmc, LANES), F32)],
        input_output_aliases=aliases,
        compiler_params=pltpu.CompilerParams(
            dimension_semantics=("arbitrary",), vmem_limit_bytes=VMEM_LIMIT_BYTES),
        name="combine",
    )(*operands)


def _sc_mesh():
    return plsc.VectorSubcoreMesh(core_axis_name="c", subcore_axis_name="s")


def _sc_chunk(rows):
    per_worker = rows // SC_WORKERS
    chunk = min(SC_MAX_CHUNK, per_worker)
    assert per_worker % chunk == 0 and chunk % 8 == 0
    return per_worker // chunk, chunk


def _sc_dispatch(xps, poss, n_slots):
    width = xps[0].shape[1]
    ngroups = len(xps)
    plans = [_sc_chunk(x.shape[0]) for x in xps]
    scratch = []
    for _, ch in plans:
        scratch += [pltpu.VMEM((ch,), jnp.int32), pltpu.VMEM((ch,), jnp.int32),
                    pltpu.VMEM((ch, width), jnp.int32)]

    @functools.partial(
        pl.kernel, mesh=_sc_mesh(),
        out_type=jax.ShapeDtypeStruct((n_slots, width), jnp.int32),
        scratch_types=scratch + [pltpu.SemaphoreType.DMA],
        name="sc_dispatch")
    def k(*refs):
        x_hbms, pos_hbms, o_hbm = refs[:ngroups], refs[ngroups:2 * ngroups], refs[2 * ngroups]
        bufs, sem = refs[2 * ngroups + 1:-1], refs[-1]
        wid = lax.axis_index("s") * SC_CORES + lax.axis_index("c")
        for g, (nch, ch) in enumerate(plans):
            idx0, idx1, buf = bufs[3 * g:3 * g + 3]

            @pl.loop(0, nch)
            def _(c, g=g, nch=nch, ch=ch, idx0=idx0, idx1=idx1, buf=buf):
                base = (wid * nch + c) * ch
                pltpu.sync_copy(x_hbms[g].at[pl.ds(base, ch)], buf)
                pltpu.sync_copy(pos_hbms[g].at[0, pl.ds(base, ch)], idx0)
                pltpu.sync_copy(pos_hbms[g].at[1, pl.ds(base, ch)], idx1)
                pltpu.async_copy(buf, o_hbm.at[idx0], sem).wait()
                pltpu.async_copy(buf, o_hbm.at[idx1], sem).wait()

    return k(*xps, *poss)


def _sc_gather(ys, poss):
    width = ys.shape[1]
    ngroups = len(poss)
    plans = [_sc_chunk(p.shape[1]) for p in poss]
    scratch = []
    for _, ch in plans:
        scratch += [pltpu.VMEM((ch,), jnp.int32), pltpu.VMEM((ch, width), jnp.int32)]

    @functools.partial(
        pl.kernel, mesh=_sc_mesh(),
        out_type=[jax.ShapeDtypeStruct((2, p.shape[1], width), jnp.int32) for p in poss],
        scratch_types=scratch + [pltpu.SemaphoreType.DMA],
        name="sc_gather")
    def k(*refs):
        y_hbm, pos_hbms = refs[0], refs[1:1 + ngroups]
        o_hbms = refs[1 + ngroups:1 + 2 * ngroups]
        bufs, sem = refs[1 + 2 * ngroups:-1], refs[-1]
        wid = lax.axis_index("s") * SC_CORES + lax.axis_index("c")
        for g, (nch, ch) in enumerate(plans):
            idx, buf = bufs[2 * g:2 * g + 2]

            @pl.loop(0, nch)
            def _(c, g=g, nch=nch, ch=ch, idx=idx, buf=buf):
                base = (wid * nch + c) * ch
                for kk in range(2):
                    pltpu.sync_copy(pos_hbms[g].at[kk, pl.ds(base, ch)], idx)
                    pltpu.async_copy(y_hbm.at[idx], buf, sem).wait()
                    pltpu.sync_copy(buf, o_hbms[g].at[kk, pl.ds(base, ch)])

    return k(ys, *poss)


def _moe_rows(xps, meta_ts, counts, lw):
    total = sum(x.shape[0] for x in xps)
    rt = 512 if total >= 8192 else 128
    n_slots = 2 * total + N_EXPERTS * rt
    nt_max = n_slots // rt

    cnt = counts[:, 0].astype(jnp.int32)
    padded = ((cnt + rt - 1) // rt) * rt
    ends = jnp.cumsum(padded)
    offs = ends - padded
    experts = jnp.arange(N_EXPERTS, dtype=jnp.int32)[:, None]
    poss = []
    for x, meta_t in zip(xps, meta_ts):
        if meta_t.ndim == 3:
            meta_t = meta_t.transpose(1, 0, 2).reshape(META_COLS, x.shape[0])
        eidx = meta_t[0:2].astype(jnp.int32)
        rank = meta_t[4:6].astype(jnp.int32)
        poss.append(jnp.sum(jnp.where(eidx[:, None, :] == experts, offs[:, None], 0), axis=1) + rank)
    n_tiles = ends[-1:] // rt
    tiles = jnp.minimum(jnp.arange(nt_max, dtype=jnp.int32), n_tiles[0] - 1)
    tile_expert = jnp.minimum(jnp.sum((ends // rt)[None, :] <= tiles[:, None], axis=1),
                              N_EXPERTS - 1).astype(jnp.int32)

    first_tile = jnp.concatenate([jnp.ones((1,), jnp.int32),
                                  (tile_expert[1:] != tile_expert[:-1]).astype(jnp.int32)])
    eid = jnp.arange(N_EXPERTS, dtype=jnp.int32)
    later = (eid[None, :] > eid[:, None]) & (cnt[None, :] > 0)
    next_of = jnp.min(jnp.where(later, eid[None, :], N_EXPERTS), axis=1)
    next_of = jnp.where(next_of == N_EXPERTS, -1, next_of)
    next_expert = jnp.sum(jnp.where(tile_expert[:, None] == eid[None, :], next_of[None, :], 0),
                          axis=1).astype(jnp.int32)

    xs = _sc_dispatch(xps, poss, n_slots)
    ys = _expert_call(xs, tile_expert, next_expert, first_tile, n_tiles.astype(jnp.int32), lw, rt=rt)
    return _sc_gather(ys, poss)


def _state_layout_kernel(hist_ref, h_ref, conv_ref, re_ref, im_ref):
    for s in range(SEQS_PER_STREAM):
        for t in range(CONV_BUF):
            r = SEQS_PER_STREAM * t + s
            conv_ref[0, s, t:t + 1, :] = hist_ref[0, r:r + 1, :]
        for gi in range(N_SSM_GROUPS):
            cols = slice(gi * SSM_STATE, (gi + 1) * SSM_STATE)
            im_ref[0, s, gi:gi + 1, :] = h_ref[0, s:s + 1, cols]
            re_ref[0, s, gi:gi + 1, :] = h_ref[0, SEQS_PER_STREAM + s:SEQS_PER_STREAM + s + 1, cols]


def _state_layout_call(hist, h):
    n = hist.shape[0]
    state = (SEQS_PER_STREAM, N_SSM_GROUPS, SSM_STATE)
    return pl.pallas_call(
        _state_layout_kernel,
        grid=(n,),
        in_specs=[pl.BlockSpec((1, HIST_ROWS, D_CONV), lambda q: (q, 0, 0)),
                  pl.BlockSpec((1, SUBLANES, D_STATE), lambda q: (q, 0, 0))],
        out_specs=[pl.BlockSpec((1, SEQS_PER_STREAM, CONV_BUF, D_CONV), lambda q: (q, 0, 0, 0)),
                   pl.BlockSpec((1,) + state, lambda q: (q, 0, 0, 0)),
                   pl.BlockSpec((1,) + state, lambda q: (q, 0, 0, 0))],
        out_shape=[jax.ShapeDtypeStruct((n, SEQS_PER_STREAM, CONV_BUF, D_CONV), F32),
                   jax.ShapeDtypeStruct((n,) + state, F32),
                   jax.ShapeDtypeStruct((n,) + state, F32)],
        compiler_params=pltpu.CompilerParams(dimension_semantics=("arbitrary",)),
        name="state_layout",
    )(hist, h)


def _memkv_kernel(m_ref, wk_ref, wv_ref, k_ref, v_ref):
    mb = m_ref[...].astype(BF16)
    k_ref[0] = _dot(mb, wk_ref[0])
    v_ref[0] = _dot(mb, wv_ref[0])


def _memkv_call(mem, wk, wv):
    depth = wk.shape[0]
    rows = mem.shape[0]
    return pl.pallas_call(
        _memkv_kernel,
        grid=(depth,),
        in_specs=[
            pl.BlockSpec((rows, D_MODEL), lambda l: (0, 0)),
            pl.BlockSpec((1, D_MODEL, D_ATT), lambda l: (l, 0, 0)),
            pl.BlockSpec((1, D_MODEL, D_ATT), lambda l: (l, 0, 0)),
        ],
        out_specs=[
            pl.BlockSpec((1, rows, D_ATT), lambda l: (l, 0, 0)),
            pl.BlockSpec((1, rows, D_ATT), lambda l: (l, 0, 0)),
        ],
        out_shape=[jax.ShapeDtypeStruct((depth, rows, D_ATT), F32)] * 2,
        compiler_params=pltpu.CompilerParams(
            dimension_semantics=("arbitrary",), vmem_limit_bytes=VMEM_LIMIT_BYTES),
        name="memkv",
    )(mem, wk, wv)


def _perm_matrices(lc):
    tm = lc * SEQS_PER_STREAM
    p = np.zeros((tm, tm), np.float32)
    for s in range(SEQS_PER_STREAM):
        for t in range(lc):
            p[s * lc + t, t * SEQS_PER_STREAM + s] = 1.0
    return jnp.asarray(p, BF16), jnp.asarray(p.T, BF16)


def _to_streams(a):
    b, l, c = a.shape
    q = b // SEQS_PER_STREAM
    return a.reshape(q, SEQS_PER_STREAM, l, c).transpose(0, 2, 1, 3).reshape(q, l * SEQS_PER_STREAM, c)


def _from_streams(a, l):
    q, _, c = a.shape
    return a.reshape(q, l, SEQS_PER_STREAM, c).transpose(0, 2, 1, 3).reshape(q * SEQS_PER_STREAM, l, c)


def _pack_state(re, im):
    b = re.shape[0]
    q = b // SEQS_PER_STREAM
    re = re.reshape(q, SEQS_PER_STREAM, D_STATE)
    im = im.reshape(q, SEQS_PER_STREAM, D_STATE)
    return jnp.concatenate([im, re], axis=1)


def _unpack_state(h):
    q = h.shape[0]
    im = h[:, 0:SEQS_PER_STREAM].reshape(q * SEQS_PER_STREAM, N_SSM_GROUPS, SSM_STATE)
    re = h[:, SEQS_PER_STREAM:].reshape(q * SEQS_PER_STREAM, N_SSM_GROUPS, SSM_STATE)
    return re, im


def _pad_heads(mk, mv):
    b = mk.shape[0]
    q = b // SEQS_PER_STREAM
    shape = (q, SEQS_PER_STREAM, N_MEM, D_ATT)
    return mk.reshape(shape).transpose(0, 1, 3, 2).astype(BF16), mv.reshape(shape).astype(BF16)


def _model_params(w_in, w_dw, b_dw, conv_ln_g, conv_ln_b, ssm_a_re, ssm_a_im, ssm_b_re,
                  ssm_b_im, ssm_c_re, ssm_c_im, ssm_d, ssm_log_dt, ssm_w_glu, ssm_b_glu,
                  w_out, ln1_g, ln1_b, w_up, b_up, w_down, ln2_g, ln2_b):
    depth = w_in.shape[0]
    a = lax.complex(ssm_a_re, ssm_a_im)
    dt = jnp.exp(ssm_log_dt)[..., None]
    a_bar = jnp.exp(a * dt)
    b_bar = ((a_bar - 1.0) / a)[..., None] * lax.complex(ssm_b_re, ssm_b_im)
    eye = jnp.eye(N_SSM_GROUPS, dtype=F32)

    def b_block(m):
        full = jnp.einsum("lgpi,gh->lgihp", m, eye).reshape(depth, D_SSM, D_STATE).astype(BF16)
        return jnp.stack([full[:, k * LANES:(k + 1) * LANES, k * SCAN_LANES:(k + 1) * SCAN_LANES]
                          for k in range(SSM_BLOCKS)], axis=1)

    def c_block(m):
        full = jnp.einsum("lgip,gh->lgphi", m, eye).reshape(depth, D_STATE, D_SSM).astype(BF16)
        return jnp.stack([full[:, k * SCAN_LANES:(k + 1) * SCAN_LANES, k * LANES:(k + 1) * LANES]
                          for k in range(SSM_BLOCKS)], axis=1)

    ar = jnp.real(a_bar).reshape(depth, 1, D_STATE)
    ai = jnp.imag(a_bar).reshape(depth, 1, D_STATE)
    half = SUBLANES // 2
    return {
        "w_in": w_in.astype(BF16),
        "wdw": jnp.repeat(w_dw, SUBLANES, axis=1),
        "bdw": b_dw[:, None], "clg": conv_ln_g[:, None], "clb": conv_ln_b[:, None],
        "a1": jnp.broadcast_to(ar, (depth, SUBLANES, D_STATE)),
        "a2": jnp.concatenate([jnp.broadcast_to(-ai, (depth, half, D_STATE)),
                               jnp.broadcast_to(ai, (depth, half, D_STATE))], axis=1),
        "bre": b_block(jnp.real(b_bar)), "bim": b_block(jnp.imag(b_bar)),
        "cre": c_block(ssm_c_re), "cim": c_block(-ssm_c_im),
        "d": ssm_d[:, None], "wglu": ssm_w_glu.astype(BF16), "bglu": ssm_b_glu[:, None],
        "w_out": w_out.astype(BF16), "g1": ln1_g[:, None], "b1": ln1_b[:, None],
        "w_up": w_up, "b_up": b_up[:, :, None, :], "w_down": w_down,
        "g2": ln2_g[:, None], "b2": ln2_b[:, None],
    }


def kernel(x_prompt, x_sample, cache_conv, state_ssm_re, state_ssm_im, cache_mem_k, cache_mem_v, mem_prompt, w_in, w_dw, b_dw, conv_ln_g, conv_ln_b, ssm_a_re, ssm_a_im, ssm_b_re, ssm_b_im, ssm_c_re, ssm_c_im, ssm_d, ssm_log_dt, ssm_w_glu, ssm_b_glu, w_mem_k, w_mem_v, w_out, ln1_g, ln1_b, w_router, b_router, w_up, b_up, w_down, ln2_g, ln2_b):
    depth = w_in.shape[0]
    alpha = (2.0 * depth) ** 0.25
    bp, seq, _ = x_prompt.shape
    bs, dec_seq, _ = x_sample.shape
    assert bp == SEQS_PER_STREAM and bs % SEQS_PER_STREAM == 0
    lc_p = min(128, seq)
    lc_s = dec_seq
    assert seq % lc_p == 0 and lc_p % 16 == 0 and lc_s % 16 == 0

    mb, mm, _ = mem_prompt.shape
    mk_all, mv_all = _memkv_call(mem_prompt.reshape(mb * mm, D_MODEL),
                                 w_mem_k.astype(BF16), w_mem_v.astype(BF16))
    mk_all = mk_all.reshape(depth, mb, mm, N_MEM_HEADS, MEM_HEAD_DIM)
    mv_all = mv_all.reshape(depth, mb, mm, N_MEM_HEADS, MEM_HEAD_DIM)

    perm_p, permt_p = _perm_matrices(lc_p)
    perm_s, permt_s = _perm_matrices(lc_s)
    wr = w_router.T.astype(BF16)
    br = b_router[:, None]

    tiles_p = seq // lc_p
    tiles_a = max(tiles_p // 2, 1)
    tiles_b = tiles_p - tiles_a
    steps_a = tiles_a * lc_p
    xa = xb = x_prompt
    xs = x_sample
    zero_hist = jnp.zeros((1, HIST_ROWS, D_CONV), F32)
    zero_h = jnp.zeros((1, SUBLANES, D_STATE), F32)
    zero_cnt = jnp.zeros((N_EXPERTS, 1), F32)

    params = _model_params(w_in, w_dw, b_dw, conv_ln_g, conv_ln_b, ssm_a_re, ssm_a_im, ssm_b_re,
                           ssm_b_im, ssm_c_re, ssm_c_im, ssm_d, ssm_log_dt, ssm_w_glu, ssm_b_glu,
                           w_out, ln1_g, ln1_b, w_up, b_up, w_down, ln2_g, ln2_b)
    nq_p = bp // SEQS_PER_STREAM
    nq_s = bs // SEQS_PER_STREAM
    head_shape = (mm, N_MEM_HEADS, MEM_HEAD_DIM)
    kp_p, vp_p = _pad_heads(mk_all.reshape((depth * mb,) + head_shape),
                            mv_all.reshape((depth * mb,) + head_shape))
    kp_s, vp_s = _pad_heads(cache_mem_k.reshape((depth * bs,) + head_shape),
                            cache_mem_v.reshape((depth * bs,) + head_shape))
    hist_s = _to_streams(cache_conv.reshape(depth * bs, CONV_BUF, D_CONV))
    h_s = _pack_state(state_ssm_re.reshape(depth * bs, N_SSM_GROUPS, SSM_STATE),
                      state_ssm_im.reshape(depth * bs, N_SSM_GROUPS, SSM_STATE))

    finals_p, finals_s = [], []
    moe_a = moe_b = moe_s = None
    for l in range(depth):
        lw = dict(params, layer=l)
        xa, hist, hst, rows_a, meta_a, metat_a, cnt_a = _mixer_call(
            xa, moe_a, zero_hist, zero_h, kp_p, vp_p, perm_p, permt_p, lw, wr, br, zero_cnt,
            lc=lc_p, alpha=alpha, tile_off=0, n_tiles=tiles_a, kv_off=l * nq_p)
        (yg_a,) = _moe_rows([rows_a], [metat_a], cnt_a, lw)
        moe_a = (yg_a, meta_a)
        xb, hist, hst, rows_b, meta_b, metat_b, cnt_b = _mixer_call(
            xb, moe_b, hist, hst, kp_p, vp_p, perm_p, permt_p, lw, wr, br, zero_cnt,
            lc=lc_p, alpha=alpha, tile_off=tiles_a, n_tiles=tiles_b, kv_off=l * nq_p)
        finals_p.append((hist, hst))
        xs, hist, hst, rows_s, meta_s, metat_s, cnt_s = _mixer_call(
            xs, moe_s, hist_s, h_s, kp_s, vp_s, perm_s, permt_s, lw, wr, br, cnt_b,
            lc=lc_s, alpha=alpha, state_off=l * nq_s, kv_off=l * nq_s)
        finals_s.append((hist, hst))
        yg_b, yg_s = _moe_rows([rows_b, rows_s], [metat_b, metat_s], cnt_s, lw)
        moe_b = (yg_b, meta_b)
        moe_s = (yg_s, meta_s)

    y_prompt = _combine_call(xa, moe_a[0], moe_a[1], lw, alpha=alpha, seq_len=seq)
    y_prompt = _combine_call(xb, moe_b[0], moe_b[1], lw, alpha=alpha, seq_len=seq, step_off=steps_a,
                             prev_out=y_prompt)
    y_sample = _combine_call(xs, moe_s[0], moe_s[1], lw, alpha=alpha, seq_len=dec_seq)

    def states(finals, batch):
        outs = _state_layout_call(jnp.concatenate([f[0] for f in finals], axis=0),
                                  jnp.concatenate([f[1] for f in finals], axis=0))
        return tuple(o.reshape((depth, batch) + o.shape[2:]) for o in outs)

    conv_p, re_p, im_p = states(finals_p, bp)
    conv_s, re_s, im_s = states(finals_s, bs)
    return (y_prompt, y_sample, conv_p, re_p, im_p, mk_all, mv_all, conv_s, re_s, im_s)
```

```python
import functools
import math

import numpy as np
import jax
import jax.numpy as jnp
from jax import lax
from jax.experimental import pallas as pl
from jax.experimental.pallas import tpu as pltpu
from jax.experimental.pallas import tpu_sc as plsc

F32 = jnp.float32
BF16 = jnp.bfloat16

D_MODEL = 1024
CONV_WIDTH = 31
CONV_BUF = CONV_WIDTH - 1
D_CONV = 384
D_SSM = 384
SSM_GROUP = 16
N_SSM_GROUPS = D_SSM // SSM_GROUP
SSM_STATE = 64
D_STATE = N_SSM_GROUPS * SSM_STATE
N_MEM = 256
N_MEM_HEADS = 4
MEM_HEAD_DIM = 64
D_ATT = N_MEM_HEADS * MEM_HEAD_DIM
D_IN = 2 * D_CONV + D_SSM + D_ATT
N_EXPERTS = 16
N_EXPERT_GROUPS = 4
EXPERTS_PER_GROUP = N_EXPERTS // N_EXPERT_GROUPS
D_EXPERT = 512
LN_EPS = 1e-5

SEQS_PER_STREAM = 4
HIST_ROWS = CONV_BUF * SEQS_PER_STREAM
SUBLANES = 8
LANES = 128
SCAN_LANES = 512
SSM_BLOCKS = D_STATE // SCAN_LANES
CONV_ROWS = 32
VMEM_LIMIT_BYTES = 56 * 1024 * 1024
HI_HALF_MASK = np.int32(-65536)
META_COLS = 8
SC_CORES = 2
SC_WORKERS = SC_CORES * 16
SC_MAX_CHUNK = 128


def _sigmoid(x):
    return 1.0 / (1.0 + jnp.exp(-x))


def _gelu_tanh(x):
    c = math.sqrt(2.0 / math.pi)
    return 0.5 * x * (1.0 + jnp.tanh(c * (x + 0.044715 * (x * x * x))))


def _layer_norm(z, g, b):
    mu = jnp.mean(z, axis=-1, keepdims=True)
    zc = z - mu
    var = jnp.mean(zc * zc, axis=-1, keepdims=True)
    return zc * lax.rsqrt(var + LN_EPS) * g + b


def _dot(a, b):
    return jnp.dot(a, b, preferred_element_type=F32)


def _mixer_kernel(*refs, tm, lc, alpha, fuse_in):
    refs = list(refs)
    if fuse_in:
        xprev_ref, yg_ref, metain_ref, g2p_ref, b2p_ref = refs[:5]
        refs = refs[5:]
    else:
        xprev_ref = refs.pop(0)
        xslab = refs.pop()
    (hist0_ref, h0_ref, k_ref, v_ref, perm_ref, permt_ref,
     w_in_ref, wdw_ref, bdw_ref, clg_ref, clb_ref, a1_ref, a2_ref,
     bre_ref, bim_ref, cre_ref, cim_ref, d_ref, wglu_ref, bglu_ref,
     wout_ref, g1_ref, b1_ref, wr_ref, br_ref, tri_ref, cnt0_ref,
     x1_ref, hist_out_ref, h_out_ref, xp_ref, meta_ref, metat_ref, cnt_ref,
     xpad0, xpad4, cy, ush, yim, bu_re, bu_im, hre, him, hcar, xin, running) = refs
    i = pl.program_id(1)

    @pl.when(i == 0)
    def _():
        xpad0[0:HIST_ROWS, :] = hist0_ref[0]
        hcar[...] = h0_ref[0]
        ush[...] = jnp.zeros_like(ush)

    @pl.when((i == 0) & (pl.program_id(0) == 0))
    def _():
        running[...] = cnt0_ref[...]

    if fuse_in:
        xin[...] = _moe_combine(xprev_ref[0], yg_ref, metain_ref[...], g2p_ref[...], b2p_ref[...], alpha)
    else:
        nslab = D_MODEL // LANES
        for s in range(SEQS_PER_STREAM):
            for j in range(nslab):
                xslab[j, pl.ds(s, lc, stride=SEQS_PER_STREAM), :] = xprev_ref[s, :, j * LANES:(j + 1) * LANES]
        xin[...] = jnp.concatenate([xslab[j] for j in range(nslab)], axis=1)
    x = xin[...]
    proj = _dot(x.astype(BF16), w_in_ref[...])

    g = proj[:, 0:D_CONV] * _sigmoid(proj[:, D_CONV:2 * D_CONV])
    xpad0[HIST_ROWS:HIST_ROWS + tm, :] = g
    xpad4[0:HIST_ROWS + tm - 4, :] = xpad0[4:HIST_ROWS + tm, :]

    nsub = CONV_ROWS // SUBLANES

    def conv_rows(rb, carry):
        r0 = pl.multiple_of(rb * CONV_ROWS, CONV_ROWS)
        accs = [jnp.broadcast_to(bdw_ref[...], (SUBLANES, D_CONV)) for _ in range(nsub)]
        for k in range(CONV_WIDTH):
            wk = wdw_ref[SUBLANES * k:SUBLANES * (k + 1), :]
            for sb in range(nsub):
                off = r0 + SEQS_PER_STREAM * k + SUBLANES * sb
                if k % 2 == 0:
                    xs = xpad0[pl.ds(pl.multiple_of(off, SUBLANES), SUBLANES), :]
                else:
                    xs = xpad4[pl.ds(pl.multiple_of(off - 4, SUBLANES), SUBLANES), :]
                accs[sb] = accs[sb] + xs * wk
        for sb in range(nsub):
            cy[pl.ds(pl.multiple_of(r0 + SUBLANES * sb, SUBLANES), SUBLANES), :] = accs[sb]
        return carry

    lax.fori_loop(0, tm // CONV_ROWS, conv_rows, 0, unroll=True)
    conv_n = _layer_norm(cy[...], clg_ref[...], clb_ref[...])
    cy[...] = conv_n * _sigmoid(conv_n)

    new_hist = xpad0[tm:tm + HIST_ROWS, :]
    xpad0[0:HIST_ROWS, :] = new_hist
    hist_out_ref[0] = new_hist

    u = proj[:, 2 * D_CONV:2 * D_CONV + D_SSM]
    ush[4:tm + 4, :] = u
    ub = u.astype(BF16)
    ub_sh = ush[...].astype(BF16)
    for m in range(SSM_BLOCKS):
        ch = slice(m * LANES, (m + 1) * LANES)
        st = slice(m * SCAN_LANES, (m + 1) * SCAN_LANES)
        bu_re[:, st] = _dot(ub[:, ch], bre_ref[m])
        bu_im[:, st] = _dot(ub_sh[:, ch], bim_ref[m])

    lo = lax.broadcasted_iota(jnp.int32, (SUBLANES, SCAN_LANES), 0) < 4
    for c in range(D_STATE // SCAN_LANES):
        cs = slice(c * SCAN_LANES, (c + 1) * SCAN_LANES)
        a1 = a1_ref[:, cs]
        a2 = a2_ref[:, cs]

        def scan_pair(j, carry, cs=cs, a1=a1, a2=a2):
            h_prev, im_cur = carry
            r = pl.multiple_of(j * SUBLANES, SUBLANES)
            re_cur = bu_re[pl.ds(r, SUBLANES), cs]
            im_next = bu_im[pl.ds(r + SUBLANES, SUBLANES), cs]
            p_even = jnp.where(lo, re_cur, im_cur)
            p_odd = jnp.where(lo, im_next, re_cur)
            h_even = a1 * pltpu.roll(h_prev, 4, 0) + a2 * h_prev + p_even
            h_odd = a1 * pltpu.roll(h_even, 4, 0) - a2 * h_even + p_odd
            hre[pl.ds(r, SUBLANES), cs] = jnp.where(lo, h_even, h_odd)
            him[pl.ds(r, SUBLANES), cs] = jnp.where(lo, h_prev, h_even)
            return h_odd, im_next

        h_last, _ = lax.fori_loop(0, lc // 2, scan_pair,
                                  (hcar[:, cs], bu_im[0:SUBLANES, cs]), unroll=True)
        him[tm:tm + SUBLANES, cs] = jnp.where(lo, h_last, 0.0)
        hcar[:, cs] = h_last
    h_out_ref[0] = hcar[...]

    y_re_blocks = []
    for m in range(SSM_BLOCKS):
        ch = slice(m * LANES, (m + 1) * LANES)
        st = slice(m * SCAN_LANES, (m + 1) * SCAN_LANES)
        y_re_blocks.append(_dot(hre[:, st].astype(BF16), cre_ref[m]))
        yim[:, ch] = _dot(him[:, st].astype(BF16), cim_ref[m])
    y_re = jnp.concatenate(y_re_blocks, axis=1)
    y = y_re + yim[4:tm + 4, :] + d_ref[...] * u
    y = _gelu_tanh(y)
    ssm_y = y * _sigmoid(_dot(y.astype(BF16), wglu_ref[...]) + bglu_ref[...])

    q = proj[:, 2 * D_CONV + D_SSM:D_IN].astype(BF16)
    q_seq = _dot(perm_ref[...], q)
    head_of_col = lax.shift_right_logical(lax.broadcasted_iota(jnp.int32, (1, D_ATT), 1), 6)
    head_masks = [jnp.where(head_of_col == h, 1.0, 0.0) for h in range(N_MEM_HEADS)]
    outs = []
    for s in range(SEQS_PER_STREAM):
        qs = q_seq[s * lc:(s + 1) * lc, :]
        ks = k_ref[0, s]
        vs = v_ref[0, s]
        acc = jnp.zeros((lc, D_ATT), F32)
        for h in range(N_MEM_HEADS):
            qh = (qs * head_masks[h]).astype(BF16)
            sc = _dot(qh, ks) * (MEM_HEAD_DIM ** -0.5)
            sc = sc - jnp.max(sc, axis=-1, keepdims=True)
            e = jnp.exp(sc)
            p = e * (1.0 / jnp.sum(e, axis=-1, keepdims=True))
            acc = acc + _dot(p.astype(BF16), vs) * head_masks[h]
        outs.append(acc)
    att_seq = jnp.concatenate(outs, axis=0).astype(BF16)
    att = _dot(permt_ref[...], att_seq).astype(BF16)

    mix = _dot(jnp.concatenate([cy[...].astype(BF16), ssm_y.astype(BF16), att], axis=1), wout_ref[...])
    x1 = _layer_norm(alpha * xin[...] + mix, g1_ref[...], b1_ref[...])
    x1_ref[0] = x1
    _route_rows(x1, wr_ref, br_ref, tri_ref, xp_ref, meta_ref, metat_ref, cnt_ref, running)


def _mixer_call(x, prev_moe, hist0, h0, kpad, vpad, perm, permt, lw, wr, br, cnt0, *, lc, alpha,
                tile_off=0, n_tiles=None, state_off=0, kv_off=0):
    fuse_in = prev_moe is not None
    tm = lc * SEQS_PER_STREAM
    if fuse_in:
        nq, rows, _ = x.shape
    else:
        nq = x.shape[0] // SEQS_PER_STREAM
        rows = (x.shape[1] // lc if n_tiles is None else n_tiles) * tm
    nt = rows // tm
    kern = functools.partial(_mixer_kernel, tm=tm, lc=lc, alpha=alpha, fuse_in=fuse_in)
    triu = jnp.asarray(np.triu(np.ones((tm, tm), np.float32), 1), BF16)

    def const(shape):
        return pl.BlockSpec(shape, lambda q, i: (0,) * len(shape))

    def flat(shape):
        return pl.BlockSpec(shape, lambda q, i: (0,) * (len(shape) - 2) + (q * nt + i, 0))

    layer = lw["layer"]

    def layered(shape, l=layer):
        return pl.BlockSpec((None,) + shape, lambda q, i: (l,) + (0,) * len(shape))

    operands = [x]
    if fuse_in:
        yg_prev, meta_prev = prev_moe
        in_specs = [pl.BlockSpec((1, tm, D_MODEL), lambda q, i: (q, i, 0)),
                    flat((2, tm, D_MODEL // 2)), flat((tm, META_COLS)),
                    layered((1, D_MODEL), layer - 1), layered((1, D_MODEL), layer - 1)]
        operands += [yg_prev, meta_prev, lw["g2"], lw["b2"]]
    else:
        in_specs = [pl.BlockSpec((SEQS_PER_STREAM, lc, D_MODEL), lambda q, i: (q, i + tile_off, 0))]
    in_specs += [
        pl.BlockSpec((1, HIST_ROWS, D_CONV), lambda q, i: (q + state_off, 0, 0)),
        pl.BlockSpec((1, SUBLANES, D_STATE), lambda q, i: (q + state_off, 0, 0)),
        pl.BlockSpec((1, SEQS_PER_STREAM, N_MEM, D_ATT), lambda q, i: (q + kv_off, 0, 0, 0)),
        pl.BlockSpec((1, SEQS_PER_STREAM, N_MEM, D_ATT), lambda q, i: (q + kv_off, 0, 0, 0)),
        const((tm, tm)), const((tm, tm)),
        layered((D_MODEL, D_IN)),
        layered((CONV_WIDTH * SUBLANES, D_CONV)), layered((1, D_CONV)), layered((1, D_CONV)),
        layered((1, D_CONV)),
        layered((SUBLANES, D_STATE)), layered((SUBLANES, D_STATE)),
        layered((SSM_BLOCKS, LANES, SCAN_LANES)), layered((SSM_BLOCKS, LANES, SCAN_LANES)),
        layered((SSM_BLOCKS, SCAN_LANES, LANES)), layered((SSM_BLOCKS, SCAN_LANES, LANES)),
        layered((1, D_SSM)), layered((D_SSM, D_SSM)), layered((1, D_SSM)),
        layered((D_MODEL, D_MODEL)), layered((1, D_MODEL)), layered((1, D_MODEL)),
        const((N_EXPERTS, D_MODEL)), const((N_EXPERTS, 1)), const((tm, tm)), const((N_EXPERTS, 1)),
    ]
    out_specs = [
        pl.BlockSpec((1, tm, D_MODEL), lambda q, i: (q, i, 0)),
        pl.BlockSpec((1, HIST_ROWS, D_CONV), lambda q, i: (q, 0, 0)),
        pl.BlockSpec((1, SUBLANES, D_STATE), lambda q, i: (q, 0, 0)),
        flat((tm, D_MODEL // 2)), flat((tm, META_COLS)),
        (pl.BlockSpec((META_COLS, tm), lambda q, i: (0, q * nt + i)) if tm % LANES == 0 else
         pl.BlockSpec((None, META_COLS, tm), lambda q, i: (q * nt + i, 0, 0))),
        const((N_EXPERTS, 1)),
    ]
    out_shape = [
        jax.ShapeDtypeStruct((nq, rows, D_MODEL), F32),
        jax.ShapeDtypeStruct((nq, HIST_ROWS, D_CONV), F32),
        jax.ShapeDtypeStruct((nq, SUBLANES, D_STATE), F32),
        jax.ShapeDtypeStruct((nq * rows, D_MODEL // 2), jnp.int32),
        jax.ShapeDtypeStruct((nq * rows, META_COLS), F32),
        jax.ShapeDtypeStruct((META_COLS, nq * rows) if tm % LANES == 0 else (nq * nt, META_COLS, tm), F32),
        jax.ShapeDtypeStruct((N_EXPERTS, 1), F32),
    ]
    scratch = [
        pltpu.VMEM((HIST_ROWS + tm + SUBLANES, D_CONV), F32),
        pltpu.VMEM((HIST_ROWS + tm + SUBLANES, D_CONV), F32),
        pltpu.VMEM((tm, D_CONV), F32),
        pltpu.VMEM((tm + SUBLANES, D_SSM), F32),
        pltpu.VMEM((tm + SUBLANES, D_SSM), F32),
        pltpu.VMEM((tm, D_STATE), F32),
        pltpu.VMEM((tm + SUBLANES, D_STATE), F32),
        pltpu.VMEM((tm, D_STATE), F32),
        pltpu.VMEM((tm + SUBLANES, D_STATE), F32),
        pltpu.VMEM((SUBLANES, D_STATE), F32),
        pltpu.VMEM((tm, D_MODEL), F32),
        pltpu.VMEM((N_EXPERTS, 1), F32),
    ]
    if not fuse_in:
        scratch.append(pltpu.VMEM((D_MODEL // LANES, tm, LANES), F32))
    return pl.pallas_call(
        kern,
        grid=(nq, nt),
        in_specs=in_specs,
        out_specs=out_specs,
        out_shape=out_shape,
        scratch_shapes=scratch,
        compiler_params=pltpu.CompilerParams(
            dimension_semantics=("arbitrary", "arbitrary"),
            vmem_limit_bytes=VMEM_LIMIT_BYTES),
        name="mixer",
    )(*operands, hist0, h0, kpad, vpad, perm, permt,
      lw["w_in"], lw["wdw"], lw["bdw"], lw["clg"], lw["clb"], lw["a1"], lw["a2"],
      lw["bre"], lw["bim"], lw["cre"], lw["cim"], lw["d"], lw["wglu"], lw["bglu"],
      lw["w_out"], lw["g1"], lw["b1"], wr, br, triu, cnt0)


def _route(logits_t):
    m = jnp.max(logits_t, axis=0, keepdims=True)
    e = jnp.exp(logits_t - m)
    aff = e / jnp.sum(e, axis=0, keepdims=True)
    rows = [aff[j:j + 1, :] for j in range(N_EXPERTS)]

    scores = []
    for gi in range(N_EXPERT_GROUPS):
        a, b, c, d = rows[EXPERTS_PER_GROUP * gi:EXPERTS_PER_GROUP * (gi + 1)]
        hi1, lo1 = jnp.maximum(a, b), jnp.minimum(a, b)
        hi2, lo2 = jnp.maximum(c, d), jnp.minimum(c, d)
        scores.append(jnp.maximum(hi1, hi2) + jnp.maximum(jnp.minimum(hi1, hi2), jnp.maximum(lo1, lo2)))
    best = scores[0]
    sel = jnp.zeros_like(best)
    for gi in range(1, N_EXPERT_GROUPS):
        better = scores[gi] > best
        sel = jnp.where(better, float(gi), sel)
        best = jnp.where(better, scores[gi], best)

    hot1, hot2 = [], []
    for gi in range(N_EXPERT_GROUPS):
        vals = rows[EXPERTS_PER_GROUP * gi:EXPERTS_PER_GROUP * (gi + 1)]
        chosen = sel == float(gi)
        for j in range(EXPERTS_PER_GROUP):
            ahead = jnp.zeros_like(best)
            for k in range(EXPERTS_PER_GROUP):
                if k < j:
                    ahead = ahead + jnp.where(vals[k] >= vals[j], 1.0, 0.0)
                elif k > j:
                    ahead = ahead + jnp.where(vals[k] > vals[j], 1.0, 0.0)
            hot1.append(jnp.where(chosen, jnp.where(ahead == 0.0, 1.0, 0.0), 0.0))
            hot2.append(jnp.where(chosen, jnp.where(ahead == 1.0, 1.0, 0.0), 0.0))
    return jnp.concatenate(hot1, axis=0), jnp.concatenate(hot2, axis=0), aff


def _pack_halves(y):
    half = y.shape[1] // 2
    lo = lax.bitcast_convert_type(y[:, :half].astype(BF16).astype(F32), jnp.int32)
    hi = lax.bitcast_convert_type(y[:, half:].astype(BF16).astype(F32), jnp.int32)
    return lax.shift_right_logical(lo, 16) | (hi & HI_HALF_MASK)


def _unpack_halves(w):
    lo = lax.bitcast_convert_type(lax.shift_left(w, 16), F32)
    hi = lax.bitcast_convert_type(w & HI_HALF_MASK, F32)
    return lo, hi


def _route_rows(x, wr_ref, br_ref, triu_ref, xp_ref, meta_ref, metat_ref, cnt_ref, running):
    tm = x.shape[0]
    xp_ref[...] = _pack_halves(x)
    logits_t = lax.dot_general(wr_ref[...], x.astype(BF16), (((1,), (1,)), ((), ())),
                               preferred_element_type=F32) + br_ref[...]
    hot1, hot2, aff = _route(logits_t)
    eid = lax.broadcasted_iota(jnp.int32, (N_EXPERTS, tm), 0).astype(F32)
    both = hot1 + hot2
    before = _dot(both.astype(BF16), triu_ref[...]) + running[...]

    def pick(hot, vals):
        return jnp.sum(hot * vals, axis=0, keepdims=True)

    v1 = pick(hot1, aff)
    v2 = pick(hot2, aff)
    denom = v1 + v2
    meta_t = jnp.concatenate(
        [pick(hot1, eid), pick(hot2, eid), v1 / denom, v2 / denom, pick(hot1, before), pick(hot2, before),
         jnp.zeros((LANES - 6, tm), F32)], axis=0)
    metat_ref[...] = meta_t[0:META_COLS, :]
    meta_ref[...] = meta_t.T[:, 0:META_COLS]
    running[...] = running[...] + jnp.sum(both, axis=1, keepdims=True)
    cnt_ref[...] = running[...]


def _expert_kernel(te_ref, nxt_ref, first_ref, nt_ref, xs_ref, wup_hbm, bup_ref, wdn_hbm, ys_ref,
                   wup_f32, wdn_f32, wup_bf, wdn_bf, sems, slot_ref, *, layer):
    i = pl.program_id(0)

    def weight_copies(e, slot):
        return (pltpu.make_async_copy(wup_hbm.at[layer, e], wup_f32.at[slot], sems.at[0, slot]),
                pltpu.make_async_copy(wdn_hbm.at[layer, e], wdn_f32.at[slot], sems.at[1, slot]))

    @pl.when(i == 0)
    def _():
        slot_ref[0] = 0
        for c in weight_copies(te_ref[0], 0):
            c.start()

    @pl.when(first_ref[i] == 1)
    def _():
        slot = slot_ref[0]
        for c in weight_copies(te_ref[i], slot):
            c.wait()
        wup_bf[...] = wup_f32[slot].astype(BF16)
        wdn_bf[...] = wdn_f32[slot].astype(BF16)

        @pl.when(nxt_ref[i] >= 0)
        def _():
            for c in weight_copies(nxt_ref[i], 1 - slot):
                c.start(priority=1)

        slot_ref[0] = 1 - slot

    @pl.when(i < nt_ref[0])
    def _():
        half = D_MODEL // 2
        lo, hi = _unpack_halves(xs_ref[...])
        h = (_dot(lo.astype(BF16), wup_bf[0:half, :])
             + _dot(hi.astype(BF16), wup_bf[half:D_MODEL, :]) + bup_ref[0])
        ys_ref[...] = _pack_halves(_dot(_gelu_tanh(h).astype(BF16), wdn_bf[...]))


def _expert_call(xs, tile_expert, next_expert, first_tile, n_tiles, lw, *, rt):
    rows = xs.shape[0]

    def row_map(i, te, nxt, first, nt):
        return (jnp.minimum(i, nt[0] - 1), 0)

    return pl.pallas_call(
        functools.partial(_expert_kernel, layer=lw["layer"]),
        grid_spec=pltpu.PrefetchScalarGridSpec(
            num_scalar_prefetch=4,
            grid=(rows // rt,),
            in_specs=[
                pl.BlockSpec((rt, D_MODEL // 2), row_map),
                pl.BlockSpec(memory_space=pl.ANY),
                pl.BlockSpec((None, 1, 1, D_EXPERT),
                             lambda i, te, nxt, first, nt: (lw["layer"], te[i], 0, 0)),
                pl.BlockSpec(memory_space=pl.ANY),
            ],
            out_specs=pl.BlockSpec((rt, D_MODEL // 2), row_map),
            scratch_shapes=[pltpu.VMEM((2, D_MODEL, D_EXPERT), F32),
                            pltpu.VMEM((2, D_EXPERT, D_MODEL), F32),
                            pltpu.VMEM((D_MODEL, D_EXPERT), BF16),
                            pltpu.VMEM((D_EXPERT, D_MODEL), BF16),
                            pltpu.SemaphoreType.DMA((2, 2)),
                            pltpu.SMEM((1,), jnp.int32)],
        ),
        out_shape=jax.ShapeDtypeStruct((rows, D_MODEL // 2), jnp.int32),
        compiler_params=pltpu.CompilerParams(
            dimension_semantics=("arbitrary",), vmem_limit_bytes=VMEM_LIMIT_BYTES),
        name="experts",
    )(tile_expert, next_expert, first_tile, n_tiles, xs, lw["w_up"], lw["b_up"], lw["w_down"])


def _moe_combine(x1, yg_ref, meta, g, b, alpha):
    g1 = meta[:, 2:3]
    g2 = meta[:, 3:4]
    lo1, hi1 = _unpack_halves(yg_ref[0])
    lo2, hi2 = _unpack_halves(yg_ref[1])
    moe = jnp.concatenate([g1 * lo1 + g2 * lo2, g1 * hi1 + g2 * hi2], axis=1)
    return _layer_norm(alpha * x1 + moe, g, b)


def _combine_kernel(x_ref, yg_ref, meta_ref, g2_ref, b2_ref, *rest, alpha):
    o_ref, yslab = rest[-2:]
    y = _moe_combine(x_ref[...], yg_ref, meta_ref[...], g2_ref[...], b2_ref[...], alpha)
    nslab = D_MODEL // LANES
    steps = y.shape[0] // SEQS_PER_STREAM
    for j in range(nslab):
        yslab[j] = y[:, j * LANES:(j + 1) * LANES]
    for s in range(SEQS_PER_STREAM):
        for j in range(nslab):
            o_ref[s, :, j * LANES:(j + 1) * LANES] = yslab[j, pl.ds(s, steps, stride=SEQS_PER_STREAM), :]


def _combine_call(x1, yg, meta, lw, *, alpha, seq_len, step_off=0, prev_out=None):
    nq, srows, _ = x1.shape
    tmc = min(1024, srows)
    tps = srows // tmc
    steps = tmc // SEQS_PER_STREAM
    blk_off = step_off // steps
    in_specs = [
        pl.BlockSpec((tmc, D_MODEL), lambda i: (i, 0)),
        pl.BlockSpec((2, tmc, D_MODEL // 2), lambda i: (0, i, 0)),
        pl.BlockSpec((tmc, META_COLS), lambda i: (i, 0)),
        pl.BlockSpec((None, 1, D_MODEL), lambda i: (lw["layer"], 0, 0)),
        pl.BlockSpec((None, 1, D_MODEL), lambda i: (lw["layer"], 0, 0)),
    ]
    operands = [x1.reshape(nq * srows, D_MODEL), yg, meta, lw["g2"], lw["b2"]]
    aliases = {}
    if prev_out is not None:
        in_specs.append(pl.BlockSpec(memory_space=pl.ANY))
        operands.append(prev_out)
        aliases = {len(operands) - 1: 0}
    return pl.pallas_call(
        functools.partial(_combine_kernel, alpha=alpha),
        grid=(nq * tps,),
        in_specs=in_specs,
        out_specs=pl.BlockSpec((SEQS_PER_STREAM, steps, D_MODEL),
                               lambda i: (i // tps, i % tps + blk_off, 0)),
        out_shape=jax.ShapeDtypeStruct((nq * SEQS_PER_STREAM, seq_len, D_MODEL), F32),
        scratch_shapes=[pltpu.VMEM((D_MODEL // LANES, tmc, LANES), F32)],
        input_output_aliases=aliases,
        compiler_params=pltpu.CompilerParams(
            dimension_semantics=("arbitrary",), vmem_limit_bytes=VMEM_LIMIT_BYTES),
        name="combine",
    )(*operands)


def _sc_mesh():
    return plsc.VectorSubcoreMesh(core_axis_name="c", subcore_axis_name="s")


def _sc_chunk(rows):
    per_worker = rows // SC_WORKERS
    chunk = min(SC_MAX_CHUNK, per_worker)
    assert per_worker % chunk == 0 and chunk % 8 == 0
    return per_worker // chunk, chunk


def _sc_dispatch(xps, poss, n_slots):
    width = xps[0].shape[1]
    ngroups = len(xps)
    plans = [_sc_chunk(x.shape[0]) for x in xps]
    scratch = []
    for _, ch in plans:
        scratch += [pltpu.VMEM((ch,), jnp.int32), pltpu.VMEM((ch,), jnp.int32),
                    pltpu.VMEM((ch, width), jnp.int32)]

    @functools.partial(
        pl.kernel, mesh=_sc_mesh(),
        out_type=jax.ShapeDtypeStruct((n_slots, width), jnp.int32),
        scratch_types=scratch + [pltpu.SemaphoreType.DMA],
        name="sc_dispatch")
    def k(*refs):
        x_hbms, pos_hbms, o_hbm = refs[:ngroups], refs[ngroups:2 * ngroups], refs[2 * ngroups]
        bufs, sem = refs[2 * ngroups + 1:-1], refs[-1]
        wid = lax.axis_index("s") * SC_CORES + lax.axis_index("c")
        for g, (nch, ch) in enumerate(plans):
            idx0, idx1, buf = bufs[3 * g:3 * g + 3]

            @pl.loop(0, nch)
            def _(c, g=g, nch=nch, ch=ch, idx0=idx0, idx1=idx1, buf=buf):
                base = (wid * nch + c) * ch
                pltpu.sync_copy(x_hbms[g].at[pl.ds(base, ch)], buf)
                pltpu.sync_copy(pos_hbms[g].at[0, pl.ds(base, ch)], idx0)
                pltpu.sync_copy(pos_hbms[g].at[1, pl.ds(base, ch)], idx1)
                pltpu.async_copy(buf, o_hbm.at[idx0], sem).wait()
                pltpu.async_copy(buf, o_hbm.at[idx1], sem).wait()

    return k(*xps, *poss)


def _sc_gather(ys, poss):
    width = ys.shape[1]
    ngroups = len(poss)
    plans = [_sc_chunk(p.shape[1]) for p in poss]
    scratch = []
    for _, ch in plans:
        scratch += [pltpu.VMEM((ch,), jnp.int32), pltpu.VMEM((ch, width), jnp.int32)]

    @functools.partial(
        pl.kernel, mesh=_sc_mesh(),
        out_type=[jax.ShapeDtypeStruct((2, p.shape[1], width), jnp.int32) for p in poss],
        scratch_types=scratch + [pltpu.SemaphoreType.DMA],
        name="sc_gather")
    def k(*refs):
        y_hbm, pos_hbms = refs[0], refs[1:1 + ngroups]
        o_hbms = refs[1 + ngroups:1 + 2 * ngroups]
        bufs, sem = refs[1 + 2 * ngroups:-1], refs[-1]
        wid = lax.axis_index("s") * SC_CORES + lax.axis_index("c")
        for g, (nch, ch) in enumerate(plans):
            idx, buf = bufs[2 * g:2 * g + 2]

            @pl.loop(0, nch)
            def _(c, g=g, nch=nch, ch=ch, idx=idx, buf=buf):
                base = (wid * nch + c) * ch
                for kk in range(2):
                    pltpu.sync_copy(pos_hbms[g].at[kk, pl.ds(base, ch)], idx)
                    pltpu.async_copy(y_hbm.at[idx], buf, sem).wait()
                    pltpu.sync_copy(buf, o_hbms[g].at[kk, pl.ds(base, ch)])

    return k(ys, *poss)


def _moe_rows(xps, meta_ts, counts, lw):
    total = sum(x.shape[0] for x in xps)
    rt = 512 if total >= 8192 else 128
    n_slots = 2 * total + N_EXPERTS * rt
    nt_max = n_slots // rt

    cnt = counts[:, 0].astype(jnp.int32)
    padded = ((cnt + rt - 1) // rt) * rt
    ends = jnp.cumsum(padded)
    offs = ends - padded
    experts = jnp.arange(N_EXPERTS, dtype=jnp.int32)[:, None]
    poss = []
    for x, meta_t in zip(xps, meta_ts):
        if meta_t.ndim == 3:
            meta_t = meta_t.transpose(1, 0, 2).reshape(META_COLS, x.shape[0])
        eidx = meta_t[0:2].astype(jnp.int32)
        rank = meta_t[4:6].astype(jnp.int32)
        poss.append(jnp.sum(jnp.where(eidx[:, None, :] == experts, offs[:, None], 0), axis=1) + rank)
    n_tiles = ends[-1:] // rt
    tiles = jnp.minimum(jnp.arange(nt_max, dtype=jnp.int32), n_tiles[0] - 1)
    tile_expert = jnp.minimum(jnp.sum((ends // rt)[None, :] <= tiles[:, None], axis=1),
                              N_EXPERTS - 1).astype(jnp.int32)

    first_tile = jnp.concatenate([jnp.ones((1,), jnp.int32),
                                  (tile_expert[1:] != tile_expert[:-1]).astype(jnp.int32)])
    eid = jnp.arange(N_EXPERTS, dtype=jnp.int32)
    later = (eid[None, :] > eid[:, None]) & (cnt[None, :] > 0)
    next_of = jnp.min(jnp.where(later, eid[None, :], N_EXPERTS), axis=1)
    next_of = jnp.where(next_of == N_EXPERTS, -1, next_of)
    next_expert = jnp.sum(jnp.where(tile_expert[:, None] == eid[None, :], next_of[None, :], 0),
                          axis=1).astype(jnp.int32)

    xs = _sc_dispatch(xps, poss, n_slots)
    ys = _expert_call(xs, tile_expert, next_expert, first_tile, n_tiles.astype(jnp.int32), lw, rt=rt)
    return _sc_gather(ys, poss)


def _memkv_kernel(m_ref, wk_ref, wv_ref, k_ref, v_ref):
    mb = m_ref[...].astype(BF16)
    k_ref[0] = _dot(mb, wk_ref[0])
    v_ref[0] = _dot(mb, wv_ref[0])


def _memkv_call(mem, wk, wv):
    depth = wk.shape[0]
    rows = mem.shape[0]
    return pl.pallas_call(
        _memkv_kernel,
        grid=(depth,),
        in_specs=[
            pl.BlockSpec((rows, D_MODEL), lambda l: (0, 0)),
            pl.BlockSpec((1, D_MODEL, D_ATT), lambda l: (l, 0, 0)),
            pl.BlockSpec((1, D_MODEL, D_ATT), lambda l: (l, 0, 0)),
        ],
        out_specs=[
            pl.BlockSpec((1, rows, D_ATT), lambda l: (l, 0, 0)),
            pl.BlockSpec((1, rows, D_ATT), lambda l: (l, 0, 0)),
        ],
        out_shape=[jax.ShapeDtypeStruct((depth, rows, D_ATT), F32)] * 2,
        compiler_params=pltpu.CompilerParams(
            dimension_semantics=("arbitrary",), vmem_limit_bytes=VMEM_LIMIT_BYTES),
        name="memkv",
    )(mem, wk, wv)


def _perm_matrices(lc):
    tm = lc * SEQS_PER_STREAM
    p = np.zeros((tm, tm), np.float32)
    for s in range(SEQS_PER_STREAM):
        for t in range(lc):
            p[s * lc + t, t * SEQS_PER_STREAM + s] = 1.0
    return jnp.asarray(p, BF16), jnp.asarray(p.T, BF16)


def _to_streams(a):
    b, l, c = a.shape
    q = b // SEQS_PER_STREAM
    return a.reshape(q, SEQS_PER_STREAM, l, c).transpose(0, 2, 1, 3).reshape(q, l * SEQS_PER_STREAM, c)


def _from_streams(a, l):
    q, _, c = a.shape
    return a.reshape(q, l, SEQS_PER_STREAM, c).transpose(0, 2, 1, 3).reshape(q * SEQS_PER_STREAM, l, c)


def _pack_state(re, im):
    b = re.shape[0]
    q = b // SEQS_PER_STREAM
    re = re.reshape(q, SEQS_PER_STREAM, D_STATE)
    im = im.reshape(q, SEQS_PER_STREAM, D_STATE)
    return jnp.concatenate([im, re], axis=1)


def _unpack_state(h):
    q = h.shape[0]
    im = h[:, 0:SEQS_PER_STREAM].reshape(q * SEQS_PER_STREAM, N_SSM_GROUPS, SSM_STATE)
    re = h[:, SEQS_PER_STREAM:].reshape(q * SEQS_PER_STREAM, N_SSM_GROUPS, SSM_STATE)
    return re, im


def _pad_heads(mk, mv):
    b = mk.shape[0]
    q = b // SEQS_PER_STREAM
    shape = (q, SEQS_PER_STREAM, N_MEM, D_ATT)
    return mk.reshape(shape).transpose(0, 1, 3, 2).astype(BF16), mv.reshape(shape).astype(BF16)


def _model_params(w_in, w_dw, b_dw, conv_ln_g, conv_ln_b, ssm_a_re, ssm_a_im, ssm_b_re,
                  ssm_b_im, ssm_c_re, ssm_c_im, ssm_d, ssm_log_dt, ssm_w_glu, ssm_b_glu,
                  w_out, ln1_g, ln1_b, w_up, b_up, w_down, ln2_g, ln2_b):
    depth = w_in.shape[0]
    a = lax.complex(ssm_a_re, ssm_a_im)
    dt = jnp.exp(ssm_log_dt)[..., None]
    a_bar = jnp.exp(a * dt)
    b_bar = ((a_bar - 1.0) / a)[..., None] * lax.complex(ssm_b_re, ssm_b_im)
    eye = jnp.eye(N_SSM_GROUPS, dtype=F32)

    def b_block(m):
        full = jnp.einsum("lgpi,gh->lgihp", m, eye).reshape(depth, D_SSM, D_STATE).astype(BF16)
        return jnp.stack([full[:, k * LANES:(k + 1) * LANES, k * SCAN_LANES:(k + 1) * SCAN_LANES]
                          for k in range(SSM_BLOCKS)], axis=1)

    def c_block(m):
        full = jnp.einsum("lgip,gh->lgphi", m, eye).reshape(depth, D_STATE, D_SSM).astype(BF16)
        return jnp.stack([full[:, k * SCAN_LANES:(k + 1) * SCAN_LANES, k * LANES:(k + 1) * LANES]
                          for k in range(SSM_BLOCKS)], axis=1)

    ar = jnp.real(a_bar).reshape(depth, 1, D_STATE)
    ai = jnp.imag(a_bar).reshape(depth, 1, D_STATE)
    half = SUBLANES // 2
    return {
        "w_in": w_in.astype(BF16),
        "wdw": jnp.repeat(w_dw, SUBLANES, axis=1),
        "bdw": b_dw[:, None], "clg": conv_ln_g[:, None], "clb": conv_ln_b[:, None],
        "a1": jnp.broadcast_to(ar, (depth, SUBLANES, D_STATE)),
        "a2": jnp.concatenate([jnp.broadcast_to(-ai, (depth, half, D_STATE)),
                               jnp.broadcast_to(ai, (depth, half, D_STATE))], axis=1),
        "bre": b_block(jnp.real(b_bar)), "bim": b_block(jnp.imag(b_bar)),
        "cre": c_block(ssm_c_re), "cim": c_block(-ssm_c_im),
        "d": ssm_d[:, None], "wglu": ssm_w_glu.astype(BF16), "bglu": ssm_b_glu[:, None],
        "w_out": w_out.astype(BF16), "g1": ln1_g[:, None], "b1": ln1_b[:, None],
        "w_up": w_up, "b_up": b_up[:, :, None, :], "w_down": w_down,
        "g2": ln2_g[:, None], "b2": ln2_b[:, None],
    }


def kernel(x_prompt, x_sample, cache_conv, state_ssm_re, state_ssm_im, cache_mem_k, cache_mem_v, mem_prompt, w_in, w_dw, b_dw, conv_ln_g, conv_ln_b, ssm_a_re, ssm_a_im, ssm_b_re, ssm_b_im, ssm_c_re, ssm_c_im, ssm_d, ssm_log_dt, ssm_w_glu, ssm_b_glu, w_mem_k, w_mem_v, w_out, ln1_g, ln1_b, w_router, b_router, w_up, b_up, w_down, ln2_g, ln2_b):
    depth = w_in.shape[0]
    alpha = (2.0 * depth) ** 0.25
    bp, seq, _ = x_prompt.shape
    bs, dec_seq, _ = x_sample.shape
    assert bp == SEQS_PER_STREAM and bs % SEQS_PER_STREAM == 0
    lc_p = min(128, seq)
    lc_s = dec_seq
    assert seq % lc_p == 0 and lc_p % 16 == 0 and lc_s % 16 == 0

    mb, mm, _ = mem_prompt.shape
    mk_all, mv_all = _memkv_call(mem_prompt.reshape(mb * mm, D_MODEL),
                                 w_mem_k.astype(BF16), w_mem_v.astype(BF16))
    mk_all = mk_all.reshape(depth, mb, mm, N_MEM_HEADS, MEM_HEAD_DIM)
    mv_all = mv_all.reshape(depth, mb, mm, N_MEM_HEADS, MEM_HEAD_DIM)

    perm_p, permt_p = _perm_matrices(lc_p)
    perm_s, permt_s = _perm_matrices(lc_s)
    wr = w_router.T.astype(BF16)
    br = b_router[:, None]

    tiles_p = seq // lc_p
    tiles_a = max(tiles_p // 2, 1)
    tiles_b = tiles_p - tiles_a
    steps_a = tiles_a * lc_p
    xa = xb = x_prompt
    xs = x_sample
    zero_hist = jnp.zeros((1, HIST_ROWS, D_CONV), F32)
    zero_h = jnp.zeros((1, SUBLANES, D_STATE), F32)
    zero_cnt = jnp.zeros((N_EXPERTS, 1), F32)

    params = _model_params(w_in, w_dw, b_dw, conv_ln_g, conv_ln_b, ssm_a_re, ssm_a_im, ssm_b_re,
                           ssm_b_im, ssm_c_re, ssm_c_im, ssm_d, ssm_log_dt, ssm_w_glu, ssm_b_glu,
                           w_out, ln1_g, ln1_b, w_up, b_up, w_down, ln2_g, ln2_b)
    nq_p = bp // SEQS_PER_STREAM
    nq_s = bs // SEQS_PER_STREAM
    head_shape = (mm, N_MEM_HEADS, MEM_HEAD_DIM)
    kp_p, vp_p = _pad_heads(mk_all.reshape((depth * mb,) + head_shape),
                            mv_all.reshape((depth * mb,) + head_shape))
    kp_s, vp_s = _pad_heads(cache_mem_k.reshape((depth * bs,) + head_shape),
                            cache_mem_v.reshape((depth * bs,) + head_shape))
    hist_s = _to_streams(cache_conv.reshape(depth * bs, CONV_BUF, D_CONV))
    h_s = _pack_state(state_ssm_re.reshape(depth * bs, N_SSM_GROUPS, SSM_STATE),
                      state_ssm_im.reshape(depth * bs, N_SSM_GROUPS, SSM_STATE))

    hists_p, hsts_p, hists_s, hsts_s = [], [], [], []
    moe_a = moe_b = moe_s = None
    for l in range(depth):
        lw = dict(params, layer=l)
        xa, hist, hst, rows_a, meta_a, metat_a, cnt_a = _mixer_call(
            xa, moe_a, zero_hist, zero_h, kp_p, vp_p, perm_p, permt_p, lw, wr, br, zero_cnt,
            lc=lc_p, alpha=alpha, tile_off=0, n_tiles=tiles_a, kv_off=l * nq_p)
        (yg_a,) = _moe_rows([rows_a], [metat_a], cnt_a, lw)
        moe_a = (yg_a, meta_a)
        xb, hist, hst, rows_b, meta_b, metat_b, cnt_b = _mixer_call(
            xb, moe_b, hist, hst, kp_p, vp_p, perm_p, permt_p, lw, wr, br, zero_cnt,
            lc=lc_p, alpha=alpha, tile_off=tiles_a, n_tiles=tiles_b, kv_off=l * nq_p)
        hists_p.append(hist)
        hsts_p.append(hst)
        xs, hist, hst, rows_s, meta_s, metat_s, cnt_s = _mixer_call(
            xs, moe_s, hist_s, h_s, kp_s, vp_s, perm_s, permt_s, lw, wr, br, cnt_b,
            lc=lc_s, alpha=alpha, state_off=l * nq_s, kv_off=l * nq_s)
        hists_s.append(hist)
        hsts_s.append(hst)
        yg_b, yg_s = _moe_rows([rows_b, rows_s], [metat_b, metat_s], cnt_s, lw)
        moe_b = (yg_b, meta_b)
        moe_s = (yg_s, meta_s)

    y_prompt = _combine_call(xa, moe_a[0], moe_a[1], lw, alpha=alpha, seq_len=seq)
    y_prompt = _combine_call(xb, moe_b[0], moe_b[1], lw, alpha=alpha, seq_len=seq, step_off=steps_a,
                             prev_out=y_prompt)
    y_sample = _combine_call(xs, moe_s[0], moe_s[1], lw, alpha=alpha, seq_len=dec_seq)

    def states(hists, hsts, batch):
        conv = _from_streams(jnp.concatenate(hists, axis=0), CONV_BUF)
        re, im = _unpack_state(jnp.concatenate(hsts, axis=0))
        return (conv.reshape(depth, batch, CONV_BUF, D_CONV),
                re.reshape(depth, batch, N_SSM_GROUPS, SSM_STATE),
                im.reshape(depth, batch, N_SSM_GROUPS, SSM_STATE))

    conv_p, re_p, im_p = states(hists_p, hsts_p, bp)
    conv_s, re_s, im_s = states(hists_s, hsts_s, bs)
    return (y_prompt, y_sample, conv_p, re_p, im_p, mk_all, mv_all, conv_s, re_s, im_s)
```

```python
import functools
import math

import numpy as np
import jax
import jax.numpy as jnp
from jax import lax
from jax.experimental import pallas as pl
from jax.experimental.pallas import tpu as pltpu
from jax.experimental.pallas import tpu_sc as plsc

F32 = jnp.float32
BF16 = jnp.bfloat16

D_MODEL = 1024
CONV_WIDTH = 31
CONV_BUF = CONV_WIDTH - 1
D_CONV = 384
D_SSM = 384
SSM_GROUP = 16
N_SSM_GROUPS = D_SSM // SSM_GROUP
SSM_STATE = 64
D_STATE = N_SSM_GROUPS * SSM_STATE
N_MEM = 256
N_MEM_HEADS = 4
MEM_HEAD_DIM = 64
D_ATT = N_MEM_HEADS * MEM_HEAD_DIM
D_IN = 2 * D_CONV + D_SSM + D_ATT
N_EXPERTS = 16
N_EXPERT_GROUPS = 4
EXPERTS_PER_GROUP = N_EXPERTS // N_EXPERT_GROUPS
D_EXPERT = 512
LN_EPS = 1e-5

SEQS_PER_STREAM = 4
HIST_ROWS = CONV_BUF * SEQS_PER_STREAM
SUBLANES = 8
LANES = 128
SCAN_LANES = 512
SSM_BLOCKS = D_STATE // SCAN_LANES
CONV_ROWS = 32
VMEM_LIMIT_BYTES = 56 * 1024 * 1024
HI_HALF_MASK = np.int32(-65536)
META_COLS = 8
SC_CORES = 2
SC_WORKERS = SC_CORES * 16
SC_MAX_CHUNK = 128


def _sigmoid(x):
    return 1.0 / (1.0 + jnp.exp(-x))


def _gelu_tanh(x):
    c = math.sqrt(2.0 / math.pi)
    return 0.5 * x * (1.0 + jnp.tanh(c * (x + 0.044715 * (x * x * x))))


def _layer_norm(z, g, b):
    mu = jnp.mean(z, axis=-1, keepdims=True)
    zc = z - mu
    var = jnp.mean(zc * zc, axis=-1, keepdims=True)
    return zc * lax.rsqrt(var + LN_EPS) * g + b


def _dot(a, b):
    return jnp.dot(a, b, preferred_element_type=F32)


def _mixer_kernel(*refs, tm, lc, alpha, fuse_in):
    refs = list(refs)
    if fuse_in:
        xprev_ref, yg_ref, metain_ref, g2p_ref, b2p_ref = refs[:5]
        refs = refs[5:]
    else:
        xprev_ref = refs.pop(0)
        xslab = refs.pop()
    (hist0_ref, h0_ref, k_ref, v_ref, perm_ref, permt_ref,
     w_in_ref, wdw_ref, bdw_ref, clg_ref, clb_ref, a1_ref, a2_ref,
     bre_ref, bim_ref, cre_ref, cim_ref, d_ref, wglu_ref, bglu_ref,
     wout_ref, g1_ref, b1_ref, wr_ref, br_ref, tri_ref, cnt0_ref,
     x1_ref, hist_out_ref, h_out_ref, xp_ref, meta_ref, metat_ref, cnt_ref,
     xpad0, xpad4, cy, ush, yim, bu_re, bu_im, hre, him, hcar, xin, running) = refs
    i = pl.program_id(1)

    @pl.when(i == 0)
    def _():
        xpad0[0:HIST_ROWS, :] = hist0_ref[0]
        hcar[...] = h0_ref[0]
        ush[...] = jnp.zeros_like(ush)

    @pl.when((i == 0) & (pl.program_id(0) == 0))
    def _():
        running[...] = cnt0_ref[...]

    if fuse_in:
        xin[...] = _moe_combine(xprev_ref[0], yg_ref, metain_ref[...], g2p_ref[...], b2p_ref[...], alpha)
    else:
        nslab = D_MODEL // LANES
        for s in range(SEQS_PER_STREAM):
            for j in range(nslab):
                xslab[j, pl.ds(s, lc, stride=SEQS_PER_STREAM), :] = xprev_ref[s, :, j * LANES:(j + 1) * LANES]
        xin[...] = jnp.concatenate([xslab[j] for j in range(nslab)], axis=1)
    x = xin[...]
    proj = _dot(x.astype(BF16), w_in_ref[...])

    g = proj[:, 0:D_CONV] * _sigmoid(proj[:, D_CONV:2 * D_CONV])
    xpad0[HIST_ROWS:HIST_ROWS + tm, :] = g
    xpad4[0:HIST_ROWS + tm - 4, :] = xpad0[4:HIST_ROWS + tm, :]

    nsub = CONV_ROWS // SUBLANES

    def conv_rows(rb, carry):
        r0 = pl.multiple_of(rb * CONV_ROWS, CONV_ROWS)
        accs = [jnp.broadcast_to(bdw_ref[...], (SUBLANES, D_CONV)) for _ in range(nsub)]
        for k in range(CONV_WIDTH):
            wk = wdw_ref[SUBLANES * k:SUBLANES * (k + 1), :]
            for sb in range(nsub):
                off = r0 + SEQS_PER_STREAM * k + SUBLANES * sb
                if k % 2 == 0:
                    xs = xpad0[pl.ds(pl.multiple_of(off, SUBLANES), SUBLANES), :]
                else:
                    xs = xpad4[pl.ds(pl.multiple_of(off - 4, SUBLANES), SUBLANES), :]
                accs[sb] = accs[sb] + xs * wk
        for sb in range(nsub):
            cy[pl.ds(pl.multiple_of(r0 + SUBLANES * sb, SUBLANES), SUBLANES), :] = accs[sb]
        return carry

    lax.fori_loop(0, tm // CONV_ROWS, conv_rows, 0, unroll=True)
    conv_n = _layer_norm(cy[...], clg_ref[...], clb_ref[...])
    cy[...] = conv_n * _sigmoid(conv_n)

    new_hist = xpad0[tm:tm + HIST_ROWS, :]
    xpad0[0:HIST_ROWS, :] = new_hist
    hist_out_ref[0] = new_hist

    u = proj[:, 2 * D_CONV:2 * D_CONV + D_SSM]
    ush[4:tm + 4, :] = u
    ub = u.astype(BF16)
    ub_sh = ush[...].astype(BF16)
    for m in range(SSM_BLOCKS):
        ch = slice(m * LANES, (m + 1) * LANES)
        st = slice(m * SCAN_LANES, (m + 1) * SCAN_LANES)
        bu_re[:, st] = _dot(ub[:, ch], bre_ref[m])
        bu_im[:, st] = _dot(ub_sh[:, ch], bim_ref[m])

    lo = lax.broadcasted_iota(jnp.int32, (SUBLANES, SCAN_LANES), 0) < 4
    for c in range(D_STATE // SCAN_LANES):
        cs = slice(c * SCAN_LANES, (c + 1) * SCAN_LANES)
        a1 = a1_ref[:, cs]
        a2 = a2_ref[:, cs]

        def scan_pair(j, carry, cs=cs, a1=a1, a2=a2):
            h_prev, im_cur = carry
            r = pl.multiple_of(j * SUBLANES, SUBLANES)
            re_cur = bu_re[pl.ds(r, SUBLANES), cs]
            im_next = bu_im[pl.ds(r + SUBLANES, SUBLANES), cs]
            p_even = jnp.where(lo, re_cur, im_cur)
            p_odd = jnp.where(lo, im_next, re_cur)
            h_even = a1 * pltpu.roll(h_prev, 4, 0) + a2 * h_prev + p_even
            h_odd = a1 * pltpu.roll(h_even, 4, 0) - a2 * h_even + p_odd
            hre[pl.ds(r, SUBLANES), cs] = jnp.where(lo, h_even, h_odd)
            him[pl.ds(r, SUBLANES), cs] = jnp.where(lo, h_prev, h_even)
            return h_odd, im_next

        h_last, _ = lax.fori_loop(0, lc // 2, scan_pair,
                                  (hcar[:, cs], bu_im[0:SUBLANES, cs]), unroll=True)
        him[tm:tm + SUBLANES, cs] = jnp.where(lo, h_last, 0.0)
        hcar[:, cs] = h_last
    h_out_ref[0] = hcar[...]

    y_re_blocks = []
    for m in range(SSM_BLOCKS):
        ch = slice(m * LANES, (m + 1) * LANES)
        st = slice(m * SCAN_LANES, (m + 1) * SCAN_LANES)
        y_re_blocks.append(_dot(hre[:, st].astype(BF16), cre_ref[m]))
        yim[:, ch] = _dot(him[:, st].astype(BF16), cim_ref[m])
    y_re = jnp.concatenate(y_re_blocks, axis=1)
    y = y_re + yim[4:tm + 4, :] + d_ref[...] * u
    y = _gelu_tanh(y)
    ssm_y = y * _sigmoid(_dot(y.astype(BF16), wglu_ref[...]) + bglu_ref[...])

    q = proj[:, 2 * D_CONV + D_SSM:D_IN].astype(BF16)
    q_seq = _dot(perm_ref[...], q)
    head_of_col = lax.shift_right_logical(lax.broadcasted_iota(jnp.int32, (1, D_ATT), 1), 6)
    head_masks = [jnp.where(head_of_col == h, 1.0, 0.0) for h in range(N_MEM_HEADS)]
    outs = []
    for s in range(SEQS_PER_STREAM):
        qs = q_seq[s * lc:(s + 1) * lc, :]
        ks = k_ref[0, s]
        vs = v_ref[0, s]
        acc = jnp.zeros((lc, D_ATT), F32)
        for h in range(N_MEM_HEADS):
            qh = (qs * head_masks[h]).astype(BF16)
            sc = _dot(qh, ks) * (MEM_HEAD_DIM ** -0.5)
            sc = sc - jnp.max(sc, axis=-1, keepdims=True)
            e = jnp.exp(sc)
            p = e * (1.0 / jnp.sum(e, axis=-1, keepdims=True))
            acc = acc + _dot(p.astype(BF16), vs) * head_masks[h]
        outs.append(acc)
    att_seq = jnp.concatenate(outs, axis=0).astype(BF16)
    att = _dot(permt_ref[...], att_seq).astype(BF16)

    mix = _dot(jnp.concatenate([cy[...].astype(BF16), ssm_y.astype(BF16), att], axis=1), wout_ref[...])
    x1 = _layer_norm(alpha * xin[...] + mix, g1_ref[...], b1_ref[...])
    x1_ref[0] = x1
    _route_rows(x1, wr_ref, br_ref, tri_ref, xp_ref, meta_ref, metat_ref, cnt_ref, running)


def _mixer_call(x, prev_moe, hist0, h0, kpad, vpad, perm, permt, lw, wr, br, cnt0, *, lc, alpha,
                tile_off=0, n_tiles=None, state_off=0, kv_off=0):
    fuse_in = prev_moe is not None
    tm = lc * SEQS_PER_STREAM
    if fuse_in:
        nq, rows, _ = x.shape
    else:
        nq = x.shape[0] // SEQS_PER_STREAM
        rows = (x.shape[1] // lc if n_tiles is None else n_tiles) * tm
    nt = rows // tm
    kern = functools.partial(_mixer_kernel, tm=tm, lc=lc, alpha=alpha, fuse_in=fuse_in)
    triu = jnp.asarray(np.triu(np.ones((tm, tm), np.float32), 1), BF16)

    def const(shape):
        return pl.BlockSpec(shape, lambda q, i: (0,) * len(shape))

    def flat(shape):
        return pl.BlockSpec(shape, lambda q, i: (0,) * (len(shape) - 2) + (q * nt + i, 0))

    layer = lw["layer"]

    def layered(shape, l=layer):
        return pl.BlockSpec((None,) + shape, lambda q, i: (l,) + (0,) * len(shape))

    operands = [x]
    if fuse_in:
        yg_prev, meta_prev = prev_moe
        in_specs = [pl.BlockSpec((1, tm, D_MODEL), lambda q, i: (q, i, 0)),
                    flat((2, tm, D_MODEL // 2)), flat((tm, META_COLS)),
                    layered((1, D_MODEL), layer - 1), layered((1, D_MODEL), layer - 1)]
        operands += [yg_prev, meta_prev, lw["g2"], lw["b2"]]
    else:
        in_specs = [pl.BlockSpec((SEQS_PER_STREAM, lc, D_MODEL), lambda q, i: (q, i + tile_off, 0))]
    in_specs += [
        pl.BlockSpec((1, HIST_ROWS, D_CONV), lambda q, i: (q + state_off, 0, 0)),
        pl.BlockSpec((1, SUBLANES, D_STATE), lambda q, i: (q + state_off, 0, 0)),
        pl.BlockSpec((1, SEQS_PER_STREAM, N_MEM, D_ATT), lambda q, i: (q + kv_off, 0, 0, 0)),
        pl.BlockSpec((1, SEQS_PER_STREAM, N_MEM, D_ATT), lambda q, i: (q + kv_off, 0, 0, 0)),
        const((tm, tm)), const((tm, tm)),
        layered((D_MODEL, D_IN)),
        layered((CONV_WIDTH * SUBLANES, D_CONV)), layered((1, D_CONV)), layered((1, D_CONV)),
        layered((1, D_CONV)),
        layered((SUBLANES, D_STATE)), layered((SUBLANES, D_STATE)),
        layered((SSM_BLOCKS, LANES, SCAN_LANES)), layered((SSM_BLOCKS, LANES, SCAN_LANES)),
        layered((SSM_BLOCKS, SCAN_LANES, LANES)), layered((SSM_BLOCKS, SCAN_LANES, LANES)),
        layered((1, D_SSM)), layered((D_SSM, D_SSM)), layered((1, D_SSM)),
        layered((D_MODEL, D_MODEL)), layered((1, D_MODEL)), layered((1, D_MODEL)),
        const((N_EXPERTS, D_MODEL)), const((N_EXPERTS, 1)), const((tm, tm)), const((N_EXPERTS, 1)),
    ]
    out_specs = [
        pl.BlockSpec((1, tm, D_MODEL), lambda q, i: (q, i, 0)),
        pl.BlockSpec((1, HIST_ROWS, D_CONV), lambda q, i: (q, 0, 0)),
        pl.BlockSpec((1, SUBLANES, D_STATE), lambda q, i: (q, 0, 0)),
        flat((tm, D_MODEL // 2)), flat((tm, META_COLS)),
        (pl.BlockSpec((META_COLS, tm), lambda q, i: (0, q * nt + i)) if tm % LANES == 0 else
         pl.BlockSpec((None, META_COLS, tm), lambda q, i: (q * nt + i, 0, 0))),
        const((N_EXPERTS, 1)),
    ]
    out_shape = [
        jax.ShapeDtypeStruct((nq, rows, D_MODEL), F32),
        jax.ShapeDtypeStruct((nq, HIST_ROWS, D_CONV), F32),
        jax.ShapeDtypeStruct((nq, SUBLANES, D_STATE), F32),
        jax.ShapeDtypeStruct((nq * rows, D_MODEL // 2), jnp.int32),
        jax.ShapeDtypeStruct((nq * rows, META_COLS), F32),
        jax.ShapeDtypeStruct((META_COLS, nq * rows) if tm % LANES == 0 else (nq * nt, META_COLS, tm), F32),
        jax.ShapeDtypeStruct((N_EXPERTS, 1), F32),
    ]
    scratch = [
        pltpu.VMEM((HIST_ROWS + tm + SUBLANES, D_CONV), F32),
        pltpu.VMEM((HIST_ROWS + tm + SUBLANES, D_CONV), F32),
        pltpu.VMEM((tm, D_CONV), F32),
        pltpu.VMEM((tm + SUBLANES, D_SSM), F32),
        pltpu.VMEM((tm + SUBLANES, D_SSM), F32),
        pltpu.VMEM((tm, D_STATE), F32),
        pltpu.VMEM((tm + SUBLANES, D_STATE), F32),
        pltpu.VMEM((tm, D_STATE), F32),
        pltpu.VMEM((tm + SUBLANES, D_STATE), F32),
        pltpu.VMEM((SUBLANES, D_STATE), F32),
        pltpu.VMEM((tm, D_MODEL), F32),
        pltpu.VMEM((N_EXPERTS, 1), F32),
    ]
    if not fuse_in:
        scratch.append(pltpu.VMEM((D_MODEL // LANES, tm, LANES), F32))
    return pl.pallas_call(
        kern,
        grid=(nq, nt),
        in_specs=in_specs,
        out_specs=out_specs,
        out_shape=out_shape,
        scratch_shapes=scratch,
        compiler_params=pltpu.CompilerParams(
            dimension_semantics=("arbitrary", "arbitrary"),
            vmem_limit_bytes=VMEM_LIMIT_BYTES),
        name="mixer",
    )(*operands, hist0, h0, kpad, vpad, perm, permt,
      lw["w_in"], lw["wdw"], lw["bdw"], lw["clg"], lw["clb"], lw["a1"], lw["a2"],
      lw["bre"], lw["bim"], lw["cre"], lw["cim"], lw["d"], lw["wglu"], lw["bglu"],
      lw["w_out"], lw["g1"], lw["b1"], wr, br, triu, cnt0)


def _route(logits_t):
    m = jnp.max(logits_t, axis=0, keepdims=True)
    e = jnp.exp(logits_t - m)
    aff = e / jnp.sum(e, axis=0, keepdims=True)
    rows = [aff[j:j + 1, :] for j in range(N_EXPERTS)]

    scores = []
    for gi in range(N_EXPERT_GROUPS):
        a, b, c, d = rows[EXPERTS_PER_GROUP * gi:EXPERTS_PER_GROUP * (gi + 1)]
        hi1, lo1 = jnp.maximum(a, b), jnp.minimum(a, b)
        hi2, lo2 = jnp.maximum(c, d), jnp.minimum(c, d)
        scores.append(jnp.maximum(hi1, hi2) + jnp.maximum(jnp.minimum(hi1, hi2), jnp.maximum(lo1, lo2)))
    best = scores[0]
    sel = jnp.zeros_like(best)
    for gi in range(1, N_EXPERT_GROUPS):
        better = scores[gi] > best
        sel = jnp.where(better, float(gi), sel)
        best = jnp.where(better, scores[gi], best)

    hot1, hot2 = [], []
    for gi in range(N_EXPERT_GROUPS):
        vals = rows[EXPERTS_PER_GROUP * gi:EXPERTS_PER_GROUP * (gi + 1)]
        chosen = sel == float(gi)
        for j in range(EXPERTS_PER_GROUP):
            ahead = jnp.zeros_like(best)
            for k in range(EXPERTS_PER_GROUP):
                if k < j:
                    ahead = ahead + jnp.where(vals[k] >= vals[j], 1.0, 0.0)
                elif k > j:
                    ahead = ahead + jnp.where(vals[k] > vals[j], 1.0, 0.0)
            hot1.append(jnp.where(chosen, jnp.where(ahead == 0.0, 1.0, 0.0), 0.0))
            hot2.append(jnp.where(chosen, jnp.where(ahead == 1.0, 1.0, 0.0), 0.0))
    return jnp.concatenate(hot1, axis=0), jnp.concatenate(hot2, axis=0), aff


def _pack_halves(y):
    half = y.shape[1] // 2
    lo = lax.bitcast_convert_type(y[:, :half].astype(BF16).astype(F32), jnp.int32)
    hi = lax.bitcast_convert_type(y[:, half:].astype(BF16).astype(F32), jnp.int32)
    return lax.shift_right_logical(lo, 16) | (hi & HI_HALF_MASK)


def _unpack_halves(w):
    lo = lax.bitcast_convert_type(lax.shift_left(w, 16), F32)
    hi = lax.bitcast_convert_type(w & HI_HALF_MASK, F32)
    return lo, hi


def _route_rows(x, wr_ref, br_ref, triu_ref, xp_ref, meta_ref, metat_ref, cnt_ref, running):
    tm = x.shape[0]
    xp_ref[...] = _pack_halves(x)
    logits_t = lax.dot_general(wr_ref[...], x.astype(BF16), (((1,), (1,)), ((), ())),
                               preferred_element_type=F32) + br_ref[...]
    hot1, hot2, aff = _route(logits_t)
    eid = lax.broadcasted_iota(jnp.int32, (N_EXPERTS, tm), 0).astype(F32)
    both = hot1 + hot2
    before = _dot(both.astype(BF16), triu_ref[...]) + running[...]

    def pick(hot, vals):
        return jnp.sum(hot * vals, axis=0, keepdims=True)

    v1 = pick(hot1, aff)
    v2 = pick(hot2, aff)
    denom = v1 + v2
    meta_t = jnp.concatenate(
        [pick(hot1, eid), pick(hot2, eid), v1 / denom, v2 / denom, pick(hot1, before), pick(hot2, before),
         jnp.zeros((LANES - 6, tm), F32)], axis=0)
    metat_ref[...] = meta_t[0:META_COLS, :]
    meta_ref[...] = meta_t.T[:, 0:META_COLS]
    running[...] = running[...] + jnp.sum(both, axis=1, keepdims=True)
    cnt_ref[...] = running[...]


def _expert_kernel(te_ref, nxt_ref, first_ref, nt_ref, xs_ref, wup_hbm, bup_ref, wdn_hbm, ys_ref,
                   wup_f32, wdn_f32, wup_bf, wdn_bf, sems, slot_ref, *, layer):
    i = pl.program_id(0)

    def weight_copies(e, slot):
        return (pltpu.make_async_copy(wup_hbm.at[layer, e], wup_f32.at[slot], sems.at[0, slot]),
                pltpu.make_async_copy(wdn_hbm.at[layer, e], wdn_f32.at[slot], sems.at[1, slot]))

    @pl.when(i == 0)
    def _():
        slot_ref[0] = 0
        for c in weight_copies(te_ref[0], 0):
            c.start()

    @pl.when(first_ref[i] == 1)
    def _():
        slot = slot_ref[0]
        for c in weight_copies(te_ref[i], slot):
            c.wait()

        @pl.when(nxt_ref[i] >= 0)
        def _():
            for c in weight_copies(nxt_ref[i], 1 - slot):
                c.start(priority=1)

        wup_bf[...] = wup_f32[slot].astype(BF16)
        wdn_bf[...] = wdn_f32[slot].astype(BF16)
        slot_ref[0] = 1 - slot

    @pl.when(i < nt_ref[0])
    def _():
        half = D_MODEL // 2
        lo, hi = _unpack_halves(xs_ref[...])
        h = (_dot(lo.astype(BF16), wup_bf[0:half, :])
             + _dot(hi.astype(BF16), wup_bf[half:D_MODEL, :]) + bup_ref[0])
        ys_ref[...] = _pack_halves(_dot(_gelu_tanh(h).astype(BF16), wdn_bf[...]))


def _expert_call(xs, tile_expert, next_expert, first_tile, n_tiles, lw, *, rt):
    rows = xs.shape[0]

    def row_map(i, te, nxt, first, nt):
        return (jnp.minimum(i, nt[0] - 1), 0)

    return pl.pallas_call(
        functools.partial(_expert_kernel, layer=lw["layer"]),
        grid_spec=pltpu.PrefetchScalarGridSpec(
            num_scalar_prefetch=4,
            grid=(rows // rt,),
            in_specs=[
                pl.BlockSpec((rt, D_MODEL // 2), row_map),
                pl.BlockSpec(memory_space=pl.ANY),
                pl.BlockSpec((None, 1, 1, D_EXPERT),
                             lambda i, te, nxt, first, nt: (lw["layer"], te[i], 0, 0)),
                pl.BlockSpec(memory_space=pl.ANY),
            ],
            out_specs=pl.BlockSpec((rt, D_MODEL // 2), row_map),
            scratch_shapes=[pltpu.VMEM((2, D_MODEL, D_EXPERT), F32),
                            pltpu.VMEM((2, D_EXPERT, D_MODEL), F32),
                            pltpu.VMEM((D_MODEL, D_EXPERT), BF16),
                            pltpu.VMEM((D_EXPERT, D_MODEL), BF16),
                            pltpu.SemaphoreType.DMA((2, 2)),
                            pltpu.SMEM((1,), jnp.int32)],
        ),
        out_shape=jax.ShapeDtypeStruct((rows, D_MODEL // 2), jnp.int32),
        compiler_params=pltpu.CompilerParams(
            dimension_semantics=("arbitrary",), vmem_limit_bytes=VMEM_LIMIT_BYTES),
        name="experts",
    )(tile_expert, next_expert, first_tile, n_tiles, xs, lw["w_up"], lw["b_up"], lw["w_down"])


def _moe_combine(x1, yg_ref, meta, g, b, alpha):
    g1 = meta[:, 2:3]
    g2 = meta[:, 3:4]
    lo1, hi1 = _unpack_halves(yg_ref[0])
    lo2, hi2 = _unpack_halves(yg_ref[1])
    moe = jnp.concatenate([g1 * lo1 + g2 * lo2, g1 * hi1 + g2 * hi2], axis=1)
    return _layer_norm(alpha * x1 + moe, g, b)


def _combine_kernel(x_ref, yg_ref, meta_ref, g2_ref, b2_ref, *rest, alpha):
    o_ref, yslab = rest[-2:]
    y = _moe_combine(x_ref[...], yg_ref, meta_ref[...], g2_ref[...], b2_ref[...], alpha)
    nslab = D_MODEL // LANES
    steps = y.shape[0] // SEQS_PER_STREAM
    for j in range(nslab):
        yslab[j] = y[:, j * LANES:(j + 1) * LANES]
    for s in range(SEQS_PER_STREAM):
        for j in range(nslab):
            o_ref[s, :, j * LANES:(j + 1) * LANES] = yslab[j, pl.ds(s, steps, stride=SEQS_PER_STREAM), :]


def _combine_call(x1, yg, meta, lw, *, alpha, seq_len, step_off=0, prev_out=None):
    nq, srows, _ = x1.shape
    tmc = min(1024, srows)
    tps = srows // tmc
    steps = tmc // SEQS_PER_STREAM
    blk_off = step_off // steps
    in_specs = [
        pl.BlockSpec((tmc, D_MODEL), lambda i: (i, 0)),
        pl.BlockSpec((2, tmc, D_MODEL // 2), lambda i: (0, i, 0)),
        pl.BlockSpec((tmc, META_COLS), lambda i: (i, 0)),
        pl.BlockSpec((None, 1, D_MODEL), lambda i: (lw["layer"], 0, 0)),
        pl.BlockSpec((None, 1, D_MODEL), lambda i: (lw["layer"], 0, 0)),
    ]
    operands = [x1.reshape(nq * srows, D_MODEL), yg, meta, lw["g2"], lw["b2"]]
    aliases = {}
    if prev_out is not None:
        in_specs.append(pl.BlockSpec(memory_space=pl.ANY))
        operands.append(prev_out)
        aliases = {len(operands) - 1: 0}
    return pl.pallas_call(
        functools.partial(_combine_kernel, alpha=alpha),
        grid=(nq * tps,),
        in_specs=in_specs,
        out_specs=pl.BlockSpec((SEQS_PER_STREAM, steps, D_MODEL),
                               lambda i: (i // tps, i % tps + blk_off, 0)),
        out_shape=jax.ShapeDtypeStruct((nq * SEQS_PER_STREAM, seq_len, D_MODEL), F32),
        scratch_shapes=[pltpu.VMEM((D_MODEL // LANES, tmc, LANES), F32)],
        input_output_aliases=aliases,
        compiler_params=pltpu.CompilerParams(
            dimension_semantics=("arbitrary",), vmem_limit_bytes=VMEM_LIMIT_BYTES),
        name="combine",
    )(*operands)


def _sc_mesh():
    return plsc.VectorSubcoreMesh(core_axis_name="c", subcore_axis_name="s")


def _sc_chunk(rows):
    per_worker = rows // SC_WORKERS
    chunk = min(SC_MAX_CHUNK, per_worker)
    assert per_worker % chunk == 0 and chunk % 8 == 0
    return per_worker // chunk, chunk


def _sc_dispatch(xps, poss, n_slots):
    width = xps[0].shape[1]
    ngroups = len(xps)
    plans = [_sc_chunk(x.shape[0]) for x in xps]
    scratch = []
    for _, ch in plans:
        scratch += [pltpu.VMEM((ch,), jnp.int32), pltpu.VMEM((ch,), jnp.int32),
                    pltpu.VMEM((ch, width), jnp.int32)]

    @functools.partial(
        pl.kernel, mesh=_sc_mesh(),
        out_type=jax.ShapeDtypeStruct((n_slots, width), jnp.int32),
        scratch_types=scratch + [pltpu.SemaphoreType.DMA],
        name="sc_dispatch")
    def k(*refs):
        x_hbms, pos_hbms, o_hbm = refs[:ngroups], refs[ngroups:2 * ngroups], refs[2 * ngroups]
        bufs, sem = refs[2 * ngroups + 1:-1], refs[-1]
        wid = lax.axis_index("s") * SC_CORES + lax.axis_index("c")
        for g, (nch, ch) in enumerate(plans):
            idx0, idx1, buf = bufs[3 * g:3 * g + 3]

            @pl.loop(0, nch)
            def _(c, g=g, nch=nch, ch=ch, idx0=idx0, idx1=idx1, buf=buf):
                base = (wid * nch + c) * ch
                pltpu.sync_copy(x_hbms[g].at[pl.ds(base, ch)], buf)
                pltpu.sync_copy(pos_hbms[g].at[0, pl.ds(base, ch)], idx0)
                pltpu.sync_copy(pos_hbms[g].at[1, pl.ds(base, ch)], idx1)
                pltpu.async_copy(buf, o_hbm.at[idx0], sem).wait()
                pltpu.async_copy(buf, o_hbm.at[idx1], sem).wait()

    return k(*xps, *poss)


def _sc_gather(ys, poss):
    width = ys.shape[1]
    ngroups = len(poss)
    plans = [_sc_chunk(p.shape[1]) for p in poss]
    scratch = []
    for _, ch in plans:
        scratch += [pltpu.VMEM((ch,), jnp.int32), pltpu.VMEM((ch, width), jnp.int32)]

    @functools.partial(
        pl.kernel, mesh=_sc_mesh(),
        out_type=[jax.ShapeDtypeStruct((2, p.shape[1], width), jnp.int32) for p in poss],
        scratch_types=scratch + [pltpu.SemaphoreType.DMA],
        name="sc_gather")
    def k(*refs):
        y_hbm, pos_hbms = refs[0], refs[1:1 + ngroups]
        o_hbms = refs[1 + ngroups:1 + 2 * ngroups]
        bufs, sem = refs[1 + 2 * ngroups:-1], refs[-1]
        wid = lax.axis_index("s") * SC_CORES + lax.axis_index("c")
        for g, (nch, ch) in enumerate(plans):
            idx, buf = bufs[2 * g:2 * g + 2]

            @pl.loop(0, nch)
            def _(c, g=g, nch=nch, ch=ch, idx=idx, buf=buf):
                base = (wid * nch + c) * ch
                for kk in range(2):
                    pltpu.sync_copy(pos_hbms[g].at[kk, pl.ds(base, ch)], idx)
                    pltpu.async_copy(y_hbm.at[idx], buf, sem).wait()
                    pltpu.sync_copy(buf, o_hbms[g].at[kk, pl.ds(base, ch)])

    return k(ys, *poss)


def _moe_rows(xps, meta_ts, counts, lw):
    total = sum(x.shape[0] for x in xps)
    rt = 512 if total >= 8192 else 128
    n_slots = 2 * total + N_EXPERTS * rt
    nt_max = n_slots // rt

    cnt = counts[:, 0].astype(jnp.int32)
    padded = ((cnt + rt - 1) // rt) * rt
    ends = jnp.cumsum(padded)
    offs = ends - padded
    experts = jnp.arange(N_EXPERTS, dtype=jnp.int32)[:, None]
    poss = []
    for x, meta_t in zip(xps, meta_ts):
        if meta_t.ndim == 3:
            meta_t = meta_t.transpose(1, 0, 2).reshape(META_COLS, x.shape[0])
        eidx = meta_t[0:2].astype(jnp.int32)
        rank = meta_t[4:6].astype(jnp.int32)
        poss.append(jnp.sum(jnp.where(eidx[:, None, :] == experts, offs[:, None], 0), axis=1) + rank)
    n_tiles = ends[-1:] // rt
    tiles = jnp.minimum(jnp.arange(nt_max, dtype=jnp.int32), n_tiles[0] - 1)
    tile_expert = jnp.minimum(jnp.sum((ends // rt)[None, :] <= tiles[:, None], axis=1),
                              N_EXPERTS - 1).astype(jnp.int32)

    first_tile = jnp.concatenate([jnp.ones((1,), jnp.int32),
                                  (tile_expert[1:] != tile_expert[:-1]).astype(jnp.int32)])
    eid = jnp.arange(N_EXPERTS, dtype=jnp.int32)
    later = (eid[None, :] > eid[:, None]) & (cnt[None, :] > 0)
    next_of = jnp.min(jnp.where(later, eid[None, :], N_EXPERTS), axis=1)
    next_of = jnp.where(next_of == N_EXPERTS, -1, next_of)
    next_expert = jnp.sum(jnp.where(tile_expert[:, None] == eid[None, :], next_of[None, :], 0),
                          axis=1).astype(jnp.int32)

    xs = _sc_dispatch(xps, poss, n_slots)
    ys = _expert_call(xs, tile_expert, next_expert, first_tile, n_tiles.astype(jnp.int32), lw, rt=rt)
    return _sc_gather(ys, poss)


def _memkv_kernel(m_ref, wk_ref, wv_ref, k_ref, v_ref):
    mb = m_ref[...].astype(BF16)
    k_ref[0] = _dot(mb, wk_ref[0])
    v_ref[0] = _dot(mb, wv_ref[0])


def _memkv_call(mem, wk, wv):
    depth = wk.shape[0]
    rows = mem.shape[0]
    return pl.pallas_call(
        _memkv_kernel,
        grid=(depth,),
        in_specs=[
            pl.BlockSpec((rows, D_MODEL), lambda l: (0, 0)),
            pl.BlockSpec((1, D_MODEL, D_ATT), lambda l: (l, 0, 0)),
            pl.BlockSpec((1, D_MODEL, D_ATT), lambda l: (l, 0, 0)),
        ],
        out_specs=[
            pl.BlockSpec((1, rows, D_ATT), lambda l: (l, 0, 0)),
            pl.BlockSpec((1, rows, D_ATT), lambda l: (l, 0, 0)),
        ],
        out_shape=[jax.ShapeDtypeStruct((depth, rows, D_ATT), F32)] * 2,
        compiler_params=pltpu.CompilerParams(
            dimension_semantics=("arbitrary",), vmem_limit_bytes=VMEM_LIMIT_BYTES),
        name="memkv",
    )(mem, wk, wv)


def _perm_matrices(lc):
    tm = lc * SEQS_PER_STREAM
    p = np.zeros((tm, tm), np.float32)
    for s in range(SEQS_PER_STREAM):
        for t in range(lc):
            p[s * lc + t, t * SEQS_PER_STREAM + s] = 1.0
    return jnp.asarray(p, BF16), jnp.asarray(p.T, BF16)


def _to_streams(a):
    b, l, c = a.shape
    q = b // SEQS_PER_STREAM
    return a.reshape(q, SEQS_PER_STREAM, l, c).transpose(0, 2, 1, 3).reshape(q, l * SEQS_PER_STREAM, c)


def _from_streams(a, l):
    q, _, c = a.shape
    return a.reshape(q, l, SEQS_PER_STREAM, c).transpose(0, 2, 1, 3).reshape(q * SEQS_PER_STREAM, l, c)


def _pack_state(re, im):
    b = re.shape[0]
    q = b // SEQS_PER_STREAM
    re = re.reshape(q, SEQS_PER_STREAM, D_STATE)
    im = im.reshape(q, SEQS_PER_STREAM, D_STATE)
    return jnp.concatenate([im, re], axis=1)


def _unpack_state(h):
    q = h.shape[0]
    im = h[:, 0:SEQS_PER_STREAM].reshape(q * SEQS_PER_STREAM, N_SSM_GROUPS, SSM_STATE)
    re = h[:, SEQS_PER_STREAM:].reshape(q * SEQS_PER_STREAM, N_SSM_GROUPS, SSM_STATE)
    return re, im


def _pad_heads(mk, mv):
    b = mk.shape[0]
    q = b // SEQS_PER_STREAM
    shape = (q, SEQS_PER_STREAM, N_MEM, D_ATT)
    return mk.reshape(shape).transpose(0, 1, 3, 2).astype(BF16), mv.reshape(shape).astype(BF16)


def _model_params(w_in, w_dw, b_dw, conv_ln_g, conv_ln_b, ssm_a_re, ssm_a_im, ssm_b_re,
                  ssm_b_im, ssm_c_re, ssm_c_im, ssm_d, ssm_log_dt, ssm_w_glu, ssm_b_glu,
                  w_out, ln1_g, ln1_b, w_up, b_up, w_down, ln2_g, ln2_b):
    depth = w_in.shape[0]
    a = lax.complex(ssm_a_re, ssm_a_im)
    dt = jnp.exp(ssm_log_dt)[..., None]
    a_bar = jnp.exp(a * dt)
    b_bar = ((a_bar - 1.0) / a)[..., None] * lax.complex(ssm_b_re, ssm_b_im)
    eye = jnp.eye(N_SSM_GROUPS, dtype=F32)

    def b_block(m):
        full = jnp.einsum("lgpi,gh->lgihp", m, eye).reshape(depth, D_SSM, D_STATE).astype(BF16)
        return jnp.stack([full[:, k * LANES:(k + 1) * LANES, k * SCAN_LANES:(k + 1) * SCAN_LANES]
                          for k in range(SSM_BLOCKS)], axis=1)

    def c_block(m):
        full = jnp.einsum("lgip,gh->lgphi", m, eye).reshape(depth, D_STATE, D_SSM).astype(BF16)
        return jnp.stack([full[:, k * SCAN_LANES:(k + 1) * SCAN_LANES, k * LANES:(k + 1) * LANES]
                          for k in range(SSM_BLOCKS)], axis=1)

    ar = jnp.real(a_bar).reshape(depth, 1, D_STATE)
    ai = jnp.imag(a_bar).reshape(depth, 1, D_STATE)
    half = SUBLANES // 2
    return {
        "w_in": w_in.astype(BF16),
        "wdw": jnp.repeat(w_dw, SUBLANES, axis=1),
        "bdw": b_dw[:, None], "clg": conv_ln_g[:, None], "clb": conv_ln_b[:, None],
        "a1": jnp.broadcast_to(ar, (depth, SUBLANES, D_STATE)),
        "a2": jnp.concatenate([jnp.broadcast_to(-ai, (depth, half, D_STATE)),
                               jnp.broadcast_to(ai, (depth, half, D_STATE))], axis=1),
        "bre": b_block(jnp.real(b_bar)), "bim": b_block(jnp.imag(b_bar)),
        "cre": c_block(ssm_c_re), "cim": c_block(-ssm_c_im),
        "d": ssm_d[:, None], "wglu": ssm_w_glu.astype(BF16), "bglu": ssm_b_glu[:, None],
        "w_out": w_out.astype(BF16), "g1": ln1_g[:, None], "b1": ln1_b[:, None],
        "w_up": w_up, "b_up": b_up[:, :, None, :], "w_down": w_down,
        "g2": ln2_g[:, None], "b2": ln2_b[:, None],
    }


def kernel(x_prompt, x_sample, cache_conv, state_ssm_re, state_ssm_im, cache_mem_k, cache_mem_v, mem_prompt, w_in, w_dw, b_dw, conv_ln_g, conv_ln_b, ssm_a_re, ssm_a_im, ssm_b_re, ssm_b_im, ssm_c_re, ssm_c_im, ssm_d, ssm_log_dt, ssm_w_glu, ssm_b_glu, w_mem_k, w_mem_v, w_out, ln1_g, ln1_b, w_router, b_router, w_up, b_up, w_down, ln2_g, ln2_b):
    depth = w_in.shape[0]
    alpha = (2.0 * depth) ** 0.25
    bp, seq, _ = x_prompt.shape
    bs, dec_seq, _ = x_sample.shape
    assert bp == SEQS_PER_STREAM and bs % SEQS_PER_STREAM == 0
    lc_p = min(128, seq)
    lc_s = dec_seq
    assert seq % lc_p == 0 and lc_p % 16 == 0 and lc_s % 16 == 0

    mb, mm, _ = mem_prompt.shape
    mk_all, mv_all = _memkv_call(mem_prompt.reshape(mb * mm, D_MODEL),
                                 w_mem_k.astype(BF16), w_mem_v.astype(BF16))
    mk_all = mk_all.reshape(depth, mb, mm, N_MEM_HEADS, MEM_HEAD_DIM)
    mv_all = mv_all.reshape(depth, mb, mm, N_MEM_HEADS, MEM_HEAD_DIM)

    perm_p, permt_p = _perm_matrices(lc_p)
    perm_s, permt_s = _perm_matrices(lc_s)
    wr = w_router.T.astype(BF16)
    br = b_router[:, None]

    tiles_p = seq // lc_p
    tiles_a = max(tiles_p // 2, 1)
    tiles_b = tiles_p - tiles_a
    steps_a = tiles_a * lc_p
    xa = xb = x_prompt
    xs = x_sample
    zero_hist = jnp.zeros((1, HIST_ROWS, D_CONV), F32)
    zero_h = jnp.zeros((1, SUBLANES, D_STATE), F32)
    zero_cnt = jnp.zeros((N_EXPERTS, 1), F32)

    params = _model_params(w_in, w_dw, b_dw, conv_ln_g, conv_ln_b, ssm_a_re, ssm_a_im, ssm_b_re,
                           ssm_b_im, ssm_c_re, ssm_c_im, ssm_d, ssm_log_dt, ssm_w_glu, ssm_b_glu,
                           w_out, ln1_g, ln1_b, w_up, b_up, w_down, ln2_g, ln2_b)
    nq_p = bp // SEQS_PER_STREAM
    nq_s = bs // SEQS_PER_STREAM
    head_shape = (mm, N_MEM_HEADS, MEM_HEAD_DIM)
    kp_p, vp_p = _pad_heads(mk_all.reshape((depth * mb,) + head_shape),
                            mv_all.reshape((depth * mb,) + head_shape))
    kp_s, vp_s = _pad_heads(cache_mem_k.reshape((depth * bs,) + head_shape),
                            cache_mem_v.reshape((depth * bs,) + head_shape))
    hist_s = _to_streams(cache_conv.reshape(depth * bs, CONV_BUF, D_CONV))
    h_s = _pack_state(state_ssm_re.reshape(depth * bs, N_SSM_GROUPS, SSM_STATE),
                      state_ssm_im.reshape(depth * bs, N_SSM_GROUPS, SSM_STATE))

    hists_p, hsts_p, hists_s, hsts_s = [], [], [], []
    moe_a = moe_b = moe_s = None
    for l in range(depth):
        lw = dict(params, layer=l)
        xa, hist, hst, rows_a, meta_a, metat_a, cnt_a = _mixer_call(
            xa, moe_a, zero_hist, zero_h, kp_p, vp_p, perm_p, permt_p, lw, wr, br, zero_cnt,
            lc=lc_p, alpha=alpha, tile_off=0, n_tiles=tiles_a, kv_off=l * nq_p)
        (yg_a,) = _moe_rows([rows_a], [metat_a], cnt_a, lw)
        moe_a = (yg_a, meta_a)
        xb, hist, hst, rows_b, meta_b, metat_b, cnt_b = _mixer_call(
            xb, moe_b, hist, hst, kp_p, vp_p, perm_p, permt_p, lw, wr, br, zero_cnt,
            lc=lc_p, alpha=alpha, tile_off=tiles_a, n_tiles=tiles_b, kv_off=l * nq_p)
        hists_p.append(hist)
        hsts_p.append(hst)
        xs, hist, hst, rows_s, meta_s, metat_s, cnt_s = _mixer_call(
            xs, moe_s, hist_s, h_s, kp_s, vp_s, perm_s, permt_s, lw, wr, br, cnt_b,
            lc=lc_s, alpha=alpha, state_off=l * nq_s, kv_off=l * nq_s)
        hists_s.append(hist)
        hsts_s.append(hst)
        yg_b, yg_s = _moe_rows([rows_b, rows_s], [metat_b, metat_s], cnt_s, lw)
        moe_b = (yg_b, meta_b)
        moe_s = (yg_s, meta_s)

    y_prompt = _combine_call(xa, moe_a[0], moe_a[1], lw, alpha=alpha, seq_len=seq)
    y_prompt = _combine_call(xb, moe_b[0], moe_b[1], lw, alpha=alpha, seq_len=seq, step_off=steps_a,
                             prev_out=y_prompt)
    y_sample = _combine_call(xs, moe_s[0], moe_s[1], lw, alpha=alpha, seq_len=dec_seq)

    def states(hists, hsts, batch):
        conv = _from_streams(jnp.concatenate(hists, axis=0), CONV_BUF)
        re, im = _unpack_state(jnp.concatenate(hsts, axis=0))
        return (conv.reshape(depth, batch, CONV_BUF, D_CONV),
                re.reshape(depth, batch, N_SSM_GROUPS, SSM_STATE),
                im.reshape(depth, batch, N_SSM_GROUPS, SSM_STATE))

    conv_p, re_p, im_p = states(hists_p, hsts_p, bp)
    conv_s, re_s, im_s = states(hists_s, hsts_s, bs)
    return (y_prompt, y_sample, conv_p, re_p, im_p, mk_all, mv_all, conv_s, re_s, im_s)
```
